```python
import math
import jax, jax.numpy as jnp
from jax import lax
import numpy as np

D_MODEL = 4096
BATCH = 4
SEQ = 2048
DEPTH = 2
DEC_BATCH = 32
DEC_SEQ = 32
PAST_LEN = 2048

CHUNK = 64
MIX_WIDTH = D_MODEL
SSD_INNER = MIX_WIDTH // 2
SSD_HD = 64
SSD_HEADS = SSD_INNER // SSD_HD
SSD_GROUPS = 8
SSD_STATE = 128
CONV_W = 4
SSD_CONV_DIM = SSD_INNER + 2 * SSD_GROUPS * SSD_STATE
HG_WIDTH = MIX_WIDTH // 4
HG_HD = 128
HG_HEADS = HG_WIDTH // HG_HD
HG_BLOCK = 16
SWA_WIDTH = MIX_WIDTH - SSD_INNER - HG_WIDTH
SWA_HD = 64
SWA_HEADS = SWA_WIDTH // SWA_HD
SWA_KV_HEADS = 2
SWA_REP = SWA_HEADS // SWA_KV_HEADS
WINDOW = 128
WIN_CHUNKS = WINDOW // CHUNK
MEM_TOKENS = 256
MEM_HEADS = 4
MEM_HD = 128
MEM_WIDTH = MEM_HEADS * MEM_HD
N_KEYS = 128
N_EXPERTS = N_KEYS * N_KEYS
PEER_HEADS = 8
PEER_QDIM = 256
PEER_TOPK = 16
PEER_BLOCK = 64
EPS = 1e-6
NEG = -1e30
TINY = 1e-30
IN_SIZES = (SSD_INNER, SSD_CONV_DIM, SSD_HEADS, HG_WIDTH, HG_WIDTH, HG_WIDTH, HG_WIDTH,
            SWA_HEADS * SWA_HD, SWA_KV_HEADS * SWA_HD, SWA_KV_HEADS * SWA_HD)
IN_COLS = sum(IN_SIZES)
IN_SPLITS = tuple(sum(IN_SIZES[:i + 1]) for i in range(len(IN_SIZES) - 1))

kernel_name = 'hymba_ssd_hgrn2_swa_peer_stream_step'


def rmsnorm(x, g):
    xf = x.astype(jnp.float32)
    y = xf * lax.rsqrt(jnp.mean(xf * xf, axis=-1, keepdims=True) + EPS)
    return (y * g.astype(jnp.float32)).astype(x.dtype)


def gated_group_rmsnorm(y, z, g):
    b, T, C = y.shape
    u = (y.astype(jnp.float32) * jax.nn.silu(z.astype(jnp.float32))).reshape(b, T, SSD_GROUPS, C // SSD_GROUPS)
    u = u * lax.rsqrt(jnp.mean(u * u, axis=-1, keepdims=True) + EPS)
    return (u.reshape(b, T, C) * g.astype(jnp.float32)).astype(y.dtype)


def alibi_slopes(n):
    return 2.0 ** (-8.0 * jnp.arange(1, n + 1, dtype=jnp.float32) / n)


def causal_conv(x, prev, w, bias):
    T = x.shape[1]
    xp = jnp.concatenate([prev.astype(x.dtype), x], axis=1)
    y = bias + sum(xp[:, j:j + T] * w[j] for j in range(CONV_W))
    return y, xp[:, xp.shape[1] - (CONV_W - 1):]


def masked_exp(cond, logits):
    return jnp.where(cond, jnp.exp(jnp.where(cond, logits, 0.0)), 0.0)


def ssd_scan(xs, dt, a, bm, cm, h0):
    b, T, H, P = xs.shape
    G, N = bm.shape[2], bm.shape[3]
    R = H // G
    L = math.gcd(T, CHUNK)
    nc = T // L
    x = xs.reshape(b, nc, L, G, R, P)
    dtc = dt.reshape(b, nc, L, G, R)
    bc = bm.reshape(b, nc, L, G, N)
    cc = cm.reshape(b, nc, L, G, N)
    acum = jnp.cumsum(dtc * a.reshape(G, R), axis=2)
    causal = jnp.tril(jnp.ones((L, L), bool))[:, :, None, None]
    seg = acum[:, :, :, None] - acum[:, :, None, :]
    decay = masked_exp(causal, seg)
    cb = jnp.einsum('bctgn,bcsgn->bctsg', cc, bc).astype(jnp.float32)
    w = cb[..., None] * decay * dtc[:, :, None]
    y_intra = jnp.einsum('bctsgr,bcsgrp->bctgrp', w, x)
    tail = jnp.exp(acum[:, :, -1:] - acum) * dtc
    chunk_state = jnp.einsum('bcsgr,bcsgn,bcsgrp->bcgrpn', tail, bc, x).astype(jnp.float32)
    chunk_decay = jnp.exp(acum[:, :, -1])

    def step(h, inp):
        cs, cd, cq, ea = inp
        y = jnp.einsum('btgn,bgrpn,btgr->btgrp', cq, h, ea)
        return h * cd[..., None, None] + cs, y

    sw = lambda t: jnp.moveaxis(t, 1, 0)
    h_last, y_inter = lax.scan(step, h0.reshape(b, G, R, P, N).astype(jnp.float32),
                               (sw(chunk_state), sw(chunk_decay), sw(cc), sw(jnp.exp(acum))))
    y = y_intra + jnp.moveaxis(y_inter, 0, 1)
    return y.reshape(b, T, H, P).astype(xs.dtype), h_last.reshape(b, H, P, N).astype(h0.dtype)


def hgrn2_scan(q, z, v, lb, h0):
    b, T, H, K = q.shape
    zf = z.astype(jnp.float32)
    f = lb + (1.0 - lb) * jax.nn.sigmoid(zf)
    logf = jnp.log(jnp.maximum(f, TINY))
    k = (1.0 - lb) * jax.nn.sigmoid(-zf)
    L = math.gcd(T, HG_BLOCK)
    n = T // L
    rs = lambda t: t.reshape(b, n, L, H, t.shape[-1])
    qb, kb, vb = rs(q.astype(jnp.float32)), rs(k), rs(v)
    gb = jnp.cumsum(rs(logf), axis=2)
    causal = jnp.tril(jnp.ones((L, L), bool))[:, :, None, None]
    dec = masked_exp(causal, gb[:, :, :, None] - gb[:, :, None, :])
    att = jnp.einsum('bnthk,bnshk,bntshk->bnhts', qb, kb, dec)
    o_intra = jnp.einsum('bnhts,bnshv->bnthv', att, vb)
    qg = qb * jnp.exp(gb)
    kdec = kb * jnp.exp(gb[:, :, -1:] - gb)
    chunk_state = jnp.einsum('bnshk,bnshv->bnhkv', kdec, vb).astype(jnp.float32)
    chunk_decay = jnp.exp(gb[:, :, -1])

    def step(h, inp):
        cs, cd, qc = inp
        o = jnp.einsum('bthk,bhkv->bthv', qc, h)
        return h * cd[..., None] + cs, o

    sw = lambda t: jnp.moveaxis(t, 1, 0)
    h_last, o_inter = lax.scan(step, h0.astype(jnp.float32), (sw(chunk_state), sw(chunk_decay), sw(qg)))
    o = o_intra + jnp.moveaxis(o_inter, 0, 1)
    return o.reshape(b, T, H, v.shape[-1]).astype(v.dtype), h_last.astype(h0.dtype)


def swa_attend(q, k, v, qpos, kpos, kvalid, sinks):
    b, nb, Tq = q.shape[:3]
    qg = q.reshape(b, nb, Tq, SWA_KV_HEADS, SWA_REP, SWA_HD)
    s = jnp.einsum('bnqgrd,bnkgd->bngrqk', qg, k).astype(jnp.float32) * SWA_HD ** -0.5
    dist = jnp.abs(qpos[:, :, None] - kpos[:, None, :]).astype(jnp.float32)
    slopes = alibi_slopes(SWA_HEADS).reshape(SWA_KV_HEADS, SWA_REP)
    s = s - slopes[None, None, :, :, None, None] * dist[None, :, None, None]
    qc = qpos // CHUNK
    kc = kpos // CHUNK
    mask = kvalid[:, None, :] & (kc[:, None, :] <= qc[:, :, None]) & (kc[:, None, :] >= qc[:, :, None] - WIN_CHUNKS)
    mask = mask[None, :, None, None]
    s = jnp.where(mask, s, NEG)
    sink = sinks.astype(jnp.float32).reshape(SWA_KV_HEADS, SWA_REP)[None, None, :, :, None, None]
    m = jnp.maximum(jnp.max(s, axis=-1, keepdims=True), sink)
    p = jnp.where(mask, jnp.exp(s - m), 0.0)
    p = p / (jnp.sum(p, axis=-1, keepdims=True) + jnp.exp(sink - m))
    o = jnp.einsum('bngrqk,bnkgd->bnqgrd', p.astype(v.dtype), v)
    return o.reshape(b, nb * Tq, SWA_HEADS * SWA_HD)


def swa_mixer(q, k, v, sinks, swa_cache):
    b, T = q.shape[:2]
    if swa_cache is None:
        nc = T // CHUNK
        pad = WIN_CHUNKS * CHUNK
        kp = jnp.pad(k, ((0, 0), (pad, 0), (0, 0), (0, 0)))
        vp = jnp.pad(v, ((0, 0), (pad, 0), (0, 0), (0, 0)))
        idx = jnp.arange(nc)[:, None] * CHUNK + jnp.arange(pad + CHUNK)[None, :]
        kpos = idx - pad
        qpos = jnp.arange(T).reshape(nc, CHUNK)
        o = swa_attend(q.reshape(b, nc, CHUNK, SWA_HEADS, SWA_HD), kp[:, idx], vp[:, idx],
                       qpos, kpos, kpos >= 0, sinks)
        nw = min(WINDOW, T)
        return o, k[:, T - nw:], v[:, T - nw:]
    ck, cv = swa_cache
    W = ck.shape[1]
    kk = jnp.concatenate([ck.astype(k.dtype), k], axis=1)
    vv = jnp.concatenate([cv.astype(v.dtype), v], axis=1)
    kpos = jnp.concatenate([PAST_LEN - W + jnp.arange(W), PAST_LEN + jnp.arange(T)])[None]
    qpos = (PAST_LEN + jnp.arange(T))[None]
    o = swa_attend(q[:, None], kk[:, None], vv[:, None], qpos, kpos, jnp.ones((1, W + T), bool), sinks)
    return o, kk[:, T:], vv[:, T:]


def parallel_mixer(xn, p, lb, conv_prev, ssm_prev, hgrn_prev, swa_cache):
    b, T, _ = xn.shape
    z, xbc, dt_raw, hq, hf, hi, hg, sq, sk, sv = jnp.split(xn @ p['w_in'], IN_SPLITS, axis=-1)
    xbc, conv_new = causal_conv(xbc, conv_prev, p['conv_w'], p['conv_b'])
    xbc = jax.nn.silu(xbc)
    xs, bm, cm = jnp.split(xbc, [SSD_INNER, SSD_INNER + SSD_GROUPS * SSD_STATE], axis=-1)
    dt = jax.nn.softplus((dt_raw + p['dt_bias']).astype(jnp.float32))
    a = -jnp.exp(p['a_log'].astype(jnp.float32))
    xs = xs.reshape(b, T, SSD_HEADS, SSD_HD)
    ys, ssm_new = ssd_scan(xs, dt, a, bm.reshape(b, T, SSD_GROUPS, SSD_STATE),
                           cm.reshape(b, T, SSD_GROUPS, SSD_STATE), ssm_prev)
    ys = ys + p['d_skip'][:, None].astype(ys.dtype) * xs
    ya = gated_group_rmsnorm(ys.reshape(b, T, SSD_INNER), z, p['ssd_norm'])
    heads = lambda t: t.reshape(b, T, HG_HEADS, HG_HD)
    qh = jax.nn.silu(heads(hq)) * HG_HD ** -0.5
    ob, hgrn_new = hgrn2_scan(qh, heads(hf), heads(hi), lb, hgrn_prev)
    yb = (rmsnorm(ob, p['hgrn_norm']) * jax.nn.silu(heads(hg))).reshape(b, T, HG_WIDTH)
    qs = rmsnorm(sq.reshape(b, T, SWA_HEADS, SWA_HD), p['swa_qnorm'])
    ksw = rmsnorm(sk.reshape(b, T, SWA_KV_HEADS, SWA_HD), p['swa_knorm'])
    vsw = sv.reshape(b, T, SWA_KV_HEADS, SWA_HD)
    yc, k_new, v_new = swa_mixer(qs, ksw, vsw, p['swa_sinks'], swa_cache)
    out = jnp.concatenate([ya, yb.astype(ya.dtype), yc.astype(ya.dtype)], axis=-1) @ p['w_out']
    return out, (ssm_new, conv_new, hgrn_new, k_new, v_new)


def memory_kv(mem, p):
    bm, M, _ = mem.shape
    mn = rmsnorm(mem, p['norm_memtok'])
    k = rmsnorm((mn @ p['w_mk']).reshape(bm, M, MEM_HEADS, MEM_HD), p['mem_knorm'])
    v = (mn @ p['w_mv']).reshape(bm, M, MEM_HEADS, MEM_HD)
    return k, v


def memory_attend(hn, mk, mv, p):
    b, T, _ = hn.shape
    q = rmsnorm((hn @ p['w_mq']).reshape(b, T, MEM_HEADS, MEM_HD), p['mem_qnorm'])
    s = jnp.einsum('bthd,bmhd->bhtm', q, mk.astype(q.dtype)).astype(jnp.float32) * MEM_HD ** -0.5
    pr = jax.nn.softmax(s, axis=-1).astype(hn.dtype)
    o = jnp.einsum('bhtm,bmhd->bthd', pr, mv.astype(hn.dtype)).reshape(b, T, MEM_WIDTH)
    return o @ p['w_mo']


def peer_ffn(xn, p):
    b, T, D = xn.shape
    n = b * T
    npad = (-n) % PEER_BLOCK
    xt = jnp.pad(xn.reshape(n, D), ((0, npad), (0, 0)))

    def block(xb):
        tb = xb.shape[0]
        q = (xb @ p['w_pq']).reshape(tb, PEER_HEADS, 2, PEER_QDIM // 2)
        s = jnp.einsum('thcd,hckd->thck', q, p['peer_keys']).astype(jnp.float32)
        sv, si = lax.top_k(s, PEER_TOPK)
        nc = PEER_TOPK * PEER_TOPK
        cand = (sv[:, :, 0, :, None] + sv[:, :, 1, None, :]).reshape(tb, PEER_HEADS, nc)
        cid = (si[:, :, 0, :, None] * N_KEYS + si[:, :, 1, None, :]).reshape(tb, PEER_HEADS, nc)
        fv, fi = lax.top_k(cand, PEER_TOPK)
        eid = jnp.take_along_axis(cid, fi, axis=-1).reshape(tb, PEER_HEADS * PEER_TOPK)
        g = jax.nn.softmax(fv, axis=-1).reshape(tb, PEER_HEADS * PEER_TOPK)
        act = jax.nn.gelu(jnp.einsum('td,ted->te', xb, p['peer_u'][eid]).astype(jnp.float32), approximate=False)
        return jnp.einsum('te,ted->td', (g * act).astype(xb.dtype), p['peer_v'][eid])

    y = lax.map(block, xt.reshape(-1, PEER_BLOCK, D))
    return y.reshape(-1, D)[:n].reshape(b, T, D)


def setup_inputs(seed: int = 0) -> dict:
    key = jax.random.key(seed)
    ks = iter(jax.random.split(key, 64))
    nrm = lambda shape, scale=1.0: jax.random.normal(next(ks), shape, jnp.float32) * scale
    gain = lambda shape: 1.0 + nrm(shape, 0.02)
    swa_rows = min(WINDOW, PAST_LEN)
    dt0 = jnp.exp(jax.random.uniform(next(ks), (DEPTH, SSD_HEADS), jnp.float32, math.log(1e-3), math.log(1e-1)))
    a0 = jax.random.uniform(next(ks), (DEPTH, SSD_HEADS), jnp.float32, 1.0, 16.0)
    return {
        'x_prompt': nrm((BATCH, SEQ, D_MODEL)),
        'x_sample': nrm((DEC_BATCH, DEC_SEQ, D_MODEL)),
        'mem_prompt': nrm((BATCH, MEM_TOKENS, D_MODEL)),
        'state_ssm': nrm((DEPTH, DEC_BATCH, SSD_HEADS, SSD_HD, SSD_STATE), 0.5),
        'state_ssd_conv': nrm((DEPTH, DEC_BATCH, CONV_W - 1, SSD_CONV_DIM)),
        'state_hgrn': nrm((DEPTH, DEC_BATCH, HG_HEADS, HG_HD, HG_HD), 0.5),
        'cache_swa_k': nrm((DEPTH, DEC_BATCH, swa_rows, SWA_KV_HEADS, SWA_HD)),
        'cache_swa_v': nrm((DEPTH, DEC_BATCH, swa_rows, SWA_KV_HEADS, SWA_HD)),
        'cache_mem_k': nrm((DEPTH, DEC_BATCH, MEM_TOKENS, MEM_HEADS, MEM_HD)),
        'cache_mem_v': nrm((DEPTH, DEC_BATCH, MEM_TOKENS, MEM_HEADS, MEM_HD)),
        'norm_mix': gain((DEPTH, D_MODEL)),
        'w_in': nrm((DEPTH, D_MODEL, IN_COLS), D_MODEL ** -0.5),
        'conv_w': nrm((DEPTH, CONV_W, SSD_CONV_DIM), CONV_W ** -0.5),
        'conv_b': nrm((DEPTH, SSD_CONV_DIM), 0.02),
        'dt_bias': dt0 + jnp.log(-jnp.expm1(-dt0)),
        'a_log': jnp.log(a0),
        'd_skip': 1.0 + nrm((DEPTH, SSD_HEADS), 0.1),
        'ssd_norm': gain((DEPTH, SSD_INNER)),
        'hgrn_lb': 1.0 + nrm((DEPTH, HG_WIDTH), 0.1),
        'hgrn_norm': gain((DEPTH, HG_HD)),
        'swa_qnorm': gain((DEPTH, SWA_HD)),
        'swa_knorm': gain((DEPTH, SWA_HD)),
        'swa_sinks': nrm((DEPTH, SWA_HEADS), 0.5),
        'w_out': nrm((DEPTH, MIX_WIDTH, D_MODEL), MIX_WIDTH ** -0.5),
        'norm_mem': gain((DEPTH, D_MODEL)),
        'norm_memtok': gain((DEPTH, D_MODEL)),
        'w_mq': nrm((DEPTH, D_MODEL, MEM_WIDTH), D_MODEL ** -0.5),
        'w_mk': nrm((DEPTH, D_MODEL, MEM_WIDTH), D_MODEL ** -0.5),
        'w_mv': nrm((DEPTH, D_MODEL, MEM_WIDTH), D_MODEL ** -0.5),
        'mem_qnorm': gain((DEPTH, MEM_HD)),
        'mem_knorm': gain((DEPTH, MEM_HD)),
        'w_mo': nrm((DEPTH, MEM_WIDTH, D_MODEL), MEM_WIDTH ** -0.5),
        'norm_ffn': gain((DEPTH, D_MODEL)),
        'w_pq': nrm((DEPTH, D_MODEL, PEER_HEADS * PEER_QDIM), D_MODEL ** -0.5),
        'peer_keys': nrm((DEPTH, PEER_HEADS, 2, N_KEYS, PEER_QDIM // 2), (PEER_QDIM // 2) ** -0.5),
        'peer_u': nrm((DEPTH, N_EXPERTS, D_MODEL), D_MODEL ** -0.5),
        'peer_v': nrm((DEPTH, N_EXPERTS, D_MODEL), PEER_HEADS ** -0.5),
    }


def reference(x_prompt, x_sample, mem_prompt, state_ssm, state_ssd_conv, state_hgrn, cache_swa_k, cache_swa_v,
              cache_mem_k, cache_mem_v, norm_mix, w_in, conv_w, conv_b, dt_bias, a_log, d_skip, ssd_norm,
              hgrn_lb, hgrn_norm, swa_qnorm, swa_knorm, swa_sinks, w_out, norm_mem, norm_memtok, w_mq, w_mk,
              w_mv, mem_qnorm, mem_knorm, w_mo, norm_ffn, w_pq, peer_keys, peer_u, peer_v):
    lbp = jax.nn.softmax(hgrn_lb.astype(jnp.float32), axis=0)
    lower = jnp.cumsum(lbp, axis=0) - lbp[0:1]
    bp = x_prompt.shape[0]
    dtp = x_prompt.dtype
    hp, hs = x_prompt, x_sample
    ssm_p, conv_p, hg_p, swk_p, swv_p, mk_p, mv_p = [], [], [], [], [], [], []
    ssm_s, conv_s, hg_s, swk_s, swv_s = [], [], [], [], []
    for l in range(DEPTH):
        p = {'w_in': w_in[l], 'conv_w': conv_w[l], 'conv_b': conv_b[l], 'dt_bias': dt_bias[l],
             'a_log': a_log[l], 'd_skip': d_skip[l], 'ssd_norm': ssd_norm[l], 'hgrn_norm': hgrn_norm[l],
             'swa_qnorm': swa_qnorm[l], 'swa_knorm': swa_knorm[l], 'swa_sinks': swa_sinks[l], 'w_out': w_out[l],
             'norm_memtok': norm_memtok[l], 'w_mq': w_mq[l], 'w_mk': w_mk[l], 'w_mv': w_mv[l],
             'mem_qnorm': mem_qnorm[l], 'mem_knorm': mem_knorm[l], 'w_mo': w_mo[l], 'w_pq': w_pq[l],
             'peer_keys': peer_keys[l], 'peer_u': peer_u[l], 'peer_v': peer_v[l]}
        lb = lower[l].reshape(HG_HEADS, HG_HD)
        mix, st = parallel_mixer(rmsnorm(hp, norm_mix[l]), p, lb,
                                 jnp.zeros((bp, CONV_W - 1, SSD_CONV_DIM), dtp),
                                 jnp.zeros((bp, SSD_HEADS, SSD_HD, SSD_STATE), dtp),
                                 jnp.zeros((bp, HG_HEADS, HG_HD, HG_HD), dtp), None)
        hp = hp + mix
        mk, mv = memory_kv(mem_prompt, p)
        hp = hp + memory_attend(rmsnorm(hp, norm_mem[l]), mk, mv, p)
        hp = hp + peer_ffn(rmsnorm(hp, norm_ffn[l]), p)
        ssm_p.append(st[0]); conv_p.append(st[1]); hg_p.append(st[2]); swk_p.append(st[3]); swv_p.append(st[4])
        mk_p.append(mk); mv_p.append(mv)
        mix, st = parallel_mixer(rmsnorm(hs, norm_mix[l]), p, lb, state_ssd_conv[l], state_ssm[l],
                                 state_hgrn[l], (cache_swa_k[l], cache_swa_v[l]))
        hs = hs + mix
        hs = hs + memory_attend(rmsnorm(hs, norm_mem[l]), cache_mem_k[l], cache_mem_v[l], p)
        hs = hs + peer_ffn(rmsnorm(hs, norm_ffn[l]), p)
        ssm_s.append(st[0]); conv_s.append(st[1]); hg_s.append(st[2]); swk_s.append(st[3]); swv_s.append(st[4])
    return (hp, hs,
            jnp.stack(ssm_p), jnp.stack(conv_p), jnp.stack(hg_p), jnp.stack(swk_p), jnp.stack(swv_p),
            jnp.stack(mk_p), jnp.stack(mv_p),
            jnp.stack(ssm_s), jnp.stack(conv_s), jnp.stack(hg_s), jnp.stack(swk_s), jnp.stack(swv_s))
```

```python
import functools
import math

import jax
import jax.numpy as jnp
from jax import lax
from jax.experimental import pallas as pl
from jax.experimental.pallas import tpu as pltpu

F32 = jnp.float32
BF16 = jnp.bfloat16
HIGHEST = lax.Precision.HIGHEST

D_MODEL = 4096
PAST_LEN = 2048
CHUNK = 64
SSD_INNER = 2048
SSD_HD = 64
SSD_HEADS = 32
SSD_GROUPS = 8
SSD_STATE = 128
SSD_GW = SSD_INNER // SSD_GROUPS
SSD_REP = SSD_HEADS // SSD_GROUPS
HG_WIDTH = 1024
HG_HD = 128
HG_HEADS = 8
HG_BLOCK = 16
SWA_HD = 64
SWA_HEADS = 16
SWA_KV_HEADS = 2
SWA_REP = SWA_HEADS // SWA_KV_HEADS
WINDOW = 128
MEM_TOKENS = 256
MEM_HEADS = 4
MEM_HD = 128
MEM_WIDTH = 512
N_KEYS = 128
N_EXPERTS = N_KEYS * N_KEYS
PEER_HEADS = 8
PEER_QDIM = 256
PEER_TOPK = 16
EPS = 1e-6
NEG = -1e30
TINY = 1e-30
LANE = 128

COL_Z, COL_XS, COL_BC = 0, 2048, 4096
COL_HQ, COL_HF, COL_HI, COL_HG = 6144, 7168, 8192, 9216
COL_SQ, COL_SK, COL_SV, COL_DT = 10240, 11264, 11392, 11520
PROJ_COLS = 11776

VMEM_LIMIT = 56 * 1024 * 1024

NT_DIMS = (((1,), (1,)), ((), ()))
TN_DIMS = (((0,), (0,)), ((), ()))


def _cparams(sem):
    return pltpu.CompilerParams(dimension_semantics=sem, vmem_limit_bytes=VMEM_LIMIT)


def _sigmoid(x):
    return 1.0 / (1.0 + jnp.exp(-x))


def _silu(x):
    return x * _sigmoid(x)


def _split_dot(x, ones_bf16):
    hi = x.astype(BF16)
    lo = (x - hi.astype(F32)).astype(BF16)
    return (jnp.dot(hi, ones_bf16, preferred_element_type=F32)
            + jnp.dot(lo, ones_bf16, preferred_element_type=F32))


def _mm_kernel(*refs, norm, head_norm, residual):
    it = iter(refs)
    x_ref = next(it)
    g_ref = next(it) if norm else None
    w_ref = next(it)
    hg_ref = next(it) if head_norm else None
    r_ref = next(it) if residual else None
    o_ref = next(it)
    xn_ref = next(it) if norm else None
    if norm:
        @pl.when(pl.program_id(1) == 0)
        def _():
            x = x_ref[...].astype(F32)
            ms = jnp.mean(x * x, axis=-1, keepdims=True)
            xn_ref[...] = (x * lax.rsqrt(ms + EPS) * g_ref[...]).astype(BF16)
        xb = xn_ref[...]
    else:
        xb = x_ref[...]
    acc = jnp.dot(xb, w_ref[...], preferred_element_type=F32)
    if head_norm:
        parts = []
        for c in range(acc.shape[1] // LANE):
            a = acc[:, c * LANE:(c + 1) * LANE]
            ms = jnp.mean(a * a, axis=-1, keepdims=True)
            parts.append(a * lax.rsqrt(ms + EPS))
        acc = jnp.concatenate(parts, axis=1) * hg_ref[...]
    if residual:
        acc = acc + r_ref[...]
    o_ref[...] = acc.astype(o_ref.dtype)


def _mm(x, w, *, gain=None, head_gain=None, res=None, out_dtype=F32, tm=512, tn=512, name="mm"):
    M, K = x.shape
    N = w.shape[1]
    tm, tn = min(tm, M), min(tn, N)
    assert M % tm == 0 and N % tn == 0
    norm, head_norm, residual = gain is not None, head_gain is not None, res is not None
    args, specs = [x], [pl.BlockSpec((tm, K), lambda i, j: (i, 0))]
    if norm:
        args.append(gain.reshape(1, K).astype(F32))
        specs.append(pl.BlockSpec((1, K), lambda i, j: (0, 0)))
    args.append(w)
    specs.append(pl.BlockSpec((K, tn), lambda i, j: (0, j)))
    if head_norm:
        args.append(jnp.tile(head_gain.astype(F32), N // head_gain.shape[0]).reshape(1, N))
        specs.append(pl.BlockSpec((1, tn), lambda i, j: (0, j)))
    if residual:
        args.append(res)
        specs.append(pl.BlockSpec((tm, tn), lambda i, j: (i, j)))
    return pl.pallas_call(
        functools.partial(_mm_kernel, norm=norm, head_norm=head_norm, residual=residual),
        grid=(M // tm, N // tn),
        in_specs=specs,
        out_specs=pl.BlockSpec((tm, tn), lambda i, j: (i, j)),
        out_shape=jax.ShapeDtypeStruct((M, N), out_dtype),
        scratch_shapes=[pltpu.VMEM((tm, K), BF16)] if norm else [],
        compiler_params=_cparams(("parallel", "arbitrary")),
        name=name,
    )(*args)


def _ssd_kernel(*refs, L, has_state):
    (z_ref, xs_ref, bc_ref, dt_ref, cwx_ref, cwb_ref, cbx_ref, cbb_ref, dtb_ref, alog_ref,
     dsk_ref, gn_ref) = refs[:12]
    rest = refs[12:]
    if has_state:
        conv0_ref, h0_ref = rest[:2]
        rest = rest[2:]
    y_ref, convo_ref, ho_ref, xpx_ref, xpb_ref, hT_ref = rest
    c = pl.program_id(1)
    nc = pl.num_programs(1)

    @pl.when(c == 0)
    def _init():
        if has_state:
            xpx_ref[0:8, :] = jnp.zeros((8, SSD_INNER), F32)
            xpb_ref[0:8, :] = jnp.zeros((8, SSD_INNER), F32)
            xpx_ref[5:8, :] = conv0_ref[0, :, 0:SSD_INNER]
            xpb_ref[5:8, :] = conv0_ref[0, :, SSD_INNER:2 * SSD_INNER]
            for g in range(SSD_GROUPS):
                hT_ref[g] = h0_ref[0, g * SSD_GW:(g + 1) * SSD_GW, :].T
        else:
            xpx_ref[0:8, :] = jnp.zeros((8, SSD_INNER), F32)
            xpb_ref[0:8, :] = jnp.zeros((8, SSD_INNER), F32)
            hT_ref[...] = jnp.zeros(hT_ref.shape, F32)

    xpx_ref[8:8 + L, :] = xs_ref[...]
    xpb_ref[8:8 + L, :] = bc_ref[...]

    def conv(xp_ref, w_ref, b_ref):
        acc = b_ref[...] + w_ref[3:4, :] * xp_ref[8:8 + L, :]
        for j in range(3):
            acc = acc + w_ref[j:j + 1, :] * xp_ref[5 + j:5 + j + L, :]
        return acc

    xc = _silu(conv(xpx_ref, cwx_ref, cbx_ref))
    bcc = _silu(conv(xpb_ref, cwb_ref, cbb_ref))
    tail_x = xpx_ref[5 + L:8 + L, :]
    tail_b = xpb_ref[5 + L:8 + L, :]
    xpx_ref[5:8, :] = tail_x
    xpb_ref[5:8, :] = tail_b
    convo_ref[0, :, 0:SSD_INNER] = tail_x
    convo_ref[0, :, SSD_INNER:2 * SSD_INNER] = tail_b

    dtr = dt_ref[...] + dtb_ref[...]
    dt = jnp.maximum(dtr, 0.0) + jnp.log(1.0 + jnp.exp(-jnp.abs(dtr)))
    a = -jnp.exp(alog_ref[...])
    dta = dt * a
    ri = lax.broadcasted_iota(jnp.int32, (L, L), 0)
    ci = lax.broadcasted_iota(jnp.int32, (L, L), 1)
    causal = ri >= ci
    acum = jnp.dot(causal.astype(F32), dta, precision=HIGHEST, preferred_element_type=F32)
    eh = lax.broadcasted_iota(jnp.int32, (LANE, SSD_INNER), 0)
    ec = lax.broadcasted_iota(jnp.int32, (LANE, SSD_INNER), 1)
    expand = (ec // SSD_HD == eh).astype(F32)
    acx = jnp.dot(acum, expand, precision=HIGHEST, preferred_element_type=F32)
    dtx = jnp.dot(dt, expand, precision=HIGHEST, preferred_element_type=F32)
    eax = jnp.exp(acx)
    lastx = acx[L - 1:L, :]
    tailw = jnp.exp(lastx - acx) * dtx
    cdx = jnp.exp(lastx)
    acT = acum.T
    dtT = dt.T
    lane_head = lax.broadcasted_iota(jnp.int32, (L, SSD_GW), 1) // SSD_HD
    z = z_ref[...]

    for g in range(SSD_GROUPS):
        gs = slice(g * SSD_GW, (g + 1) * SSD_GW)
        Bg = bcc[:, g * SSD_STATE:(g + 1) * SSD_STATE].astype(BF16)
        Cg = bcc[:, SSD_GROUPS * SSD_STATE + g * SSD_STATE:
                 SSD_GROUPS * SSD_STATE + (g + 1) * SSD_STATE].astype(BF16)
        cb = lax.dot_general(Cg, Bg, NT_DIMS, preferred_element_type=F32)
        ws = []
        for r in range(SSD_REP):
            hd = g * SSD_REP + r
            seg = acum[:, hd:hd + 1] - acT[hd:hd + 1, :]
            dec = jnp.where(causal, jnp.exp(jnp.where(causal, seg, 0.0)), 0.0)
            ws.append(cb * dec * dtT[hd:hd + 1, :])
        wst = jnp.concatenate(ws, axis=0).astype(BF16)
        xg = xc[:, gs]
        full = jnp.dot(wst, xg.astype(BF16), preferred_element_type=F32)
        y_intra = jnp.zeros((L, SSD_GW), F32)
        for r in range(SSD_REP):
            y_intra = y_intra + jnp.where(lane_head == r, full[r * L:(r + 1) * L, :], 0.0)
        hTg = hT_ref[g]
        y_inter = jnp.dot(Cg, hTg.astype(BF16), preferred_element_type=F32) * eax[:, gs]
        xt = (xg * tailw[:, gs]).astype(BF16)
        hT_ref[g] = hTg * cdx[:, gs] + lax.dot_general(Bg, xt, TN_DIMS, preferred_element_type=F32)
        yg = y_intra + y_inter + dsk_ref[:, gs] * xg
        u = yg * _silu(z[:, gs])
        ms = jnp.mean(u * u, axis=-1, keepdims=True)
        y_ref[:, gs] = (u * lax.rsqrt(ms + EPS) * gn_ref[:, gs]).astype(y_ref.dtype)

    @pl.when(c == nc - 1)
    def _fin():
        for g in range(SSD_GROUPS):
            ho_ref[0, g * SSD_GW:(g + 1) * SSD_GW, :] = hT_ref[g].T


def _ssd(proj, p, *, row0, B, T, conv0=None, h0=None):
    L = math.gcd(T, CHUNK)
    nc = T // L
    rb0 = row0 // L
    has_state = conv0 is not None

    def rows(col):
        return lambda b, c: (rb0 + b * nc + c, col)

    const = lambda b, c: (0, 0)
    args = [proj, proj, proj, proj, p['cwx'], p['cwb'], p['cbx'], p['cbb'], p['dtb'], p['alog'],
            p['dsk'], p['ssd_norm']]
    specs = [pl.BlockSpec((L, SSD_INNER), rows(COL_Z // SSD_INNER)),
             pl.BlockSpec((L, SSD_INNER), rows(COL_XS // SSD_INNER)),
             pl.BlockSpec((L, SSD_INNER), rows(COL_BC // SSD_INNER)),
             pl.BlockSpec((L, LANE), rows(COL_DT // LANE)),
             pl.BlockSpec((4, SSD_INNER), const), pl.BlockSpec((4, SSD_INNER), const),
             pl.BlockSpec((1, SSD_INNER), const), pl.BlockSpec((1, SSD_INNER), const),
             pl.BlockSpec((1, LANE), const), pl.BlockSpec((1, LANE), const),
             pl.BlockSpec((1, SSD_INNER), const), pl.BlockSpec((1, SSD_INNER), const)]
    if has_state:
        args += [conv0, h0.reshape(B, SSD_INNER, SSD_STATE)]
        specs += [pl.BlockSpec((1, 3, 2 * SSD_INNER), lambda b, c: (b, 0, 0)),
                  pl.BlockSpec((1, SSD_INNER, SSD_STATE), lambda b, c: (b, 0, 0))]
    y, convo, ho = pl.pallas_call(
        functools.partial(_ssd_kernel, L=L, has_state=has_state),
        grid=(B, nc),
        in_specs=specs,
        out_specs=[pl.BlockSpec((L, SSD_INNER), lambda b, c: (b * nc + c, 0)),
                   pl.BlockSpec((1, 3, 2 * SSD_INNER), lambda b, c: (b, 0, 0)),
                   pl.BlockSpec((1, SSD_INNER, SSD_STATE), lambda b, c: (b, 0, 0))],
        out_shape=[jax.ShapeDtypeStruct((B * T, SSD_INNER), BF16),
                   jax.ShapeDtypeStruct((B, 3, 2 * SSD_INNER), F32),
                   jax.ShapeDtypeStruct((B, SSD_INNER, SSD_STATE), F32)],
        scratch_shapes=[pltpu.VMEM((8 + L, SSD_INNER), F32), pltpu.VMEM((8 + L, SSD_INNER), F32),
                        pltpu.VMEM((SSD_GROUPS, SSD_STATE, SSD_GW), F32)],
        compiler_params=_cparams(("parallel", "arbitrary")),
        name="ssd",
    )(*args)
    return y, convo, ho.reshape(B, SSD_HEADS, SSD_HD, SSD_STATE)


def _hgrn_kernel(*refs, Lc, has_state):
    q_ref, f_ref, i_ref, g_ref, lb_ref, gn_ref = refs[:6]
    rest = refs[6:]
    if has_state:
        s0_ref = rest[0]
        rest = rest[1:]
    y_ref, so_ref, kp_ref, gp_ref, vp_ref, st_ref = rest
    c = pl.program_id(1)
    nc = pl.num_programs(1)
    nb = Lc // HG_BLOCK

    @pl.when(c == 0)
    def _init():
        for h in range(HG_HEADS):
            st_ref[h] = s0_ref[0, h].T if has_state else jnp.zeros((HG_HD, HG_HD), F32)
        kp_ref[0:HG_BLOCK, :] = jnp.zeros((HG_BLOCK, HG_WIDTH), F32)
        gp_ref[0:HG_BLOCK, :] = jnp.zeros((HG_BLOCK, HG_WIDTH), F32)
        vp_ref[0:HG_BLOCK, :] = jnp.zeros((HG_BLOCK, HG_WIDTH), F32)

    zf = f_ref[...]
    lb = lb_ref[...]
    f = lb + (1.0 - lb) * _sigmoid(zf)
    logf = jnp.log(jnp.maximum(f, TINY))
    k = (1.0 - lb) * _sigmoid(-zf)
    q = _silu(q_ref[...]) * (HG_HD ** -0.5)
    v = i_ref[...]
    ri = lax.broadcasted_iota(jnp.int32, (Lc, Lc), 0)
    ci = lax.broadcasted_iota(jnp.int32, (Lc, Lc), 1)
    same = (ri // HG_BLOCK) == (ci // HG_BLOCK)
    ltri = jnp.where(same, jnp.where(ri >= ci, 1.0, 0.0), 0.0)
    lall = jnp.where(same, 1.0, 0.0)
    gb = jnp.dot(ltri, logf, precision=HIGHEST, preferred_element_type=F32)
    gl = jnp.dot(lall, logf, precision=HIGHEST, preferred_element_type=F32)
    kp_ref[HG_BLOCK:HG_BLOCK + Lc, :] = k
    gp_ref[HG_BLOCK:HG_BLOCK + Lc, :] = gb
    vp_ref[HG_BLOCK:HG_BLOCK + Lc, :] = v
    tpos = lax.broadcasted_iota(jnp.int32, (Lc, HG_WIDTH), 0) % HG_BLOCK
    ones = jnp.ones((HG_HD, HG_HD), BF16)

    o_intra = [jnp.zeros((Lc, HG_HD), F32) for _ in range(HG_HEADS)]
    for d in range(HG_BLOCK):
        lo = HG_BLOCK - d
        kd = kp_ref[lo:lo + Lc, :]
        gd = gp_ref[lo:lo + Lc, :]
        vd = vp_ref[lo:lo + Lc, :]
        m = tpos >= d
        dec = jnp.exp(jnp.where(m, gb - gd, 0.0))
        pr = jnp.where(m, q * kd * dec, 0.0)
        for h in range(HG_HEADS):
            hs = slice(h * HG_HD, (h + 1) * HG_HD)
            o_intra[h] = o_intra[h] + _split_dot(pr[:, hs], ones) * vd[:, hs]

    qg = (q * jnp.exp(gb)).astype(BF16)
    kdec = (k * jnp.exp(gl - gb)).astype(BF16)
    cd = jnp.exp(gl)
    vb = v.astype(BF16)
    gate = _silu(g_ref[...])
    for h in range(HG_HEADS):
        hs = slice(h * HG_HD, (h + 1) * HG_HD)
        st = st_ref[h]
        outs = []
        for b in range(nb):
            rs = slice(b * HG_BLOCK, (b + 1) * HG_BLOCK)
            outs.append(lax.dot_general(qg[rs, hs], st.astype(BF16), NT_DIMS, preferred_element_type=F32))
            st = st * cd[b * HG_BLOCK:b * HG_BLOCK + 1, hs] + lax.dot_general(
                vb[rs, hs], kdec[rs, hs], TN_DIMS, preferred_element_type=F32)
        st_ref[h] = st
        o = o_intra[h] + jnp.concatenate(outs, axis=0)
        ms = jnp.mean(o * o, axis=-1, keepdims=True)
        y_ref[:, hs] = (o * lax.rsqrt(ms + EPS) * gn_ref[:, hs] * gate[:, hs]).astype(y_ref.dtype)

    @pl.when(c == nc - 1)
    def _fin():
        for h in range(HG_HEADS):
            so_ref[0, h] = st_ref[h].T


def _hgrn(proj, p, lb, *, row0, B, T, Lc, s0=None):
    nc = T // Lc
    rb0 = row0 // Lc
    has_state = s0 is not None

    def rows(col):
        return lambda b, c: (rb0 + b * nc + c, col // HG_WIDTH)

    const = lambda b, c: (0, 0)
    args = [proj, proj, proj, proj, lb, p['hgrn_norm']]
    specs = [pl.BlockSpec((Lc, HG_WIDTH), rows(COL_HQ)), pl.BlockSpec((Lc, HG_WIDTH), rows(COL_HF)),
             pl.BlockSpec((Lc, HG_WIDTH), rows(COL_HI)), pl.BlockSpec((Lc, HG_WIDTH), rows(COL_HG)),
             pl.BlockSpec((1, HG_WIDTH), const), pl.BlockSpec((1, HG_WIDTH), const)]
    if has_state:
        args.append(s0)
        specs.append(pl.BlockSpec((1, HG_HEADS, HG_HD, HG_HD), lambda b, c: (b, 0, 0, 0)))
    y, so = pl.pallas_call(
        functools.partial(_hgrn_kernel, Lc=Lc, has_state=has_state),
        grid=(B, nc),
        in_specs=specs,
        out_specs=[pl.BlockSpec((Lc, HG_WIDTH), lambda b, c: (b * nc + c, 0)),
                   pl.BlockSpec((1, HG_HEADS, HG_HD, HG_HD), lambda b, c: (b, 0, 0, 0))],
        out_shape=[jax.ShapeDtypeStruct((B * T, HG_WIDTH), BF16),
                   jax.ShapeDtypeStruct((B, HG_HEADS, HG_HD, HG_HD), F32)],
        scratch_shapes=[pltpu.VMEM((HG_BLOCK + Lc, HG_WIDTH), F32)] * 3
                       + [pltpu.VMEM((HG_HEADS, HG_HD, HG_HD), F32)],
        compiler_params=_cparams(("parallel", "arbitrary")),
        name="hgrn",
    )(*args)
    return y, so


def _swa_kernel(*refs, Tq, prompt):
    sink_ref, q_ref, k_ref, v_ref, qg_ref, kg_ref = refs[:6]
    rest = refs[6:]
    if not prompt:
        ck_ref, cv_ref = rest[:2]
        rest = rest[2:]
    y_ref, ko_ref, vo_ref = rest[:3]
    if prompt:
        kp_ref, vp_ref = rest[3:]
    c = pl.program_id(1)
    Tk = WINDOW + Tq
    bi = lax.broadcasted_iota(jnp.int32, (LANE, LANE), 0) // SWA_HD
    bj = lax.broadcasted_iota(jnp.int32, (LANE, LANE), 1) // SWA_HD
    bd = jnp.where(bi == bj, 1.0, 0.0).astype(BF16)

    def hnorm(x, gain):
        outs = []
        for j in range(x.shape[1] // LANE):
            xs = x[:, j * LANE:(j + 1) * LANE]
            ms = _split_dot(xs * xs, bd) * (1.0 / SWA_HD)
            outs.append(xs * lax.rsqrt(ms + EPS))
        return (jnp.concatenate(outs, axis=1) if len(outs) > 1 else outs[0]) * gain

    qn = hnorm(q_ref[...], qg_ref[...])
    kn = hnorm(k_ref[...], kg_ref[...])
    vn = v_ref[...]
    if prompt:
        @pl.when(c == 0)
        def _init():
            kp_ref[...] = jnp.zeros((WINDOW, LANE), F32)
            vp_ref[...] = jnp.zeros((WINDOW, LANE), F32)
        kprev = kp_ref[...]
        vprev = vp_ref[...]
    else:
        kprev = ck_ref[0]
        vprev = cv_ref[0]
    kall = jnp.concatenate([kprev, kn], axis=0)
    vall = jnp.concatenate([vprev, vn], axis=0)
    knew = kall[Tq:, :]
    vnew = vall[Tq:, :]
    ko_ref[0] = knew
    vo_ref[0] = vnew
    if prompt:
        kp_ref[...] = knew
        vp_ref[...] = vnew
    ksw = pltpu.roll(kall, SWA_HD, 1)
    vsw = pltpu.roll(vall, SWA_HD, 1)
    kall_b, ksw_b, vall_b, vsw_b = (t.astype(BF16) for t in (kall, ksw, vall, vsw))
    qi = lax.broadcasted_iota(jnp.int32, (Tq, Tk), 0)
    kj = lax.broadcasted_iota(jnp.int32, (Tq, Tk), 1)
    dist = jnp.abs(qi + WINDOW - kj).astype(F32)
    valid = (kj + c * Tq) >= WINDOW
    lane = lax.broadcasted_iota(jnp.int32, (Tq, LANE), 1)
    low = lane < SWA_HD
    for j in range(SWA_HEADS // 2):
        qp = qn[:, j * LANE:(j + 1) * LANE]
        grp = (2 * j) // SWA_REP
        halves = []
        for half in (0, 1):
            hd = 2 * j + half
            qm = jnp.where(low if half == 0 else jnp.logical_not(low), qp, 0.0).astype(BF16)
            kuse, vuse = (kall_b, vall_b) if grp == half else (ksw_b, vsw_b)
            slope = 2.0 ** (-8.0 * (hd + 1) / SWA_HEADS)
            s = lax.dot_general(qm, kuse, NT_DIMS, preferred_element_type=F32) * (SWA_HD ** -0.5)
            s = s - slope * dist
            if prompt:
                s = jnp.where(valid, s, NEG)
            sink = sink_ref[hd]
            mx = jnp.maximum(jnp.max(s, axis=-1, keepdims=True), sink)
            pe = jnp.exp(s - mx)
            if prompt:
                pe = jnp.where(valid, pe, 0.0)
            pe = pe / (jnp.sum(pe, axis=-1, keepdims=True) + jnp.exp(sink - mx))
            halves.append(jnp.dot(pe.astype(BF16), vuse, preferred_element_type=F32))
        y_ref[:, j * LANE:(j + 1) * LANE] = jnp.where(low, halves[0], halves[1]).astype(y_ref.dtype)


def _swa(proj, p, *, row0, B, T, ck=None, cv=None):
    prompt = ck is None
    Tq = CHUNK if prompt else T
    nc = T // Tq
    rb0 = row0 // Tq

    def rows(col, width):
        return lambda b, c: (rb0 + b * nc + c, col // width)

    const = lambda b, c: (0, 0)
    args = [p['swa_sinks'], proj, proj, proj, p['swa_qnorm'], p['swa_knorm']]
    specs = [pl.BlockSpec(memory_space=pltpu.SMEM),
             pl.BlockSpec((Tq, 1024), rows(COL_SQ, 1024)),
             pl.BlockSpec((Tq, LANE), rows(COL_SK, LANE)),
             pl.BlockSpec((Tq, LANE), rows(COL_SV, LANE)),
             pl.BlockSpec((1, 1024), const), pl.BlockSpec((1, LANE), const)]
    if not prompt:
        args += [ck.reshape(B, WINDOW, LANE), cv.reshape(B, WINDOW, LANE)]
        specs += [pl.BlockSpec((1, WINDOW, LANE), lambda b, c: (b, 0, 0))] * 2
    y, ko, vo = pl.pallas_call(
        functools.partial(_swa_kernel, Tq=Tq, prompt=prompt),
        grid=(B, nc),
        in_specs=specs,
        out_specs=[pl.BlockSpec((Tq, 1024), lambda b, c: (b * nc + c, 0)),
                   pl.BlockSpec((1, WINDOW, LANE), lambda b, c: (b, 0, 0)),
                   pl.BlockSpec((1, WINDOW, LANE), lambda b, c: (b, 0, 0))],
        out_shape=[jax.ShapeDtypeStruct((B * T, 1024), BF16),
                   jax.ShapeDtypeStruct((B, WINDOW, LANE), F32),
                   jax.ShapeDtypeStruct((B, WINDOW, LANE), F32)],
        scratch_shapes=[pltpu.VMEM((WINDOW, LANE), F32)] * 2 if prompt else [],
        compiler_params=_cparams(("parallel", "arbitrary")),
        name="swa",
    )(*args)
    shp = (B, WINDOW, SWA_KV_HEADS, SWA_HD)
    return y, ko.reshape(shp), vo.reshape(shp)


def _mem_kernel(q_ref, k_ref, v_ref, o_ref):
    for h in range(MEM_HEADS):
        hs = slice(h * MEM_HD, (h + 1) * MEM_HD)
        s = lax.dot_general(q_ref[:, hs].astype(BF16), k_ref[:, hs].astype(BF16), NT_DIMS,
                            preferred_element_type=F32) * (MEM_HD ** -0.5)
        mx = jnp.max(s, axis=-1, keepdims=True)
        pe = jnp.exp(s - mx)
        pe = pe / jnp.sum(pe, axis=-1, keepdims=True)
        o_ref[:, hs] = jnp.dot(pe.astype(BF16), v_ref[:, hs].astype(BF16),
                               preferred_element_type=F32).astype(o_ref.dtype)


def _mem_attend(qn, mk, mv, *, row0, B, T, tq):
    nt = T // tq
    rb0 = row0 // tq
    return pl.pallas_call(
        _mem_kernel,
        grid=(B, nt),
        in_specs=[pl.BlockSpec((tq, MEM_WIDTH), lambda b, t: (rb0 + b * nt + t, 0)),
                  pl.BlockSpec((MEM_TOKENS, MEM_WIDTH), lambda b, t: (b, 0)),
                  pl.BlockSpec((MEM_TOKENS, MEM_WIDTH), lambda b, t: (b, 0))],
        out_specs=pl.BlockSpec((tq, MEM_WIDTH), lambda b, t: (b * nt + t, 0)),
        out_shape=jax.ShapeDtypeStruct((B * T, MEM_WIDTH), BF16),
        compiler_params=_cparams(("parallel", "arbitrary")),
        name="mem_attend",
    )(qn, mk, mv)


def _route_kernel(pq_ref, keys_ref, s1_ref, e1_ref, s2_ref, e2_ref, tau_ref):
    half = PEER_QDIM // 2

    def top_values(s):
        cur, vals = s, []
        for _ in range(PEER_TOPK):
            mx = jnp.max(cur, axis=0, keepdims=True)
            vals.append(mx)
            cur = jnp.where(cur == mx, -jnp.inf, cur)
        return vals

    for h in range(PEER_HEADS):
        q1 = pq_ref[:, h * PEER_QDIM:h * PEER_QDIM + half]
        q2 = pq_ref[:, h * PEER_QDIM + half:(h + 1) * PEER_QDIM]
        s1 = lax.dot_general(keys_ref[h, 0], q1, NT_DIMS, precision=HIGHEST, preferred_element_type=F32)
        s2 = lax.dot_general(keys_ref[h, 1], q2, NT_DIMS, precision=HIGHEST, preferred_element_type=F32)
        v1 = top_values(s1)
        v2 = jnp.concatenate(top_values(s2), axis=0)
        cand = jnp.concatenate([v1[r] + v2 for r in range(PEER_TOPK)], axis=0)
        cur = cand
        cnt = jnp.zeros_like(v1[0])
        tau = jnp.full_like(v1[0], -jnp.inf)
        for _ in range(PEER_TOPK):
            mx = jnp.max(cur, axis=0, keepdims=True)
            eq = cur == mx
            tau = jnp.where(cnt < PEER_TOPK, mx, tau)
            cnt = cnt + jnp.sum(jnp.where(eq, 1.0, 0.0), axis=0, keepdims=True)
            cur = jnp.where(eq, -jnp.inf, cur)
        m0 = v1[0] + v2[0:1, :]
        zsum = jnp.sum(jnp.where(cand >= tau, jnp.exp(cand - m0), 0.0), axis=0, keepdims=True)
        s1_ref[h] = s1
        s2_ref[h] = s2
        e1_ref[h] = jnp.exp(s1 - v1[0])
        e2_ref[h] = jnp.exp(s2 - v2[0:1, :]) / zsum
        tau_ref[h:h + 1, :] = tau


def _peer_route(pq, keys, *, tm=256):
    N = pq.shape[0]
    big = jax.ShapeDtypeStruct((PEER_HEADS, N_KEYS, N), F32)
    bspec = pl.BlockSpec((PEER_HEADS, N_KEYS, tm), lambda i: (0, 0, i))
    return pl.pallas_call(
        _route_kernel,
        grid=(N // tm,),
        in_specs=[pl.BlockSpec((tm, PEER_HEADS * PEER_QDIM), lambda i: (i, 0)),
                  pl.BlockSpec((PEER_HEADS, 2, N_KEYS, PEER_QDIM // 2), lambda i: (0, 0, 0, 0))],
        out_specs=[bspec, bspec, bspec, bspec, pl.BlockSpec((PEER_HEADS, tm), lambda i: (0, i))],
        out_shape=[big, big, big, big, jax.ShapeDtypeStruct((PEER_HEADS, N), F32)],
        compiler_params=_cparams(("parallel",)),
        name="peer_route",
    )(pq, keys)


def _dense_kernel(h_ref, g_ref, u_ref, v_ref, s1_ref, e1_ref, s2_ref, e2_ref, tau_ref, o_ref, xn_ref, c_ref,
                  *, te):
    j = pl.program_id(1)
    na = te // N_KEYS

    @pl.when(j == 0)
    def _init():
        x = h_ref[...]
        ms = jnp.mean(x * x, axis=-1, keepdims=True)
        xn_ref[...] = (x * lax.rsqrt(ms + EPS) * g_ref[...]).astype(BF16)
        o_ref[...] = x

    pre = lax.dot_general(xn_ref[...], u_ref[...], NT_DIMS, preferred_element_type=F32)
    act = 0.5 * pre * (1.0 + lax.erf(pre * (2.0 ** -0.5)))
    for al in range(na):
        a = j * na + al
        gt = jnp.zeros((N_KEYS, h_ref.shape[0]), F32)
        for h in range(PEER_HEADS):
            s1r = s1_ref[h, pl.ds(a, 1), :]
            e1r = e1_ref[h, pl.ds(a, 1), :]
            hit = (s1r + s2_ref[h]) >= tau_ref[h:h + 1, :]
            gt = gt + jnp.where(hit, e1r * e2_ref[h], 0.0)
        c_ref[:, al * N_KEYS:(al + 1) * N_KEYS] = (gt.T * act[:, al * N_KEYS:(al + 1) * N_KEYS]).astype(BF16)
    o_ref[...] += jnp.dot(c_ref[...], v_ref[...], preferred_element_type=F32)


def _peer_dense(h, gain, u, v, s1, e1, s2, e2, tau, *, tm=512, te=512):
    N, D = h.shape
    one = pl.Buffered(1)
    rt = pl.BlockSpec((PEER_HEADS, N_KEYS, tm), lambda i, j: (0, 0, i), pipeline_mode=one)
    return pl.pallas_call(
        functools.partial(_dense_kernel, te=te),
        grid=(N // tm, N_EXPERTS // te),
        in_specs=[pl.BlockSpec((tm, D), lambda i, j: (i, 0), pipeline_mode=one),
                  pl.BlockSpec((1, D), lambda i, j: (0, 0)),
                  pl.BlockSpec((te, D), lambda i, j: (j, 0)),
                  pl.BlockSpec((te, D), lambda i, j: (j, 0)),
                  rt, rt, rt, rt,
                  pl.BlockSpec((PEER_HEADS, tm), lambda i, j: (0, i), pipeline_mode=one)],
        out_specs=pl.BlockSpec((tm, D), lambda i, j: (i, 0), pipeline_mode=one),
        out_shape=jax.ShapeDtypeStruct((N, D), F32),
        scratch_shapes=[pltpu.VMEM((tm, D), BF16), pltpu.VMEM((tm, te), BF16)],
        compiler_params=_cparams(("parallel", "arbitrary")),
        name="peer_dense",
    )(h, gain.reshape(1, D).astype(F32), u, v, s1, e1, s2, e2, tau)


def _row(x, width=None):
    x = x.astype(F32).reshape(1, -1)
    if width is not None and x.shape[1] < width:
        x = jnp.pad(x, ((0, 0), (0, width - x.shape[1])))
    return x


def _layer_params(l, w_in, conv_w, conv_b, dt_bias, a_log, d_skip, ssd_norm, hgrn_norm, swa_qnorm,
                  swa_knorm, swa_sinks):
    w = w_in[l]
    z_end, xbc_end, dt_end = SSD_INNER, SSD_INNER + 2 * SSD_INNER, SSD_INNER + 2 * SSD_INNER + SSD_HEADS
    w_packed = jnp.concatenate(
        [w[:, :xbc_end], w[:, dt_end:], w[:, xbc_end:dt_end],
         jnp.zeros((D_MODEL, PROJ_COLS - w.shape[1]), w.dtype)], axis=1).astype(BF16)
    del z_end
    return {
        'w_in': w_packed,
        'cwx': conv_w[l][:, :SSD_INNER].astype(F32), 'cwb': conv_w[l][:, SSD_INNER:].astype(F32),
        'cbx': _row(conv_b[l][:SSD_INNER]), 'cbb': _row(conv_b[l][SSD_INNER:]),
        'dtb': _row(dt_bias[l], LANE), 'alog': _row(a_log[l], LANE),
        'dsk': _row(jnp.repeat(d_skip[l], SSD_HD)), 'ssd_norm': _row(ssd_norm[l]),
        'hgrn_norm': _row(jnp.tile(hgrn_norm[l], HG_HEADS)),
        'swa_qnorm': _row(jnp.tile(swa_qnorm[l], SWA_HEADS)),
        'swa_knorm': _row(jnp.tile(swa_knorm[l], SWA_KV_HEADS)),
        'swa_sinks': swa_sinks[l].astype(F32),
    }


def kernel(x_prompt, x_sample, mem_prompt, state_ssm, state_ssd_conv, state_hgrn, cache_swa_k, cache_swa_v,
           cache_mem_k, cache_mem_v, norm_mix, w_in, conv_w, conv_b, dt_bias, a_log, d_skip, ssd_norm,
           hgrn_lb, hgrn_norm, swa_qnorm, swa_knorm, swa_sinks, w_out, norm_mem, norm_memtok, w_mq, w_mk,
           w_mv, mem_qnorm, mem_knorm, w_mo, norm_ffn, w_pq, peer_keys, peer_u, peer_v):
    depth = w_in.shape[0]
    bp, tp, _ = x_prompt.shape
    bs, ts, _ = x_sample.shape
    n_p, n_s = bp * tp, bs * ts
    lbp = jax.nn.softmax(hgrn_lb.astype(F32), axis=0)
    lower = jnp.cumsum(lbp, axis=0) - lbp[0:1]
    h = jnp.concatenate([x_prompt.reshape(n_p, D_MODEL), x_sample.reshape(n_s, D_MODEL)], axis=0)
    mem2d = mem_prompt.reshape(bp * MEM_TOKENS, D_MODEL)
    outs = {k: [] for k in ('ssm_p', 'conv_p', 'hg_p', 'swk_p', 'swv_p', 'mk_p', 'mv_p',
                            'ssm_s', 'conv_s', 'hg_s', 'swk_s', 'swv_s')}
    for l in range(depth):
        p = _layer_params(l, w_in, conv_w, conv_b, dt_bias, a_log, d_skip, ssd_norm, hgrn_norm,
                          swa_qnorm, swa_knorm, swa_sinks)
        lb = _row(lower[l])
        proj = _mm(h, p['w_in'], gain=norm_mix[l], name="proj_in")
        ya_p, conv_p, ssm_p = _ssd(proj, p, row0=0, B=bp, T=tp)
        ya_s, conv_s, ssm_s = _ssd(proj, p, row0=n_p, B=bs, T=ts, conv0=state_ssd_conv[l], h0=state_ssm[l])
        yb_p, hg_p = _hgrn(proj, p, lb, row0=0, B=bp, T=tp, Lc=128)
        yb_s, hg_s = _hgrn(proj, p, lb, row0=n_p, B=bs, T=ts, Lc=ts, s0=state_hgrn[l])
        yc_p, swk_p, swv_p = _swa(proj, p, row0=0, B=bp, T=tp)
        yc_s, swk_s, swv_s = _swa(proj, p, row0=n_p, B=bs, T=ts, ck=cache_swa_k[l], cv=cache_swa_v[l])
        ycat = jnp.concatenate([jnp.concatenate([ya_p, yb_p, yc_p], axis=1),
                                jnp.concatenate([ya_s, yb_s, yc_s], axis=1)], axis=0)
        h = _mm(ycat, w_out[l].astype(BF16), res=h, name="proj_out")
        mk = _mm(mem2d, w_mk[l].astype(BF16), gain=norm_memtok[l], head_gain=mem_knorm[l], name="mem_k")
        mv = _mm(mem2d, w_mv[l].astype(BF16), gain=norm_memtok[l], name="mem_v")
        qn = _mm(h, w_mq[l].astype(BF16), gain=norm_mem[l], head_gain=mem_qnorm[l], name="mem_q")
        o_p = _mem_attend(qn, mk, mv, row0=0, B=bp, T=tp, tq=512)
        o_s = _mem_attend(qn, cache_mem_k[l].reshape(bs * MEM_TOKENS, MEM_WIDTH),
                          cache_mem_v[l].reshape(bs * MEM_TOKENS, MEM_WIDTH), row0=n_p, B=bs, T=ts, tq=ts)
        h = _mm(jnp.concatenate([o_p, o_s], axis=0), w_mo[l].astype(BF16), res=h, name="mem_o")
        pq = _mm(h, w_pq[l].astype(BF16), gain=norm_ffn[l], name="peer_q")
        s1, e1, s2, e2, tau = _peer_route(pq, peer_keys[l].astype(F32))
        h = _peer_dense(h, norm_ffn[l], peer_u[l].astype(BF16), peer_v[l].astype(BF16), s1, e1, s2, e2, tau)
        for name, val in (('ssm_p', ssm_p), ('conv_p', conv_p), ('hg_p', hg_p), ('swk_p', swk_p),
                          ('swv_p', swv_p), ('ssm_s', ssm_s), ('conv_s', conv_s), ('hg_s', hg_s),
                          ('swk_s', swk_s), ('swv_s', swv_s)):
            outs[name].append(val)
        outs['mk_p'].append(mk.reshape(bp, MEM_TOKENS, MEM_HEADS, MEM_HD))
        outs['mv_p'].append(mv.reshape(bp, MEM_TOKENS, MEM_HEADS, MEM_HD))
    st = lambda k: jnp.stack(outs[k])
    return (h[:n_p].reshape(bp, tp, D_MODEL), h[n_p:].reshape(bs, ts, D_MODEL),
            st('ssm_p'), st('conv_p'), st('hg_p'), st('swk_p'), st('swv_p'), st('mk_p'), st('mv_p'),
            st('ssm_s'), st('conv_s'), st('hg_s'), st('swk_s'), st('swv_s'))
```

```python
import functools
import math

import jax
import jax.numpy as jnp
from jax import lax
from jax.experimental import pallas as pl
from jax.experimental.pallas import tpu as pltpu

F32 = jnp.float32
BF16 = jnp.bfloat16
HIGHEST = lax.Precision.HIGHEST

D_MODEL = 4096
PAST_LEN = 2048
CHUNK = 64
SSD_INNER = 2048
SSD_HD = 64
SSD_HEADS = 32
SSD_GROUPS = 8
SSD_STATE = 128
SSD_GW = SSD_INNER // SSD_GROUPS
SSD_REP = SSD_HEADS // SSD_GROUPS
HG_WIDTH = 1024
HG_HD = 128
HG_HEADS = 8
HG_BLOCK = 16
HG_LONG_BLOCK = 64
HG_SAFE_DECAY = 80.0
SWA_HD = 64
SWA_HEADS = 16
SWA_KV_HEADS = 2
SWA_REP = SWA_HEADS // SWA_KV_HEADS
WINDOW = 128
MEM_TOKENS = 256
MEM_HEADS = 4
MEM_HD = 128
MEM_WIDTH = 512
N_KEYS = 128
N_EXPERTS = N_KEYS * N_KEYS
PEER_HEADS = 8
PEER_QDIM = 256
PEER_TOPK = 16
EPS = 1e-6
NEG = -1e30
TINY = 1e-30
LANE = 128

COL_Z, COL_XS, COL_BC = 0, 2048, 4096
COL_HQ, COL_HF, COL_HI, COL_HG = 6144, 7168, 8192, 9216
COL_SQ, COL_SK, COL_SV, COL_DT = 10240, 11264, 11392, 11520
PROJ_COLS = 11776
YCOL_SSD, YCOL_HG, YCOL_SWA = 0, 2048, 3072

VMEM_LIMIT = 56 * 1024 * 1024

NT_DIMS = (((1,), (1,)), ((), ()))
TN_DIMS = (((0,), (0,)), ((), ()))


def _cparams(sem):
    return pltpu.CompilerParams(dimension_semantics=sem, vmem_limit_bytes=VMEM_LIMIT)


def _sigmoid(x):
    return 1.0 / (1.0 + jnp.exp(-x))


def _silu(x):
    return x * _sigmoid(x)


def _split_dot(x, ones_bf16):
    hi = x.astype(BF16)
    lo = (x - hi.astype(F32)).astype(BF16)
    return (jnp.dot(hi, ones_bf16, preferred_element_type=F32)
            + jnp.dot(lo, ones_bf16, preferred_element_type=F32))


def _mm_kernel(*refs, norm, head_norm, residual):
    it = iter(refs)
    x_ref = next(it)
    g_ref = next(it) if norm else None
    w_ref = next(it)
    hg_ref = next(it) if head_norm else None
    r_ref = next(it) if residual else None
    o_ref = next(it)
    xn_ref = next(it) if norm else None
    if norm:
        @pl.when(pl.program_id(1) == 0)
        def _():
            x = x_ref[...].astype(F32)
            ms = jnp.mean(x * x, axis=-1, keepdims=True)
            xn_ref[...] = (x * lax.rsqrt(ms + EPS) * g_ref[...]).astype(BF16)
        xb = xn_ref[...]
    else:
        xb = x_ref[...]
    acc = jnp.dot(xb, w_ref[...], preferred_element_type=F32)
    if head_norm:
        parts = []
        for c in range(acc.shape[1] // LANE):
            a = acc[:, c * LANE:(c + 1) * LANE]
            ms = jnp.mean(a * a, axis=-1, keepdims=True)
            parts.append(a * lax.rsqrt(ms + EPS))
        acc = jnp.concatenate(parts, axis=1) * hg_ref[...]
    if residual:
        acc = acc + r_ref[...]
    o_ref[...] = acc.astype(o_ref.dtype)


def _mm(x, w, *, layer=None, gain=None, head_gain=None, res=None, out_dtype=F32, tm=512, tn=512, name="mm"):
    M, K = x.shape
    N = w.shape[-1]
    tm, tn = min(tm, M), min(tn, N)
    assert M % tm == 0 and N % tn == 0
    norm, head_norm, residual = gain is not None, head_gain is not None, res is not None
    args, specs = [x], [pl.BlockSpec((tm, K), lambda i, j: (i, 0))]
    if norm:
        args.append(gain.reshape(1, K).astype(F32))
        specs.append(pl.BlockSpec((1, K), lambda i, j: (0, 0)))
    args.append(w)
    if layer is None:
        specs.append(pl.BlockSpec((K, tn), lambda i, j: (0, j)))
    else:
        specs.append(pl.BlockSpec((None, K, tn), lambda i, j: (layer, 0, j)))
    if head_norm:
        args.append(jnp.tile(head_gain.astype(F32), N // head_gain.shape[0]).reshape(1, N))
        specs.append(pl.BlockSpec((1, tn), lambda i, j: (0, j)))
    if residual:
        args.append(res)
        specs.append(pl.BlockSpec((tm, tn), lambda i, j: (i, j)))
    return pl.pallas_call(
        functools.partial(_mm_kernel, norm=norm, head_norm=head_norm, residual=residual),
        grid=(M // tm, N // tn),
        in_specs=specs,
        out_specs=pl.BlockSpec((tm, tn), lambda i, j: (i, j)),
        out_shape=jax.ShapeDtypeStruct((M, N), out_dtype),
        scratch_shapes=[pltpu.VMEM((tm, K), BF16)] if norm else [],
        compiler_params=_cparams(("parallel", "arbitrary")),
        name=name,
    )(*args)


def _ssd_kernel(*refs, L, has_state, aliased):
    (z_ref, xs_ref, bc_ref, dt_ref, cwx_ref, cwb_ref, cbx_ref, cbb_ref, dtb_ref, alog_ref,
     dsk_ref, gn_ref, exp_ref) = refs[:13]
    rest = refs[13:]
    if has_state:
        conv0_ref, h0_ref = rest[:2]
        rest = rest[2:]
    if aliased:
        rest = rest[1:]
    y_ref, convo_ref, ho_ref, xpx_ref, xpb_ref, hT_ref = rest
    c = pl.program_id(1)
    nc = pl.num_programs(1)

    @pl.when(c == 0)
    def _init():
        if has_state:
            xpx_ref[0:8, :] = jnp.zeros((8, SSD_INNER), F32)
            xpb_ref[0:8, :] = jnp.zeros((8, SSD_INNER), F32)
            xpx_ref[5:8, :] = conv0_ref[0, :, 0:SSD_INNER]
            xpb_ref[5:8, :] = conv0_ref[0, :, SSD_INNER:2 * SSD_INNER]
            for g in range(SSD_GROUPS):
                hT_ref[g] = h0_ref[0, g * SSD_GW:(g + 1) * SSD_GW, :].T
        else:
            xpx_ref[0:8, :] = jnp.zeros((8, SSD_INNER), F32)
            xpb_ref[0:8, :] = jnp.zeros((8, SSD_INNER), F32)
            hT_ref[...] = jnp.zeros(hT_ref.shape, F32)

    xpx_ref[8:8 + L, :] = xs_ref[...]
    xpb_ref[8:8 + L, :] = bc_ref[...]

    def conv(xp_ref, w_ref, b_ref):
        acc = b_ref[...] + w_ref[3:4, :] * xp_ref[8:8 + L, :]
        for j in range(3):
            acc = acc + w_ref[j:j + 1, :] * xp_ref[5 + j:5 + j + L, :]
        return acc

    xc = _silu(conv(xpx_ref, cwx_ref, cbx_ref))
    bcc = _silu(conv(xpb_ref, cwb_ref, cbb_ref))
    tail_x = xpx_ref[5 + L:8 + L, :]
    tail_b = xpb_ref[5 + L:8 + L, :]
    xpx_ref[5:8, :] = tail_x
    xpb_ref[5:8, :] = tail_b
    convo_ref[0, :, 0:SSD_INNER] = tail_x
    convo_ref[0, :, SSD_INNER:2 * SSD_INNER] = tail_b

    dtr = dt_ref[...] + dtb_ref[...]
    dt = jnp.maximum(dtr, 0.0) + jnp.log(1.0 + jnp.exp(-jnp.abs(dtr)))
    a = -jnp.exp(alog_ref[...])
    dta = dt * a
    ri = lax.broadcasted_iota(jnp.int32, (L, L), 0)
    ci = lax.broadcasted_iota(jnp.int32, (L, L), 1)
    causal = ri >= ci
    acum = jnp.dot(causal.astype(F32), dta, precision=HIGHEST, preferred_element_type=F32)
    both = jnp.concatenate([acum, dt], axis=0)
    p0 = both.astype(BF16)
    r1 = both - p0.astype(F32)
    p1 = r1.astype(BF16)
    p2 = (r1 - p1.astype(F32)).astype(BF16)
    expand = exp_ref[...]
    bothx = ((jnp.dot(p0, expand, preferred_element_type=F32)
              + jnp.dot(p1, expand, preferred_element_type=F32))
             + jnp.dot(p2, expand, preferred_element_type=F32))
    acx = bothx[:L, :]
    dtx = bothx[L:, :]
    eax = jnp.exp(acx)
    lastx = acx[L - 1:L, :]
    tailw = jnp.exp(lastx - acx) * dtx
    cdx = jnp.exp(lastx)
    acT = acum.T
    dtT = dt.T
    lane_head = lax.broadcasted_iota(jnp.int32, (L, SSD_GW), 1) // SSD_HD
    z = z_ref[...]

    for g in range(SSD_GROUPS):
        gs = slice(g * SSD_GW, (g + 1) * SSD_GW)
        Bg = bcc[:, g * SSD_STATE:(g + 1) * SSD_STATE].astype(BF16)
        Cg = bcc[:, SSD_GROUPS * SSD_STATE + g * SSD_STATE:
                 SSD_GROUPS * SSD_STATE + (g + 1) * SSD_STATE].astype(BF16)
        cb = lax.dot_general(Cg, Bg, NT_DIMS, preferred_element_type=F32)
        ws = []
        for r in range(SSD_REP):
            hd = g * SSD_REP + r
            seg = acum[:, hd:hd + 1] - acT[hd:hd + 1, :]
            dec = jnp.where(causal, jnp.exp(jnp.where(causal, seg, 0.0)), 0.0)
            ws.append(cb * dec * dtT[hd:hd + 1, :])
        wst = jnp.concatenate(ws, axis=0).astype(BF16)
        xg = xc[:, gs]
        full = jnp.dot(wst, xg.astype(BF16), preferred_element_type=F32)
        y_intra = jnp.zeros((L, SSD_GW), F32)
        for r in range(SSD_REP):
            y_intra = y_intra + jnp.where(lane_head == r, full[r * L:(r + 1) * L, :], 0.0)
        hTg = hT_ref[g]
        y_inter = jnp.dot(Cg, hTg.astype(BF16), preferred_element_type=F32) * eax[:, gs]
        xt = (xg * tailw[:, gs]).astype(BF16)
        hT_ref[g] = hTg * cdx[:, gs] + lax.dot_general(Bg, xt, TN_DIMS, preferred_element_type=F32)
        yg = y_intra + y_inter + dsk_ref[:, gs] * xg
        u = yg * _silu(z[:, gs])
        ms = jnp.mean(u * u, axis=-1, keepdims=True)
        y_ref[:, gs] = (u * lax.rsqrt(ms + EPS) * gn_ref[:, gs]).astype(y_ref.dtype)

    @pl.when(c == nc - 1)
    def _fin():
        for g in range(SSD_GROUPS):
            ho_ref[0, g * SSD_GW:(g + 1) * SSD_GW, :] = hT_ref[g].T


def _mixed_out(ycat, n_total, args, specs):
    aliases = {}
    if ycat is not None:
        args.append(ycat)
        specs.append(pl.BlockSpec(memory_space=pl.ANY))
        aliases = {len(args) - 1: 0}
    return jax.ShapeDtypeStruct((n_total, D_MODEL), BF16), aliases


def _ssd(proj, p, *, row0, B, T, conv0=None, h0=None, ycat=None):
    L = math.gcd(T, CHUNK)
    nc = T // L
    rb0 = row0 // L
    has_state = conv0 is not None

    def rows(col):
        return lambda b, c: (rb0 + b * nc + c, col)

    const = lambda b, c: (0, 0)
    args = [proj, proj, proj, proj, p['cwx'], p['cwb'], p['cbx'], p['cbb'], p['dtb'], p['alog'],
            p['dsk'], p['ssd_norm'], p['expand']]
    specs = [pl.BlockSpec((L, SSD_INNER), rows(COL_Z // SSD_INNER)),
             pl.BlockSpec((L, SSD_INNER), rows(COL_XS // SSD_INNER)),
             pl.BlockSpec((L, SSD_INNER), rows(COL_BC // SSD_INNER)),
             pl.BlockSpec((L, LANE), rows(COL_DT // LANE)),
             pl.BlockSpec((4, SSD_INNER), const), pl.BlockSpec((4, SSD_INNER), const),
             pl.BlockSpec((1, SSD_INNER), const), pl.BlockSpec((1, SSD_INNER), const),
             pl.BlockSpec((1, LANE), const), pl.BlockSpec((1, LANE), const),
             pl.BlockSpec((1, SSD_INNER), const), pl.BlockSpec((1, SSD_INNER), const),
             pl.BlockSpec((LANE, SSD_INNER), const)]
    if has_state:
        args += [conv0, h0.reshape(B, SSD_INNER, SSD_STATE)]
        specs += [pl.BlockSpec((1, 3, 2 * SSD_INNER), lambda b, c: (b, 0, 0)),
                  pl.BlockSpec((1, SSD_INNER, SSD_STATE), lambda b, c: (b, 0, 0))]
    yshape, aliases = _mixed_out(ycat, proj.shape[0], args, specs)
    y, convo, ho = pl.pallas_call(
        functools.partial(_ssd_kernel, L=L, has_state=has_state, aliased=ycat is not None),
        grid=(B, nc),
        in_specs=specs,
        out_specs=[pl.BlockSpec((L, SSD_INNER), rows(0)),
                   pl.BlockSpec((1, 3, 2 * SSD_INNER), lambda b, c: (b, 0, 0)),
                   pl.BlockSpec((1, SSD_INNER, SSD_STATE), lambda b, c: (b, 0, 0))],
        out_shape=[yshape,
                   jax.ShapeDtypeStruct((B, 3, 2 * SSD_INNER), F32),
                   jax.ShapeDtypeStruct((B, SSD_INNER, SSD_STATE), F32)],
        scratch_shapes=[pltpu.VMEM((8 + L, SSD_INNER), F32), pltpu.VMEM((8 + L, SSD_INNER), F32),
                        pltpu.VMEM((SSD_GROUPS, SSD_STATE, SSD_GW), F32)],
        input_output_aliases=aliases,
        compiler_params=_cparams(("parallel", "arbitrary")),
        name="ssd",
    )(*args)
    return y, convo, ho.reshape(B, SSD_HEADS, SSD_HD, SSD_STATE)


def _hgrn_kernel(*refs, Lc, has_state, aliased):
    q_ref, f_ref, i_ref, g_ref, lb_ref, gn_ref = refs[:6]
    rest = refs[6:]
    if has_state:
        s0_ref = rest[0]
        rest = rest[1:]
    if aliased:
        rest = rest[1:]
    y_ref, so_ref, kp_ref, gp_ref, vp_ref, st_ref, oi_ref = rest
    c = pl.program_id(1)
    nc = pl.num_programs(1)
    nb = Lc // HG_BLOCK

    @pl.when(c == 0)
    def _init():
        for h in range(HG_HEADS):
            st_ref[h] = s0_ref[0, h].T if has_state else jnp.zeros((HG_HD, HG_HD), F32)
        kp_ref[0:HG_BLOCK, :] = jnp.zeros((HG_BLOCK, HG_WIDTH), F32)
        gp_ref[0:HG_BLOCK, :] = jnp.zeros((HG_BLOCK, HG_WIDTH), F32)
        vp_ref[0:HG_BLOCK, :] = jnp.zeros((HG_BLOCK, HG_WIDTH), F32)

    zf = f_ref[...]
    lb = lb_ref[...]
    f = lb + (1.0 - lb) * _sigmoid(zf)
    logf = jnp.log(jnp.maximum(f, TINY))
    k = (1.0 - lb) * _sigmoid(-zf)
    q = _silu(q_ref[...]) * (HG_HD ** -0.5)
    v = i_ref[...]
    vb = v.astype(BF16)
    gate = _silu(g_ref[...])
    ri = lax.broadcasted_iota(jnp.int32, (Lc, Lc), 0)
    ci = lax.broadcasted_iota(jnp.int32, (Lc, Lc), 1)

    def log_decays(blk):
        same = (ri // blk) == (ci // blk)
        ltri = jnp.where(same, jnp.where(ri >= ci, 1.0, 0.0), 0.0)
        lall = jnp.where(same, 1.0, 0.0)
        return (ltri, jnp.dot(ltri, logf, precision=HIGHEST, preferred_element_type=F32),
                jnp.dot(lall, logf, precision=HIGHEST, preferred_element_type=F32))

    def pass_state(blk, g_cum, g_tot, intra):
        qg = (q * jnp.exp(g_cum)).astype(BF16)
        kdec = (k * jnp.exp(g_tot - g_cum)).astype(BF16)
        cd = jnp.exp(g_tot)
        for h in range(HG_HEADS):
            hs = slice(h * HG_HD, (h + 1) * HG_HD)
            st = st_ref[h]
            outs = []
            for b in range(Lc // blk):
                rs = slice(b * blk, (b + 1) * blk)
                outs.append(lax.dot_general(qg[rs, hs], st.astype(BF16), NT_DIMS, preferred_element_type=F32))
                st = st * cd[b * blk:b * blk + 1, hs] + lax.dot_general(
                    vb[rs, hs], kdec[rs, hs], TN_DIMS, preferred_element_type=F32)
            st_ref[h] = st
            o = intra(hs, qg) + (jnp.concatenate(outs, axis=0) if len(outs) > 1 else outs[0])
            ms = jnp.mean(o * o, axis=-1, keepdims=True)
            y_ref[:, hs] = (o * lax.rsqrt(ms + EPS) * gn_ref[:, hs] * gate[:, hs]).astype(y_ref.dtype)

    blk_long = min(HG_LONG_BLOCK, Lc)
    ltri_l, g_l, gt_l = log_decays(blk_long)
    worst = jnp.max(-gt_l)

    @pl.when(worst < HG_SAFE_DECAY)
    def _factorised():
        kgrow = (k * jnp.exp(-g_l)).astype(BF16)
        in_block = ltri_l > 0.5

        def intra(hs, qg):
            att = lax.dot_general(qg[:, hs], kgrow[:, hs], NT_DIMS, preferred_element_type=F32)
            att = jnp.where(in_block, att, 0.0).astype(BF16)
            return jnp.dot(att, vb[:, hs], preferred_element_type=F32)

        pass_state(blk_long, g_l, gt_l, intra)

    @pl.when(worst >= HG_SAFE_DECAY)
    def _unfactorised():
        _, gb, gl = log_decays(HG_BLOCK)
        kp_ref[HG_BLOCK:HG_BLOCK + Lc, :] = k
        gp_ref[HG_BLOCK:HG_BLOCK + Lc, :] = gb
        vp_ref[HG_BLOCK:HG_BLOCK + Lc, :] = v
        tpos = lax.broadcasted_iota(jnp.int32, (Lc, HG_WIDTH), 0) % HG_BLOCK
        pw = 2 * HG_HD
        oi = lax.broadcasted_iota(jnp.int32, (pw, pw), 0) // HG_HD
        oj = lax.broadcasted_iota(jnp.int32, (pw, pw), 1) // HG_HD
        ones2 = jnp.where(oi == oj, 1.0, 0.0).astype(BF16)
        o_pair = [jnp.zeros((Lc, pw), F32) for _ in range(HG_HEADS // 2)]
        for d in range(HG_BLOCK):
            lo = HG_BLOCK - d
            kd = kp_ref[lo:lo + Lc, :]
            gd = gp_ref[lo:lo + Lc, :]
            vd = vp_ref[lo:lo + Lc, :]
            m = tpos >= d
            dec = jnp.exp(jnp.where(m, gb - gd, 0.0))
            pr = jnp.where(m, q * kd * dec, 0.0).astype(BF16)
            for hp in range(HG_HEADS // 2):
                ps = slice(hp * pw, (hp + 1) * pw)
                o_pair[hp] = o_pair[hp] + jnp.dot(pr[:, ps], ones2, preferred_element_type=F32) * vd[:, ps]
        for hp in range(HG_HEADS // 2):
            oi_ref[:, hp * pw:(hp + 1) * pw] = o_pair[hp]
        pass_state(HG_BLOCK, gb, gl, lambda hs, qg: oi_ref[:, hs])

    @pl.when(c == nc - 1)
    def _fin():
        for h in range(HG_HEADS):
            so_ref[0, h] = st_ref[h].T


def _hgrn(proj, p, lb, *, row0, B, T, Lc, s0=None, ycat=None):
    nc = T // Lc
    rb0 = row0 // Lc
    has_state = s0 is not None

    def rows(col):
        return lambda b, c: (rb0 + b * nc + c, col // HG_WIDTH)

    const = lambda b, c: (0, 0)
    args = [proj, proj, proj, proj, lb, p['hgrn_norm']]
    specs = [pl.BlockSpec((Lc, HG_WIDTH), rows(COL_HQ)), pl.BlockSpec((Lc, HG_WIDTH), rows(COL_HF)),
             pl.BlockSpec((Lc, HG_WIDTH), rows(COL_HI)), pl.BlockSpec((Lc, HG_WIDTH), rows(COL_HG)),
             pl.BlockSpec((1, HG_WIDTH), const), pl.BlockSpec((1, HG_WIDTH), const)]
    if has_state:
        args.append(s0)
        specs.append(pl.BlockSpec((1, HG_HEADS, HG_HD, HG_HD), lambda b, c: (b, 0, 0, 0)))
    yshape, aliases = _mixed_out(ycat, proj.shape[0], args, specs)
    y, so = pl.pallas_call(
        functools.partial(_hgrn_kernel, Lc=Lc, has_state=has_state, aliased=ycat is not None),
        grid=(B, nc),
        in_specs=specs,
        out_specs=[pl.BlockSpec((Lc, HG_WIDTH), rows(YCOL_HG)),
                   pl.BlockSpec((1, HG_HEADS, HG_HD, HG_HD), lambda b, c: (b, 0, 0, 0))],
        out_shape=[yshape, jax.ShapeDtypeStruct((B, HG_HEADS, HG_HD, HG_HD), F32)],
        scratch_shapes=[pltpu.VMEM((HG_BLOCK + Lc, HG_WIDTH), F32)] * 3
                       + [pltpu.VMEM((HG_HEADS, HG_HD, HG_HD), F32), pltpu.VMEM((Lc, HG_WIDTH), F32)],
        input_output_aliases=aliases,
        compiler_params=_cparams(("parallel", "arbitrary")),
        name="hgrn",
    )(*args)
    return y, so


def _swa_kernel(*refs, Tq, prompt, aliased):
    sink_ref, q_ref, k_ref, v_ref, qg_ref, kg_ref = refs[:6]
    rest = refs[6:]
    if not prompt:
        ck_ref, cv_ref = rest[:2]
        rest = rest[2:]
    if aliased:
        rest = rest[1:]
    y_ref, ko_ref, vo_ref, bias_ref = rest[:4]
    if prompt:
        kp_ref, vp_ref = rest[4:]
    c = pl.program_id(1)
    Tk = WINDOW + Tq
    bi = lax.broadcasted_iota(jnp.int32, (LANE, LANE), 0) // SWA_HD
    bj = lax.broadcasted_iota(jnp.int32, (LANE, LANE), 1) // SWA_HD
    bd = jnp.where(bi == bj, 1.0, 0.0).astype(BF16)

    def hnorm(x, gain):
        outs = []
        for j in range(x.shape[1] // LANE):
            xs = x[:, j * LANE:(j + 1) * LANE]
            ms = _split_dot(xs * xs, bd) * (1.0 / SWA_HD)
            outs.append(xs * lax.rsqrt(ms + EPS))
        return (jnp.concatenate(outs, axis=1) if len(outs) > 1 else outs[0]) * gain

    qn = hnorm(q_ref[...], qg_ref[...])
    kn = hnorm(k_ref[...], kg_ref[...])
    vn = v_ref[...]
    if prompt:
        @pl.when(c == 0)
        def _init():
            kp_ref[...] = jnp.zeros((WINDOW, LANE), F32)
            vp_ref[...] = jnp.zeros((WINDOW, LANE), F32)
        kprev = kp_ref[...]
        vprev = vp_ref[...]
    else:
        kprev = ck_ref[0]
        vprev = cv_ref[0]
    kall = jnp.concatenate([kprev, kn], axis=0)
    vall = jnp.concatenate([vprev, vn], axis=0)
    knew = kall[Tq:, :]
    vnew = vall[Tq:, :]
    ko_ref[0] = knew
    vo_ref[0] = vnew
    if prompt:
        kp_ref[...] = knew
        vp_ref[...] = vnew
    ksw = pltpu.roll(kall, SWA_HD, 1)
    vsw = pltpu.roll(vall, SWA_HD, 1)
    lowk = lax.broadcasted_iota(jnp.int32, (Tk, LANE), 1) < SWA_HD
    R = SWA_REP * Tq
    rows_i = lax.broadcasted_iota(jnp.int32, (R, Tk), 0)
    cols_i = lax.broadcasted_iota(jnp.int32, (R, Tk), 1)
    rep_i = rows_i // Tq
    rep_c = lax.broadcasted_iota(jnp.int32, (R, 1), 0) // Tq

    @pl.when(c == 0)
    def _bias():
        dist = jnp.abs(rows_i % Tq + WINDOW - cols_i).astype(F32)
        for g in range(SWA_KV_HEADS):
            slope = jnp.zeros((R, Tk), F32)
            for r in range(SWA_REP):
                slope = jnp.where(rep_i == r, 2.0 ** (-8.0 * (g * SWA_REP + r + 1) / SWA_HEADS), slope)
            bias_ref[g] = slope * dist

    valid = (cols_i + c * Tq) >= WINDOW
    lane = lax.broadcasted_iota(jnp.int32, (Tq, LANE), 1)
    low = lane < SWA_HD
    for g in range(SWA_KV_HEADS):
        parts = []
        for r in range(SWA_REP):
            hd = g * SWA_REP + r
            qp = qn[:, (hd // 2) * LANE:(hd // 2 + 1) * LANE]
            parts.append(jnp.where(low if hd % 2 == 0 else jnp.logical_not(low), qp, 0.0))
        qst = jnp.concatenate(parts, axis=0).astype(BF16)
        kboth = (jnp.where(lowk, kall, ksw) if g == 0 else jnp.where(lowk, ksw, kall)).astype(BF16)
        vboth = (jnp.where(lowk, vall, vsw) if g == 0 else jnp.where(lowk, vsw, vall)).astype(BF16)
        sink = jnp.zeros((R, 1), F32)
        for r in range(SWA_REP):
            sink = jnp.where(rep_c == r, sink_ref[g * SWA_REP + r], sink)
        s = lax.dot_general(qst, kboth, NT_DIMS, preferred_element_type=F32) * (SWA_HD ** -0.5)
        s = s - bias_ref[g]
        if prompt:
            s = jnp.where(valid, s, NEG)
        mx = jnp.maximum(jnp.max(s, axis=-1, keepdims=True), sink)
        pe = jnp.exp(s - mx)
        if prompt:
            pe = jnp.where(valid, pe, 0.0)
        inv = 1.0 / (jnp.sum(pe, axis=-1, keepdims=True) + jnp.exp(sink - mx))
        o = jnp.dot((pe * inv).astype(BF16), vboth, preferred_element_type=F32)
        for jj in range(SWA_REP // 2):
            pair = g * (SWA_REP // 2) + jj
            y_ref[:, pair * LANE:(pair + 1) * LANE] = jnp.where(
                low, o[(2 * jj) * Tq:(2 * jj + 1) * Tq, :], o[(2 * jj + 1) * Tq:(2 * jj + 2) * Tq, :]
            ).astype(y_ref.dtype)


def _swa(proj, p, *, row0, B, T, ck=None, cv=None, ycat=None):
    prompt = ck is None
    Tq = CHUNK if prompt else T
    nc = T // Tq
    rb0 = row0 // Tq

    def rows(col, width):
        return lambda b, c: (rb0 + b * nc + c, col // width)

    const = lambda b, c: (0, 0)
    args = [p['swa_sinks'], proj, proj, proj, p['swa_qnorm'], p['swa_knorm']]
    specs = [pl.BlockSpec(memory_space=pltpu.SMEM),
             pl.BlockSpec((Tq, 1024), rows(COL_SQ, 1024)),
             pl.BlockSpec((Tq, LANE), rows(COL_SK, LANE)),
             pl.BlockSpec((Tq, LANE), rows(COL_SV, LANE)),
             pl.BlockSpec((1, 1024), const), pl.BlockSpec((1, LANE), const)]
    if not prompt:
        args += [ck.reshape(B, WINDOW, LANE), cv.reshape(B, WINDOW, LANE)]
        specs += [pl.BlockSpec((1, WINDOW, LANE), lambda b, c: (b, 0, 0))] * 2
    yshape, aliases = _mixed_out(ycat, proj.shape[0], args, specs)
    y, ko, vo = pl.pallas_call(
        functools.partial(_swa_kernel, Tq=Tq, prompt=prompt, aliased=ycat is not None),
        grid=(B, nc),
        in_specs=specs,
        out_specs=[pl.BlockSpec((Tq, 1024), rows(YCOL_SWA, 1024)),
                   pl.BlockSpec((1, WINDOW, LANE), lambda b, c: (b, 0, 0)),
                   pl.BlockSpec((1, WINDOW, LANE), lambda b, c: (b, 0, 0))],
        out_shape=[yshape,
                   jax.ShapeDtypeStruct((B, WINDOW, LANE), F32),
                   jax.ShapeDtypeStruct((B, WINDOW, LANE), F32)],
        scratch_shapes=[pltpu.VMEM((SWA_KV_HEADS, SWA_REP * Tq, WINDOW + Tq), F32)]
                       + ([pltpu.VMEM((WINDOW, LANE), F32)] * 2 if prompt else []),
        input_output_aliases=aliases,
        compiler_params=_cparams(("parallel", "arbitrary")),
        name="swa",
    )(*args)
    shp = (B, WINDOW, SWA_KV_HEADS, SWA_HD)
    return y, ko.reshape(shp), vo.reshape(shp)


def _mem_kernel(q_ref, k_ref, v_ref, *rest):
    o_ref = rest[-1]
    for h in range(MEM_HEADS):
        hs = slice(h * MEM_HD, (h + 1) * MEM_HD)
        s = lax.dot_general(q_ref[:, hs].astype(BF16), k_ref[:, hs].astype(BF16), NT_DIMS,
                            preferred_element_type=F32) * (MEM_HD ** -0.5)
        mx = jnp.max(s, axis=-1, keepdims=True)
        pe = jnp.exp(s - mx)
        pe = pe / jnp.sum(pe, axis=-1, keepdims=True)
        o_ref[:, hs] = jnp.dot(pe.astype(BF16), v_ref[:, hs].astype(BF16),
                               preferred_element_type=F32).astype(o_ref.dtype)


def _mem_attend(qn, mk, mv, *, row0, B, T, tq, out=None):
    nt = T // tq
    rb0 = row0 // tq
    args = [qn, mk, mv]
    specs = [pl.BlockSpec((tq, MEM_WIDTH), lambda b, t: (rb0 + b * nt + t, 0)),
             pl.BlockSpec((MEM_TOKENS, MEM_WIDTH), lambda b, t: (b, 0)),
             pl.BlockSpec((MEM_TOKENS, MEM_WIDTH), lambda b, t: (b, 0))]
    aliases = {}
    if out is not None:
        args.append(out)
        specs.append(pl.BlockSpec(memory_space=pl.ANY))
        aliases = {3: 0}
    return pl.pallas_call(
        _mem_kernel,
        grid=(B, nt),
        in_specs=specs,
        out_specs=pl.BlockSpec((tq, MEM_WIDTH), lambda b, t: (rb0 + b * nt + t, 0)),
        out_shape=jax.ShapeDtypeStruct((qn.shape[0], MEM_WIDTH), BF16),
        input_output_aliases=aliases,
        compiler_params=_cparams(("parallel", "arbitrary")),
        name="mem_attend",
    )(*args)


def _route_kernel(pq_ref, keys_ref, s1_ref, e1_ref, s2_ref, e2_ref, tau_ref):
    half = PEER_QDIM // 2
    ninf = -jnp.inf
    K = PEER_TOPK

    def top_distinct(s):
        cur, vals = s, []
        for _ in range(K):
            mx = jnp.max(cur, axis=0, keepdims=True)
            vals.append(mx)
            cur = jnp.where(cur == mx, ninf, cur)
        return vals

    def pick_first_max(cur, rows):
        mx = jnp.max(cur, axis=0, keepdims=True)
        first = jnp.min(jnp.where(cur == mx, rows, float(cur.shape[0])), axis=0, keepdims=True)
        return mx, rows == first

    def top_ranked(s):
        rows = lax.broadcasted_iota(jnp.int32, s.shape, 0).astype(F32)
        cur, vals, rank = s, [], jnp.full(s.shape, float(K), F32)
        for it in range(K):
            mx, hit = pick_first_max(cur, rows)
            vals.append(mx)
            rank = jnp.where(hit, float(it), rank)
            cur = jnp.where(hit, ninf, cur)
        return vals, rank

    def count_ge(x, thr):
        return jnp.sum(jnp.where(x >= thr, 1.0, 0.0), axis=0, keepdims=True)

    def all_cands(v1, v2m):
        return jnp.concatenate([v1[r] + v2m for r in range(K)], axis=0)

    def finish(h, s1, s2, v1, v2m, cand, tau):
        m0 = v1[0] + v2m[0:1, :]
        zsum = jnp.sum(jnp.where(cand >= tau, jnp.exp(cand - m0), 0.0), axis=0, keepdims=True)
        s1_ref[h] = jnp.where(s1 >= v1[K - 1], s1, ninf)
        s2_ref[h] = jnp.where(s2 >= v2m[K - 1:K, :], s2, ninf)
        e1_ref[h] = jnp.exp(s1 - v1[0])
        e2_ref[h] = jnp.exp(s2 - v2m[0:1, :]) / zsum
        tau_ref[h:h + 1, :] = tau

    tm = pq_ref.shape[0]

    def scores(h):
        q1 = pq_ref[:, h * PEER_QDIM:h * PEER_QDIM + half]
        q2 = pq_ref[:, h * PEER_QDIM + half:(h + 1) * PEER_QDIM]
        return (lax.dot_general(keys_ref[h, 0], q1, NT_DIMS, precision=HIGHEST, preferred_element_type=F32),
                lax.dot_general(keys_ref[h, 1], q2, NT_DIMS, precision=HIGHEST, preferred_element_type=F32))

    any_tie = []
    for h in range(PEER_HEADS):
        s1, s2 = scores(h)
        v1 = top_distinct(s1)
        v2 = top_distinct(s2)
        v2m = jnp.concatenate(v2, axis=0)
        rowi = lax.broadcasted_iota(jnp.int32, (8, tm), 0)
        blocks = [v1[0] + v2m, v1[1] + v2m[0:8, :]]
        for r in range(2, 8):
            blocks.append(jnp.where(rowi < K // (r + 1), v1[r] + v2m[0:8, :], ninf))
        blocks.append(jnp.concatenate(v1[8:], axis=0) + v2m[0:1, :])
        cand = jnp.concatenate(blocks, axis=0)
        cur = cand
        for _ in range(K):
            tau = jnp.max(cur, axis=0, keepdims=True)
            cur = jnp.where(cur == tau, ninf, cur)
        full = all_cands(v1, v2m)
        tied = jnp.where(count_ge(full, tau) == K,
                         jnp.where(count_ge(s1, v1[K - 1]) == K,
                                   jnp.where(count_ge(s2, v2[K - 1]) == K, 0.0, 1.0), 1.0), 1.0)
        finish(h, s1, s2, v1, v2m, cand, tau)
        any_tie.append(jnp.max(tied))

    for h in range(PEER_HEADS):
        @pl.when(any_tie[h] > 0.0)
        def _with_ties():
            s1, s2 = scores(h)
            w1, rank1 = top_ranked(s1)
            w2, rank2 = top_ranked(s2)
            w2m = jnp.concatenate(w2, axis=0)
            allc = all_cands(w1, w2m)
            rows = lax.broadcasted_iota(jnp.int32, allc.shape, 0).astype(F32)
            cur, picked = allc, jnp.zeros(allc.shape, F32)
            for _ in range(K):
                _, hit = pick_first_max(cur, rows)
                picked = jnp.where(hit, 1.0, picked)
                cur = jnp.where(hit, ninf, cur)
            m0 = w1[0] + w2m[0:1, :]
            zsum = jnp.sum(picked * jnp.exp(allc - m0), axis=0, keepdims=True)
            lmap = jnp.zeros(s1.shape, F32)
            for r in range(K):
                length = jnp.sum(picked[r * K:(r + 1) * K, :], axis=0, keepdims=True)
                lmap = jnp.where(rank1 == float(r), length, lmap)
            s1_ref[h] = lmap
            s2_ref[h] = jnp.where(rank2 < float(K), -rank2, -1000.0)
            e1_ref[h] = jnp.exp(s1 - w1[0])
            e2_ref[h] = jnp.exp(s2 - w2m[0:1, :]) / zsum
            tau_ref[h:h + 1, :] = jnp.ones((1, tm), F32)


def _peer_route(pq, keys, *, tm=256):
    N = pq.shape[0]
    big = jax.ShapeDtypeStruct((PEER_HEADS, N_KEYS, N), F32)
    bspec = pl.BlockSpec((PEER_HEADS, N_KEYS, tm), lambda i: (0, 0, i))
    return pl.pallas_call(
        _route_kernel,
        grid=(N // tm,),
        in_specs=[pl.BlockSpec((tm, PEER_HEADS * PEER_QDIM), lambda i: (i, 0)),
                  pl.BlockSpec((PEER_HEADS, 2, N_KEYS, PEER_QDIM // 2), lambda i: (0, 0, 0, 0))],
        out_specs=[bspec, bspec, bspec, bspec, pl.BlockSpec((PEER_HEADS, tm), lambda i: (0, i))],
        out_shape=[big, big, big, big, jax.ShapeDtypeStruct((PEER_HEADS, N), F32)],
        compiler_params=_cparams(("parallel",)),
        name="peer_route",
    )(pq, keys)


def _dense_kernel(h_ref, g_ref, u_ref, v_ref, s1_ref, e1_ref, s2_ref, e2_ref, tau_ref, o_ref, xn_ref, c_ref,
                  *, te):
    j = pl.program_id(1)
    na = te // N_KEYS

    @pl.when(j == 0)
    def _init():
        x = h_ref[...]
        ms = jnp.mean(x * x, axis=-1, keepdims=True)
        xn_ref[...] = (x * lax.rsqrt(ms + EPS) * g_ref[...]).astype(BF16)
        o_ref[...] = x

    pre = lax.dot_general(xn_ref[...], u_ref[...], NT_DIMS, preferred_element_type=F32)
    act = 0.5 * pre * (1.0 + lax.erf(pre * (2.0 ** -0.5)))
    for al in range(na):
        a = j * na + al
        gt = jnp.zeros((N_KEYS, h_ref.shape[0]), F32)
        for h in range(PEER_HEADS):
            s1r = s1_ref[h, pl.ds(a, 1), :]
            e1r = e1_ref[h, pl.ds(a, 1), :]
            hit = (s1r + s2_ref[h]) >= tau_ref[h:h + 1, :]
            gt = gt + jnp.where(hit, e1r * e2_ref[h], 0.0)
        c_ref[:, al * N_KEYS:(al + 1) * N_KEYS] = (gt.T * act[:, al * N_KEYS:(al + 1) * N_KEYS]).astype(BF16)
    o_ref[...] += jnp.dot(c_ref[...], v_ref[...], preferred_element_type=F32)


def _peer_dense(h, gain, u, v, layer, s1, e1, s2, e2, tau, *, tm=512, te=512):
    N, D = h.shape
    one = pl.Buffered(1)
    rt = pl.BlockSpec((PEER_HEADS, N_KEYS, tm), lambda i, j: (0, 0, i), pipeline_mode=one)
    return pl.pallas_call(
        functools.partial(_dense_kernel, te=te),
        grid=(N // tm, N_EXPERTS // te),
        in_specs=[pl.BlockSpec((tm, D), lambda i, j: (i, 0), pipeline_mode=one),
                  pl.BlockSpec((1, D), lambda i, j: (0, 0)),
                  pl.BlockSpec((None, te, D), lambda i, j: (layer, j, 0)),
                  pl.BlockSpec((None, te, D), lambda i, j: (layer, j, 0)),
                  rt, rt, rt, rt,
                  pl.BlockSpec((PEER_HEADS, tm), lambda i, j: (0, i), pipeline_mode=one)],
        out_specs=pl.BlockSpec((tm, D), lambda i, j: (i, 0), pipeline_mode=one),
        out_shape=jax.ShapeDtypeStruct((N, D), F32),
        scratch_shapes=[pltpu.VMEM((tm, D), BF16), pltpu.VMEM((tm, te), BF16)],
        compiler_params=_cparams(("parallel", "arbitrary")),
        name="peer_dense",
    )(h, gain.reshape(1, D).astype(F32), u, v, s1, e1, s2, e2, tau)


def _row(x, width=None):
    x = x.astype(F32).reshape(1, -1)
    if width is not None and x.shape[1] < width:
        x = jnp.pad(x, ((0, 0), (0, width - x.shape[1])))
    return x


def _pack_w_in(w_in):
    xbc_end = 3 * SSD_INNER
    dt_end = xbc_end + SSD_HEADS
    wb = w_in.astype(BF16)
    pad = jnp.zeros(w_in.shape[:2] + (PROJ_COLS - w_in.shape[2],), BF16)
    return jnp.concatenate([wb[..., :xbc_end], wb[..., dt_end:], wb[..., xbc_end:dt_end], pad], axis=-1)


def _layer_params(l, conv_w, conv_b, dt_bias, a_log, d_skip, ssd_norm, hgrn_norm, swa_qnorm,
                  swa_knorm, swa_sinks):
    eh = lax.broadcasted_iota(jnp.int32, (LANE, SSD_INNER), 0)
    ec = lax.broadcasted_iota(jnp.int32, (LANE, SSD_INNER), 1)
    return {
        'expand': (ec // SSD_HD == eh).astype(BF16),
        'cwx': conv_w[l][:, :SSD_INNER].astype(F32), 'cwb': conv_w[l][:, SSD_INNER:].astype(F32),
        'cbx': _row(conv_b[l][:SSD_INNER]), 'cbb': _row(conv_b[l][SSD_INNER:]),
        'dtb': _row(dt_bias[l], LANE), 'alog': _row(a_log[l], LANE),
        'dsk': _row(jnp.repeat(d_skip[l], SSD_HD)), 'ssd_norm': _row(ssd_norm[l]),
        'hgrn_norm': _row(jnp.tile(hgrn_norm[l], HG_HEADS)),
        'swa_qnorm': _row(jnp.tile(swa_qnorm[l], SWA_HEADS)),
        'swa_knorm': _row(jnp.tile(swa_knorm[l], SWA_KV_HEADS)),
        'swa_sinks': swa_sinks[l].astype(F32),
    }


def kernel(x_prompt, x_sample, mem_prompt, state_ssm, state_ssd_conv, state_hgrn, cache_swa_k, cache_swa_v,
           cache_mem_k, cache_mem_v, norm_mix, w_in, conv_w, conv_b, dt_bias, a_log, d_skip, ssd_norm,
           hgrn_lb, hgrn_norm, swa_qnorm, swa_knorm, swa_sinks, w_out, norm_mem, norm_memtok, w_mq, w_mk,
           w_mv, mem_qnorm, mem_knorm, w_mo, norm_ffn, w_pq, peer_keys, peer_u, peer_v):
    depth = w_in.shape[0]
    bp, tp, _ = x_prompt.shape
    bs, ts, _ = x_sample.shape
    n_p, n_s = bp * tp, bs * ts
    lbp = jax.nn.softmax(hgrn_lb.astype(F32), axis=0)
    lower = jnp.cumsum(lbp, axis=0) - lbp[0:1]
    h = jnp.concatenate([x_prompt.reshape(n_p, D_MODEL), x_sample.reshape(n_s, D_MODEL)], axis=0)
    mem2d = mem_prompt.reshape(bp * MEM_TOKENS, D_MODEL)
    outs = {k: [] for k in ('ssm_p', 'conv_p', 'hg_p', 'swk_p', 'swv_p', 'mk_p', 'mv_p',
                            'ssm_s', 'conv_s', 'hg_s', 'swk_s', 'swv_s')}
    w_in_b = _pack_w_in(w_in)
    w_out_b, w_mq_b, w_mk_b, w_mv_b, w_mo_b, w_pq_b, u_b, v_b = (
        t.astype(BF16) for t in (w_out, w_mq, w_mk, w_mv, w_mo, w_pq, peer_u, peer_v))
    for l in range(depth):
        p = _layer_params(l, conv_w, conv_b, dt_bias, a_log, d_skip, ssd_norm, hgrn_norm,
                          swa_qnorm, swa_knorm, swa_sinks)
        lb = _row(lower[l])
        proj = _mm(h, w_in_b, layer=l, gain=norm_mix[l], name="proj_in")
        ycat, conv_p, ssm_p = _ssd(proj, p, row0=0, B=bp, T=tp)
        ycat, conv_s, ssm_s = _ssd(proj, p, row0=n_p, B=bs, T=ts, conv0=state_ssd_conv[l],
                                   h0=state_ssm[l], ycat=ycat)
        ycat, hg_p = _hgrn(proj, p, lb, row0=0, B=bp, T=tp, Lc=128, ycat=ycat)
        ycat, hg_s = _hgrn(proj, p, lb, row0=n_p, B=bs, T=ts, Lc=ts, s0=state_hgrn[l], ycat=ycat)
        ycat, swk_p, swv_p = _swa(proj, p, row0=0, B=bp, T=tp, ycat=ycat)
        ycat, swk_s, swv_s = _swa(proj, p, row0=n_p, B=bs, T=ts, ck=cache_swa_k[l], cv=cache_swa_v[l],
                                  ycat=ycat)
        h = _mm(ycat, w_out_b, layer=l, res=h, name="proj_out")
        mk = _mm(mem2d, w_mk_b, layer=l, gain=norm_memtok[l], head_gain=mem_knorm[l], name="mem_k")
        mv = _mm(mem2d, w_mv_b, layer=l, gain=norm_memtok[l], name="mem_v")
        qn = _mm(h, w_mq_b, layer=l, gain=norm_mem[l], head_gain=mem_qnorm[l], name="mem_q")
        om = _mem_attend(qn, mk, mv, row0=0, B=bp, T=tp, tq=512)
        om = _mem_attend(qn, cache_mem_k[l].reshape(bs * MEM_TOKENS, MEM_WIDTH),
                         cache_mem_v[l].reshape(bs * MEM_TOKENS, MEM_WIDTH), row0=n_p, B=bs, T=ts, tq=ts,
                         out=om)
        h = _mm(om, w_mo_b, layer=l, res=h, name="mem_o")
        pq = _mm(h, w_pq_b, layer=l, gain=norm_ffn[l], name="peer_q")
        s1, e1, s2, e2, tau = _peer_route(pq, peer_keys[l].astype(F32))
        h = _peer_dense(h, norm_ffn[l], u_b, v_b, l, s1, e1, s2, e2, tau)
        for name, val in (('ssm_p', ssm_p), ('conv_p', conv_p), ('hg_p', hg_p), ('swk_p', swk_p),
                          ('swv_p', swv_p), ('ssm_s', ssm_s), ('conv_s', conv_s), ('hg_s', hg_s),
                          ('swk_s', swk_s), ('swv_s', swv_s)):
            outs[name].append(val)
        outs['mk_p'].append(mk.reshape(bp, MEM_TOKENS, MEM_HEADS, MEM_HD))
        outs['mv_p'].append(mv.reshape(bp, MEM_TOKENS, MEM_HEADS, MEM_HD))
    st = lambda k: jnp.stack(outs[k])
    return (h[:n_p].reshape(bp, tp, D_MODEL), h[n_p:].reshape(bs, ts, D_MODEL),
            st('ssm_p'), st('conv_p'), st('hg_p'), st('swk_p'), st('swv_p'), st('mk_p'), st('mv_p'),
            st('ssm_s'), st('conv_s'), st('hg_s'), st('swk_s'), st('swv_s'))
```

```python
import functools
import math

import jax
import jax.numpy as jnp
from jax import lax
from jax.experimental import pallas as pl
from jax.experimental.pallas import tpu as pltpu

F32 = jnp.float32
BF16 = jnp.bfloat16
HIGHEST = lax.Precision.HIGHEST

D_MODEL = 4096
PAST_LEN = 2048
CHUNK = 64
SSD_INNER = 2048
SSD_HD = 64
SSD_HEADS = 32
SSD_GROUPS = 8
SSD_STATE = 128
SSD_GW = SSD_INNER // SSD_GROUPS
SSD_REP = SSD_HEADS // SSD_GROUPS
HG_WIDTH = 1024
HG_HD = 128
HG_HEADS = 8
HG_BLOCK = 16
HG_LONG_BLOCK = 64
HG_SAFE_DECAY = 80.0
SWA_HD = 64
SWA_HEADS = 16
SWA_KV_HEADS = 2
SWA_REP = SWA_HEADS // SWA_KV_HEADS
WINDOW = 128
MEM_TOKENS = 256
MEM_HEADS = 4
MEM_HD = 128
MEM_WIDTH = 512
N_KEYS = 128
N_EXPERTS = N_KEYS * N_KEYS
PEER_HEADS = 8
PEER_QDIM = 256
PEER_TOPK = 16
EPS = 1e-6
NEG = -1e30
TINY = 1e-30
LANE = 128
MXU_N = 256

COL_Z, COL_XS, COL_BC = 0, 2048, 4096
COL_HQ, COL_HF, COL_HI, COL_HG = 6144, 7168, 8192, 9216
COL_SQ, COL_SK, COL_SV, COL_DT = 10240, 11264, 11392, 11520
PROJ_COLS = 11776
YCOL_SSD, YCOL_HG, YCOL_SWA = 0, 2048, 3072

VMEM_LIMIT = 60 * 1024 * 1024

NT_DIMS = (((1,), (1,)), ((), ()))
TN_DIMS = (((0,), (0,)), ((), ()))


def _cparams(sem):
    return pltpu.CompilerParams(dimension_semantics=sem, vmem_limit_bytes=VMEM_LIMIT)


def _sigmoid(x):
    return 1.0 / (1.0 + jnp.exp(-x))


def _silu(x):
    return x * _sigmoid(x)


def _split_dot(x, ones_bf16):
    hi = x.astype(BF16)
    lo = (x - hi.astype(F32)).astype(BF16)
    return (jnp.dot(hi, ones_bf16, preferred_element_type=F32)
            + jnp.dot(lo, ones_bf16, preferred_element_type=F32))


def _mm_kernel(*refs, norm, head_norm, residual):
    it = iter(refs)
    x_ref = next(it)
    g_ref = next(it) if norm else None
    w_ref = next(it)
    hg_ref = next(it) if head_norm else None
    r_ref = next(it) if residual else None
    o_ref = next(it)
    xn_ref = next(it) if norm else None
    if norm:
        @pl.when(pl.program_id(1) == 0)
        def _():
            x = x_ref[...].astype(F32)
            ms = jnp.mean(x * x, axis=-1, keepdims=True)
            xn_ref[...] = (x * lax.rsqrt(ms + EPS) * g_ref[...]).astype(BF16)
        xb = xn_ref[...]
    else:
        xb = x_ref[...]
    acc = jnp.dot(xb, w_ref[...], preferred_element_type=F32)
    if head_norm:
        parts = []
        for c in range(acc.shape[1] // LANE):
            a = acc[:, c * LANE:(c + 1) * LANE]
            ms = jnp.mean(a * a, axis=-1, keepdims=True)
            parts.append(a * lax.rsqrt(ms + EPS))
        acc = jnp.concatenate(parts, axis=1) * hg_ref[...]
    if residual:
        acc = acc + r_ref[...]
    o_ref[...] = acc.astype(o_ref.dtype)


def _mm(x, w, *, layer=None, gain=None, head_gain=None, res=None, out_dtype=F32, tm=512, tn=512, name="mm"):
    M, K = x.shape
    N = w.shape[-1]
    tm, tn = min(tm, M), min(tn, N)
    assert M % tm == 0 and N % tn == 0
    norm, head_norm, residual = gain is not None, head_gain is not None, res is not None
    args, specs = [x], [pl.BlockSpec((tm, K), lambda i, j: (i, 0))]
    if norm:
        args.append(gain.reshape(1, K).astype(F32))
        specs.append(pl.BlockSpec((1, K), lambda i, j: (0, 0)))
    args.append(w)
    if layer is None:
        specs.append(pl.BlockSpec((K, tn), lambda i, j: (0, j)))
    else:
        specs.append(pl.BlockSpec((None, K, tn), lambda i, j: (layer, 0, j)))
    if head_norm:
        args.append(jnp.tile(head_gain.astype(F32), N // head_gain.shape[0]).reshape(1, N))
        specs.append(pl.BlockSpec((1, tn), lambda i, j: (0, j)))
    if residual:
        args.append(res)
        specs.append(pl.BlockSpec((tm, tn), lambda i, j: (i, j)))
    return pl.pallas_call(
        functools.partial(_mm_kernel, norm=norm, head_norm=head_norm, residual=residual),
        grid=(M // tm, N // tn),
        in_specs=specs,
        out_specs=pl.BlockSpec((tm, tn), lambda i, j: (i, j)),
        out_shape=jax.ShapeDtypeStruct((M, N), out_dtype),
        scratch_shapes=[pltpu.VMEM((tm, K), BF16)] if norm else [],
        compiler_params=_cparams(("parallel", "arbitrary")),
        name=name,
    )(*args)


def _ssd_kernel(*refs, L, has_state, aliased):
    (z_ref, xs_ref, bc_ref, dt_ref, cwx_ref, cwb_ref, cbx_ref, cbb_ref, dtb_ref, alog_ref,
     dsk_ref, gn_ref, exp_ref) = refs[:13]
    rest = refs[13:]
    if has_state:
        conv0_ref, h0_ref = rest[:2]
        rest = rest[2:]
    if aliased:
        rest = rest[1:]
    y_ref, convo_ref, ho_ref, xpx_ref, xpb_ref, hT_ref = rest
    c = pl.program_id(1)
    nc = pl.num_programs(1)

    @pl.when(c == 0)
    def _init():
        if has_state:
            xpx_ref[0:8, :] = jnp.zeros((8, SSD_INNER), F32)
            xpb_ref[0:8, :] = jnp.zeros((8, SSD_INNER), F32)
            xpx_ref[5:8, :] = conv0_ref[0, :, 0:SSD_INNER]
            xpb_ref[5:8, :] = conv0_ref[0, :, SSD_INNER:2 * SSD_INNER]
            for g in range(SSD_GROUPS):
                hT_ref[g] = h0_ref[0, g * SSD_GW:(g + 1) * SSD_GW, :].T
        else:
            xpx_ref[0:8, :] = jnp.zeros((8, SSD_INNER), F32)
            xpb_ref[0:8, :] = jnp.zeros((8, SSD_INNER), F32)
            hT_ref[...] = jnp.zeros(hT_ref.shape, F32)

    xpx_ref[8:8 + L, :] = xs_ref[...]
    xpb_ref[8:8 + L, :] = bc_ref[...]

    def conv(xp_ref, w_ref, b_ref):
        acc = b_ref[...] + w_ref[3:4, :] * xp_ref[8:8 + L, :]
        for j in range(3):
            acc = acc + w_ref[j:j + 1, :] * xp_ref[5 + j:5 + j + L, :]
        return acc

    xc = _silu(conv(xpx_ref, cwx_ref, cbx_ref))
    bcc = _silu(conv(xpb_ref, cwb_ref, cbb_ref))
    tail_x = xpx_ref[5 + L:8 + L, :]
    tail_b = xpb_ref[5 + L:8 + L, :]
    xpx_ref[5:8, :] = tail_x
    xpb_ref[5:8, :] = tail_b
    convo_ref[0, :, 0:SSD_INNER] = tail_x
    convo_ref[0, :, SSD_INNER:2 * SSD_INNER] = tail_b

    dtr = dt_ref[...] + dtb_ref[...]
    dt = jnp.maximum(dtr, 0.0) + jnp.log(1.0 + jnp.exp(-jnp.abs(dtr)))
    a = -jnp.exp(alog_ref[...])
    dta = dt * a
    ri = lax.broadcasted_iota(jnp.int32, (L, L), 0)
    ci = lax.broadcasted_iota(jnp.int32, (L, L), 1)
    causal = ri >= ci
    acum = jnp.dot(causal.astype(F32), dta, precision=HIGHEST, preferred_element_type=F32)
    both = jnp.concatenate([acum, dt], axis=0)
    p0 = both.astype(BF16)
    r1 = both - p0.astype(F32)
    p1 = r1.astype(BF16)
    p2 = (r1 - p1.astype(F32)).astype(BF16)
    expand = exp_ref[...]
    bothx = ((jnp.dot(p0, expand, preferred_element_type=F32)
              + jnp.dot(p1, expand, preferred_element_type=F32))
             + jnp.dot(p2, expand, preferred_element_type=F32))
    acx = bothx[:L, :]
    dtx = bothx[L:, :]
    eax = jnp.exp(acx)
    lastx = acx[L - 1:L, :]
    tailw = jnp.exp(lastx - acx) * dtx
    cdx = jnp.exp(lastx)
    acT = acum.T
    dtT = dt.T
    lane_head = lax.broadcasted_iota(jnp.int32, (L, SSD_GW), 1) // SSD_HD
    z = z_ref[...]

    for g in range(SSD_GROUPS):
        gs = slice(g * SSD_GW, (g + 1) * SSD_GW)
        Bg = bcc[:, g * SSD_STATE:(g + 1) * SSD_STATE].astype(BF16)
        Cg = bcc[:, SSD_GROUPS * SSD_STATE + g * SSD_STATE:
                 SSD_GROUPS * SSD_STATE + (g + 1) * SSD_STATE].astype(BF16)
        cb = lax.dot_general(Cg, Bg, NT_DIMS, preferred_element_type=F32)
        ws = []
        for r in range(SSD_REP):
            hd = g * SSD_REP + r
            seg = acum[:, hd:hd + 1] - acT[hd:hd + 1, :]
            dec = jnp.where(causal, jnp.exp(jnp.where(causal, seg, 0.0)), 0.0)
            ws.append(cb * dec * dtT[hd:hd + 1, :])
        wst = jnp.concatenate(ws, axis=0).astype(BF16)
        xg = xc[:, gs]
        full = jnp.dot(wst, xg.astype(BF16), preferred_element_type=F32)
        y_intra = jnp.zeros((L, SSD_GW), F32)
        for r in range(SSD_REP):
            y_intra = y_intra + jnp.where(lane_head == r, full[r * L:(r + 1) * L, :], 0.0)
        hTg = hT_ref[g]
        y_inter = jnp.dot(Cg, hTg.astype(BF16), preferred_element_type=F32) * eax[:, gs]
        xt = (xg * tailw[:, gs]).astype(BF16)
        hT_ref[g] = hTg * cdx[:, gs] + lax.dot_general(Bg, xt, TN_DIMS, preferred_element_type=F32)
        yg = y_intra + y_inter + dsk_ref[:, gs] * xg
        u = yg * _silu(z[:, gs])
        ms = jnp.mean(u * u, axis=-1, keepdims=True)
        y_ref[:, gs] = (u * lax.rsqrt(ms + EPS) * gn_ref[:, gs]).astype(y_ref.dtype)

    @pl.when(c == nc - 1)
    def _fin():
        for g in range(SSD_GROUPS):
            ho_ref[0, g * SSD_GW:(g + 1) * SSD_GW, :] = hT_ref[g].T


def _mixed_out(ycat, n_total, args, specs):
    aliases = {}
    if ycat is not None:
        args.append(ycat)
        specs.append(pl.BlockSpec(memory_space=pl.ANY))
        aliases = {len(args) - 1: 0}
    return jax.ShapeDtypeStruct((n_total, D_MODEL), BF16), aliases


def _ssd(proj, p, *, row0, B, T, conv0=None, h0=None, ycat=None):
    L = math.gcd(T, CHUNK)
    nc = T // L
    rb0 = row0 // L
    has_state = conv0 is not None

    def rows(col):
        return lambda b, c: (rb0 + b * nc + c, col)

    const = lambda b, c: (0, 0)
    args = [proj, proj, proj, proj, p['cwx'], p['cwb'], p['cbx'], p['cbb'], p['dtb'], p['alog'],
            p['dsk'], p['ssd_norm'], p['expand']]
    specs = [pl.BlockSpec((L, SSD_INNER), rows(COL_Z // SSD_INNER)),
             pl.BlockSpec((L, SSD_INNER), rows(COL_XS // SSD_INNER)),
             pl.BlockSpec((L, SSD_INNER), rows(COL_BC // SSD_INNER)),
             pl.BlockSpec((L, LANE), rows(COL_DT // LANE)),
             pl.BlockSpec((4, SSD_INNER), const), pl.BlockSpec((4, SSD_INNER), const),
             pl.BlockSpec((1, SSD_INNER), const), pl.BlockSpec((1, SSD_INNER), const),
             pl.BlockSpec((1, LANE), const), pl.BlockSpec((1, LANE), const),
             pl.BlockSpec((1, SSD_INNER), const), pl.BlockSpec((1, SSD_INNER), const),
             pl.BlockSpec((LANE, SSD_INNER), const)]
    if has_state:
        args += [conv0, h0.reshape(B, SSD_INNER, SSD_STATE)]
        specs += [pl.BlockSpec((1, 3, 2 * SSD_INNER), lambda b, c: (b, 0, 0)),
                  pl.BlockSpec((1, SSD_INNER, SSD_STATE), lambda b, c: (b, 0, 0))]
    yshape, aliases = _mixed_out(ycat, proj.shape[0], args, specs)
    y, convo, ho = pl.pallas_call(
        functools.partial(_ssd_kernel, L=L, has_state=has_state, aliased=ycat is not None),
        grid=(B, nc),
        in_specs=specs,
        out_specs=[pl.BlockSpec((L, SSD_INNER), rows(0)),
                   pl.BlockSpec((1, 3, 2 * SSD_INNER), lambda b, c: (b, 0, 0)),
                   pl.BlockSpec((1, SSD_INNER, SSD_STATE), lambda b, c: (b, 0, 0))],
        out_shape=[yshape,
                   jax.ShapeDtypeStruct((B, 3, 2 * SSD_INNER), F32),
                   jax.ShapeDtypeStruct((B, SSD_INNER, SSD_STATE), F32)],
        scratch_shapes=[pltpu.VMEM((8 + L, SSD_INNER), F32), pltpu.VMEM((8 + L, SSD_INNER), F32),
                        pltpu.VMEM((SSD_GROUPS, SSD_STATE, SSD_GW), F32)],
        input_output_aliases=aliases,
        compiler_params=_cparams(("parallel", "arbitrary")),
        name="ssd",
    )(*args)
    return y, convo, ho.reshape(B, SSD_HEADS, SSD_HD, SSD_STATE)


def _hgrn_kernel(*refs, Lc, has_state, aliased):
    q_ref, f_ref, i_ref, g_ref, lb_ref, gn_ref = refs[:6]
    rest = refs[6:]
    if has_state:
        s0_ref = rest[0]
        rest = rest[1:]
    if aliased:
        rest = rest[1:]
    y_ref, so_ref, kp_ref, gp_ref, vp_ref, st_ref, oi_ref = rest
    c = pl.program_id(1)
    nc = pl.num_programs(1)
    nb = Lc // HG_BLOCK

    @pl.when(c == 0)
    def _init():
        for h in range(HG_HEADS):
            st_ref[h] = s0_ref[0, h].T if has_state else jnp.zeros((HG_HD, HG_HD), F32)
        kp_ref[0:HG_BLOCK, :] = jnp.zeros((HG_BLOCK, HG_WIDTH), F32)
        gp_ref[0:HG_BLOCK, :] = jnp.zeros((HG_BLOCK, HG_WIDTH), F32)
        vp_ref[0:HG_BLOCK, :] = jnp.zeros((HG_BLOCK, HG_WIDTH), F32)

    zf = f_ref[...]
    lb = lb_ref[...]
    f = lb + (1.0 - lb) * _sigmoid(zf)
    logf = jnp.log(jnp.maximum(f, TINY))
    k = (1.0 - lb) * _sigmoid(-zf)
    q = _silu(q_ref[...]) * (HG_HD ** -0.5)
    v = i_ref[...]
    vb = v.astype(BF16)
    gate = _silu(g_ref[...])
    ri = lax.broadcasted_iota(jnp.int32, (Lc, Lc), 0)
    ci = lax.broadcasted_iota(jnp.int32, (Lc, Lc), 1)

    def log_decays(blk):
        same = (ri // blk) == (ci // blk)
        ltri = jnp.where(same, jnp.where(ri >= ci, 1.0, 0.0), 0.0)
        lall = jnp.where(same, 1.0, 0.0)
        return (ltri, jnp.dot(ltri, logf, precision=HIGHEST, preferred_element_type=F32),
                jnp.dot(lall, logf, precision=HIGHEST, preferred_element_type=F32))

    def pass_state(blk, g_cum, g_tot, intra):
        qg = (q * jnp.exp(g_cum)).astype(BF16)
        kdec = (k * jnp.exp(g_tot - g_cum)).astype(BF16)
        cd = jnp.exp(g_tot)
        for h in range(HG_HEADS):
            hs = slice(h * HG_HD, (h + 1) * HG_HD)
            st = st_ref[h]
            outs = []
            for b in range(Lc // blk):
                rs = slice(b * blk, (b + 1) * blk)
                outs.append(lax.dot_general(qg[rs, hs], st.astype(BF16), NT_DIMS, preferred_element_type=F32))
                st = st * cd[b * blk:b * blk + 1, hs] + lax.dot_general(
                    vb[rs, hs], kdec[rs, hs], TN_DIMS, preferred_element_type=F32)
            st_ref[h] = st
            o = intra(hs, qg) + (jnp.concatenate(outs, axis=0) if len(outs) > 1 else outs[0])
            ms = jnp.mean(o * o, axis=-1, keepdims=True)
            y_ref[:, hs] = (o * lax.rsqrt(ms + EPS) * gn_ref[:, hs] * gate[:, hs]).astype(y_ref.dtype)

    blk_long = min(HG_LONG_BLOCK, Lc)
    ltri_l, g_l, gt_l = log_decays(blk_long)
    worst = jnp.max(-gt_l)

    @pl.when(worst < HG_SAFE_DECAY)
    def _factorised():
        kgrow = (k * jnp.exp(-g_l)).astype(BF16)
        in_block = ltri_l > 0.5

        def intra(hs, qg):
            att = lax.dot_general(qg[:, hs], kgrow[:, hs], NT_DIMS, preferred_element_type=F32)
            att = jnp.where(in_block, att, 0.0).astype(BF16)
            return jnp.dot(att, vb[:, hs], preferred_element_type=F32)

        pass_state(blk_long, g_l, gt_l, intra)

    @pl.when(worst >= HG_SAFE_DECAY)
    def _unfactorised():
        _, gb, gl = log_decays(HG_BLOCK)
        kp_ref[HG_BLOCK:HG_BLOCK + Lc, :] = k
        gp_ref[HG_BLOCK:HG_BLOCK + Lc, :] = gb
        vp_ref[HG_BLOCK:HG_BLOCK + Lc, :] = v
        tpos = lax.broadcasted_iota(jnp.int32, (Lc, HG_WIDTH), 0) % HG_BLOCK
        pw = 2 * HG_HD
        oi = lax.broadcasted_iota(jnp.int32, (pw, pw), 0) // HG_HD
        oj = lax.broadcasted_iota(jnp.int32, (pw, pw), 1) // HG_HD
        ones2 = jnp.where(oi == oj, 1.0, 0.0).astype(BF16)
        o_pair = [jnp.zeros((Lc, pw), F32) for _ in range(HG_HEADS // 2)]
        for d in range(HG_BLOCK):
            lo = HG_BLOCK - d
            kd = kp_ref[lo:lo + Lc, :]
            gd = gp_ref[lo:lo + Lc, :]
            vd = vp_ref[lo:lo + Lc, :]
            m = tpos >= d
            dec = jnp.exp(jnp.where(m, gb - gd, 0.0))
            pr = jnp.where(m, q * kd * dec, 0.0).astype(BF16)
            for hp in range(HG_HEADS // 2):
                ps = slice(hp * pw, (hp + 1) * pw)
                o_pair[hp] = o_pair[hp] + jnp.dot(pr[:, ps], ones2, preferred_element_type=F32) * vd[:, ps]
        for hp in range(HG_HEADS // 2):
            oi_ref[:, hp * pw:(hp + 1) * pw] = o_pair[hp]
        pass_state(HG_BLOCK, gb, gl, lambda hs, qg: oi_ref[:, hs])

    @pl.when(c == nc - 1)
    def _fin():
        for h in range(HG_HEADS):
            so_ref[0, h] = st_ref[h].T


def _hgrn(proj, p, lb, *, row0, B, T, Lc, s0=None, ycat=None):
    nc = T // Lc
    rb0 = row0 // Lc
    has_state = s0 is not None

    def rows(col):
        return lambda b, c: (rb0 + b * nc + c, col // HG_WIDTH)

    const = lambda b, c: (0, 0)
    args = [proj, proj, proj, proj, lb, p['hgrn_norm']]
    specs = [pl.BlockSpec((Lc, HG_WIDTH), rows(COL_HQ)), pl.BlockSpec((Lc, HG_WIDTH), rows(COL_HF)),
             pl.BlockSpec((Lc, HG_WIDTH), rows(COL_HI)), pl.BlockSpec((Lc, HG_WIDTH), rows(COL_HG)),
             pl.BlockSpec((1, HG_WIDTH), const), pl.BlockSpec((1, HG_WIDTH), const)]
    if has_state:
        args.append(s0)
        specs.append(pl.BlockSpec((1, HG_HEADS, HG_HD, HG_HD), lambda b, c: (b, 0, 0, 0)))
    yshape, aliases = _mixed_out(ycat, proj.shape[0], args, specs)
    y, so = pl.pallas_call(
        functools.partial(_hgrn_kernel, Lc=Lc, has_state=has_state, aliased=ycat is not None),
        grid=(B, nc),
        in_specs=specs,
        out_specs=[pl.BlockSpec((Lc, HG_WIDTH), rows(YCOL_HG)),
                   pl.BlockSpec((1, HG_HEADS, HG_HD, HG_HD), lambda b, c: (b, 0, 0, 0))],
        out_shape=[yshape, jax.ShapeDtypeStruct((B, HG_HEADS, HG_HD, HG_HD), F32)],
        scratch_shapes=[pltpu.VMEM((HG_BLOCK + Lc, HG_WIDTH), F32)] * 3
                       + [pltpu.VMEM((HG_HEADS, HG_HD, HG_HD), F32), pltpu.VMEM((Lc, HG_WIDTH), F32)],
        input_output_aliases=aliases,
        compiler_params=_cparams(("parallel", "arbitrary")),
        name="hgrn",
    )(*args)
    return y, so


def _swa_kernel(*refs, Tq, prompt, aliased):
    sink_ref, q_ref, k_ref, v_ref, qg_ref, kg_ref = refs[:6]
    rest = refs[6:]
    if not prompt:
        ck_ref, cv_ref = rest[:2]
        rest = rest[2:]
    if aliased:
        rest = rest[1:]
    y_ref, ko_ref, vo_ref, bias_ref = rest[:4]
    if prompt:
        kp_ref, vp_ref = rest[4:]
    c = pl.program_id(1)
    Tk = WINDOW + Tq
    bi = lax.broadcasted_iota(jnp.int32, (LANE, LANE), 0) // SWA_HD
    bj = lax.broadcasted_iota(jnp.int32, (LANE, LANE), 1) // SWA_HD
    bd = jnp.where(bi == bj, 1.0, 0.0).astype(BF16)

    def hnorm(x, gain):
        outs = []
        for j in range(x.shape[1] // LANE):
            xs = x[:, j * LANE:(j + 1) * LANE]
            ms = _split_dot(xs * xs, bd) * (1.0 / SWA_HD)
            outs.append(xs * lax.rsqrt(ms + EPS))
        return (jnp.concatenate(outs, axis=1) if len(outs) > 1 else outs[0]) * gain

    qn = hnorm(q_ref[...], qg_ref[...])
    kn = hnorm(k_ref[...], kg_ref[...])
    vn = v_ref[...]
    if prompt:
        @pl.when(c == 0)
        def _init():
            kp_ref[...] = jnp.zeros((WINDOW, LANE), F32)
            vp_ref[...] = jnp.zeros((WINDOW, LANE), F32)
        kprev = kp_ref[...]
        vprev = vp_ref[...]
    else:
        kprev = ck_ref[0]
        vprev = cv_ref[0]
    kall = jnp.concatenate([kprev, kn], axis=0)
    vall = jnp.concatenate([vprev, vn], axis=0)
    knew = kall[Tq:, :]
    vnew = vall[Tq:, :]
    ko_ref[0] = knew
    vo_ref[0] = vnew
    if prompt:
        kp_ref[...] = knew
        vp_ref[...] = vnew
    ksw = pltpu.roll(kall, SWA_HD, 1)
    vsw = pltpu.roll(vall, SWA_HD, 1)
    lowk = lax.broadcasted_iota(jnp.int32, (Tk, LANE), 1) < SWA_HD
    R = SWA_REP * Tq
    rows_i = lax.broadcasted_iota(jnp.int32, (R, Tk), 0)
    cols_i = lax.broadcasted_iota(jnp.int32, (R, Tk), 1)
    rep_i = rows_i // Tq
    rep_c = lax.broadcasted_iota(jnp.int32, (R, 1), 0) // Tq

    @pl.when(c == 0)
    def _bias():
        dist = jnp.abs(rows_i % Tq + WINDOW - cols_i).astype(F32)
        for g in range(SWA_KV_HEADS):
            slope = jnp.zeros((R, Tk), F32)
            for r in range(SWA_REP):
                slope = jnp.where(rep_i == r, 2.0 ** (-8.0 * (g * SWA_REP + r + 1) / SWA_HEADS), slope)
            bias_ref[g] = slope * dist

    valid = (cols_i + c * Tq) >= WINDOW
    lane = lax.broadcasted_iota(jnp.int32, (Tq, LANE), 1)
    low = lane < SWA_HD
    for g in range(SWA_KV_HEADS):
        parts = []
        for r in range(SWA_REP):
            hd = g * SWA_REP + r
            qp = qn[:, (hd // 2) * LANE:(hd // 2 + 1) * LANE]
            parts.append(jnp.where(low if hd % 2 == 0 else jnp.logical_not(low), qp, 0.0))
        qst = jnp.concatenate(parts, axis=0).astype(BF16)
        kboth = (jnp.where(lowk, kall, ksw) if g == 0 else jnp.where(lowk, ksw, kall)).astype(BF16)
        vboth = (jnp.where(lowk, vall, vsw) if g == 0 else jnp.where(lowk, vsw, vall)).astype(BF16)
        sink = jnp.zeros((R, 1), F32)
        for r in range(SWA_REP):
            sink = jnp.where(rep_c == r, sink_ref[g * SWA_REP + r], sink)
        s = lax.dot_general(qst, kboth, NT_DIMS, preferred_element_type=F32) * (SWA_HD ** -0.5)
        s = s - bias_ref[g]
        if prompt:
            s = jnp.where(valid, s, NEG)
        mx = jnp.maximum(jnp.max(s, axis=-1, keepdims=True), sink)
        pe = jnp.exp(s - mx)
        if prompt:
            pe = jnp.where(valid, pe, 0.0)
        inv = 1.0 / (jnp.sum(pe, axis=-1, keepdims=True) + jnp.exp(sink - mx))
        o = jnp.dot((pe * inv).astype(BF16), vboth, preferred_element_type=F32)
        for jj in range(SWA_REP // 2):
            pair = g * (SWA_REP // 2) + jj
            y_ref[:, pair * LANE:(pair + 1) * LANE] = jnp.where(
                low, o[(2 * jj) * Tq:(2 * jj + 1) * Tq, :], o[(2 * jj + 1) * Tq:(2 * jj + 2) * Tq, :]
            ).astype(y_ref.dtype)


def _swa(proj, p, *, row0, B, T, ck=None, cv=None, ycat=None):
    prompt = ck is None
    Tq = CHUNK if prompt else T
    nc = T // Tq
    rb0 = row0 // Tq

    def rows(col, width):
        return lambda b, c: (rb0 + b * nc + c, col // width)

    const = lambda b, c: (0, 0)
    args = [p['swa_sinks'], proj, proj, proj, p['swa_qnorm'], p['swa_knorm']]
    specs = [pl.BlockSpec(memory_space=pltpu.SMEM),
             pl.BlockSpec((Tq, 1024), rows(COL_SQ, 1024)),
             pl.BlockSpec((Tq, LANE), rows(COL_SK, LANE)),
             pl.BlockSpec((Tq, LANE), rows(COL_SV, LANE)),
             pl.BlockSpec((1, 1024), const), pl.BlockSpec((1, LANE), const)]
    if not prompt:
        args += [ck.reshape(B, WINDOW, LANE), cv.reshape(B, WINDOW, LANE)]
        specs += [pl.BlockSpec((1, WINDOW, LANE), lambda b, c: (b, 0, 0))] * 2
    yshape, aliases = _mixed_out(ycat, proj.shape[0], args, specs)
    y, ko, vo = pl.pallas_call(
        functools.partial(_swa_kernel, Tq=Tq, prompt=prompt, aliased=ycat is not None),
        grid=(B, nc),
        in_specs=specs,
        out_specs=[pl.BlockSpec((Tq, 1024), rows(YCOL_SWA, 1024)),
                   pl.BlockSpec((1, WINDOW, LANE), lambda b, c: (b, 0, 0)),
                   pl.BlockSpec((1, WINDOW, LANE), lambda b, c: (b, 0, 0))],
        out_shape=[yshape,
                   jax.ShapeDtypeStruct((B, WINDOW, LANE), F32),
                   jax.ShapeDtypeStruct((B, WINDOW, LANE), F32)],
        scratch_shapes=[pltpu.VMEM((SWA_KV_HEADS, SWA_REP * Tq, WINDOW + Tq), F32)]
                       + ([pltpu.VMEM((WINDOW, LANE), F32)] * 2 if prompt else []),
        input_output_aliases=aliases,
        compiler_params=_cparams(("parallel", "arbitrary")),
        name="swa",
    )(*args)
    shp = (B, WINDOW, SWA_KV_HEADS, SWA_HD)
    return y, ko.reshape(shp), vo.reshape(shp)


def _mem_kernel(q_ref, k_ref, v_ref, *rest):
    o_ref = rest[-1]
    for h in range(MEM_HEADS):
        hs = slice(h * MEM_HD, (h + 1) * MEM_HD)
        s = lax.dot_general(q_ref[:, hs].astype(BF16), k_ref[:, hs].astype(BF16), NT_DIMS,
                            preferred_element_type=F32) * (MEM_HD ** -0.5)
        mx = jnp.max(s, axis=-1, keepdims=True)
        pe = jnp.exp(s - mx)
        pe = pe / jnp.sum(pe, axis=-1, keepdims=True)
        o_ref[:, hs] = jnp.dot(pe.astype(BF16), v_ref[:, hs].astype(BF16),
                               preferred_element_type=F32).astype(o_ref.dtype)


def _mem_attend(qn, mk, mv, *, row0, B, T, tq, out=None):
    nt = T // tq
    rb0 = row0 // tq
    args = [qn, mk, mv]
    specs = [pl.BlockSpec((tq, MEM_WIDTH), lambda b, t: (rb0 + b * nt + t, 0)),
             pl.BlockSpec((MEM_TOKENS, MEM_WIDTH), lambda b, t: (b, 0)),
             pl.BlockSpec((MEM_TOKENS, MEM_WIDTH), lambda b, t: (b, 0))]
    aliases = {}
    if out is not None:
        args.append(out)
        specs.append(pl.BlockSpec(memory_space=pl.ANY))
        aliases = {3: 0}
    return pl.pallas_call(
        _mem_kernel,
        grid=(B, nt),
        in_specs=specs,
        out_specs=pl.BlockSpec((tq, MEM_WIDTH), lambda b, t: (rb0 + b * nt + t, 0)),
        out_shape=jax.ShapeDtypeStruct((qn.shape[0], MEM_WIDTH), BF16),
        input_output_aliases=aliases,
        compiler_params=_cparams(("parallel", "arbitrary")),
        name="mem_attend",
    )(*args)


def _route_kernel(pq_ref, keys_ref, s1_ref, e1_ref, s2_ref, e2_ref, tau_ref):
    half = PEER_QDIM // 2
    ninf = -jnp.inf
    K = PEER_TOPK

    def top_distinct(s):
        cur, vals = s, []
        for _ in range(K):
            mx = jnp.max(cur, axis=0, keepdims=True)
            vals.append(mx)
            cur = jnp.where(cur == mx, ninf, cur)
        return vals

    def pick_first_max(cur, rows):
        mx = jnp.max(cur, axis=0, keepdims=True)
        first = jnp.min(jnp.where(cur == mx, rows, float(cur.shape[0])), axis=0, keepdims=True)
        return mx, rows == first

    def top_ranked(s):
        rows = lax.broadcasted_iota(jnp.int32, s.shape, 0).astype(F32)
        cur, vals, rank = s, [], jnp.full(s.shape, float(K), F32)
        for it in range(K):
            mx, hit = pick_first_max(cur, rows)
            vals.append(mx)
            rank = jnp.where(hit, float(it), rank)
            cur = jnp.where(hit, ninf, cur)
        return vals, rank

    def count_ge(x, thr):
        return jnp.sum(jnp.where(x >= thr, 1.0, 0.0), axis=0, keepdims=True)

    def all_cands(v1, v2m):
        return jnp.concatenate([v1[r] + v2m for r in range(K)], axis=0)

    def finish(h, s1, s2, v1, v2m, cand, tau):
        m0 = v1[0] + v2m[0:1, :]
        zsum = jnp.sum(jnp.where(cand >= tau, jnp.exp(cand - m0), 0.0), axis=0, keepdims=True)
        s1_ref[h] = jnp.where(s1 >= v1[K - 1], s1, ninf)
        s2_ref[h] = jnp.where(s2 >= v2m[K - 1:K, :], s2, ninf)
        e1_ref[h] = jnp.exp(s1 - v1[0])
        e2_ref[h] = jnp.exp(s2 - v2m[0:1, :]) / zsum
        tau_ref[h:h + 1, :] = tau

    tm = pq_ref.shape[0]

    def scores(h):
        q1 = pq_ref[:, h * PEER_QDIM:h * PEER_QDIM + half]
        q2 = pq_ref[:, h * PEER_QDIM + half:(h + 1) * PEER_QDIM]
        return (lax.dot_general(keys_ref[h, 0], q1, NT_DIMS, precision=HIGHEST, preferred_element_type=F32),
                lax.dot_general(keys_ref[h, 1], q2, NT_DIMS, precision=HIGHEST, preferred_element_type=F32))

    any_tie = []
    for h in range(PEER_HEADS):
        s1, s2 = scores(h)
        v1 = top_distinct(s1)
        v2 = top_distinct(s2)
        v2m = jnp.concatenate(v2, axis=0)
        rowi = lax.broadcasted_iota(jnp.int32, (8, tm), 0)
        blocks = [v1[0] + v2m, v1[1] + v2m[0:8, :]]
        for r in range(2, 8):
            blocks.append(jnp.where(rowi < K // (r + 1), v1[r] + v2m[0:8, :], ninf))
        blocks.append(jnp.concatenate(v1[8:], axis=0) + v2m[0:1, :])
        cand = jnp.concatenate(blocks, axis=0)
        cur = cand
        for _ in range(K):
            tau = jnp.max(cur, axis=0, keepdims=True)
            cur = jnp.where(cur == tau, ninf, cur)
        full = all_cands(v1, v2m)
        tied = jnp.where(count_ge(full, tau) == K,
                         jnp.where(count_ge(s1, v1[K - 1]) == K,
                                   jnp.where(count_ge(s2, v2[K - 1]) == K, 0.0, 1.0), 1.0), 1.0)
        finish(h, s1, s2, v1, v2m, cand, tau)
        any_tie.append(jnp.max(tied))

    for h in range(PEER_HEADS):
        @pl.when(any_tie[h] > 0.0)
        def _with_ties():
            s1, s2 = scores(h)
            w1, rank1 = top_ranked(s1)
            w2, rank2 = top_ranked(s2)
            w2m = jnp.concatenate(w2, axis=0)
            allc = all_cands(w1, w2m)
            rows = lax.broadcasted_iota(jnp.int32, allc.shape, 0).astype(F32)
            cur, picked = allc, jnp.zeros(allc.shape, F32)
            for _ in range(K):
                _, hit = pick_first_max(cur, rows)
                picked = jnp.where(hit, 1.0, picked)
                cur = jnp.where(hit, ninf, cur)
            m0 = w1[0] + w2m[0:1, :]
            zsum = jnp.sum(picked * jnp.exp(allc - m0), axis=0, keepdims=True)
            lmap = jnp.zeros(s1.shape, F32)
            for r in range(K):
                length = jnp.sum(picked[r * K:(r + 1) * K, :], axis=0, keepdims=True)
                lmap = jnp.where(rank1 == float(r), length, lmap)
            s1_ref[h] = lmap
            s2_ref[h] = jnp.where(rank2 < float(K), -rank2, -1000.0)
            e1_ref[h] = jnp.exp(s1 - w1[0])
            e2_ref[h] = jnp.exp(s2 - w2m[0:1, :]) / zsum
            tau_ref[h:h + 1, :] = jnp.ones((1, tm), F32)


def _peer_route(pq, keys, *, tm=256):
    N = pq.shape[0]
    big = jax.ShapeDtypeStruct((PEER_HEADS, N_KEYS, N), F32)
    bspec = pl.BlockSpec((PEER_HEADS, N_KEYS, tm), lambda i: (0, 0, i))
    return pl.pallas_call(
        _route_kernel,
        grid=(N // tm,),
        in_specs=[pl.BlockSpec((tm, PEER_HEADS * PEER_QDIM), lambda i: (i, 0)),
                  pl.BlockSpec((PEER_HEADS, 2, N_KEYS, PEER_QDIM // 2), lambda i: (0, 0, 0, 0))],
        out_specs=[bspec, bspec, bspec, bspec, pl.BlockSpec((PEER_HEADS, tm), lambda i: (0, i))],
        out_shape=[big, big, big, big, jax.ShapeDtypeStruct((PEER_HEADS, N), F32)],
        compiler_params=_cparams(("parallel",)),
        name="peer_route",
    )(pq, keys)


def _dense_kernel(h_ref, g_ref, u_ref, v_ref, s1_ref, e1_ref, s2_ref, e2_ref, tau_ref, o_ref, xn_ref, c_ref,
                  *, te):
    j = pl.program_id(1)
    na = te // N_KEYS

    @pl.when(j == 0)
    def _init():
        x = h_ref[...]
        ms = jnp.mean(x * x, axis=-1, keepdims=True)
        xn_ref[...] = (x * lax.rsqrt(ms + EPS) * g_ref[...]).astype(BF16)
        o_ref[...] = x

    pre = lax.dot_general(xn_ref[...], u_ref[...], NT_DIMS, preferred_element_type=F32)
    act = 0.5 * pre * (1.0 + lax.erf(pre * (2.0 ** -0.5)))
    for al in range(na):
        a = j * na + al
        gt = jnp.zeros((N_KEYS, h_ref.shape[0]), F32)
        for h in range(PEER_HEADS):
            hit = (s1_ref[h, pl.ds(a, 1), :] + s2_ref[h]) >= tau_ref[h:h + 1, :]
            gt = gt + jnp.where(hit, e1_ref[h, pl.ds(a, 1), :] * e2_ref[h], 0.0)
        c_ref[:, al * N_KEYS:(al + 1) * N_KEYS] = (gt.T * act[:, al * N_KEYS:(al + 1) * N_KEYS]).astype(BF16)
    o_ref[...] += jnp.dot(c_ref[...], v_ref[...], preferred_element_type=F32)


def _peer_dense(h, gain, u, v, layer, s1, e1, s2, e2, tau, *, tm=512, te=512):
    N, D = h.shape
    na = te // N_KEYS
    nj = N_EXPERTS // te
    assert N % tm == 0 and N_EXPERTS % te == 0
    rows = pl.BlockSpec((PEER_HEADS, N_KEYS, tm), lambda i, j: (0, 0, i), pipeline_mode=pl.Buffered(1))
    full = pl.BlockSpec((PEER_HEADS, N_KEYS, tm), lambda i, j: (0, 0, i))
    return pl.pallas_call(
        functools.partial(_dense_kernel, te=te),
        grid=(N // tm, nj),
        in_specs=[pl.BlockSpec((tm, D), lambda i, j: (i, 0), pipeline_mode=pl.Buffered(1)),
                  pl.BlockSpec((1, D), lambda i, j: (0, 0)),
                  pl.BlockSpec((None, te, D), lambda i, j: (layer, j, 0)),
                  pl.BlockSpec((None, te, D), lambda i, j: (layer, j, 0)),
                  rows, rows, full, full,
                  pl.BlockSpec((PEER_HEADS, tm), lambda i, j: (0, i))],
        out_specs=pl.BlockSpec((tm, D), lambda i, j: (i, 0), pipeline_mode=pl.Buffered(1)),
        out_shape=jax.ShapeDtypeStruct((N, D), F32),
        scratch_shapes=[pltpu.VMEM((tm, D), BF16), pltpu.VMEM((tm, te), BF16)],
        compiler_params=_cparams(("parallel", "arbitrary")),
        name="peer_dense",
    )(h, gain.reshape(1, D).astype(F32), u, v, s1, e1, s2, e2, tau)


def _row(x, width=None):
    x = x.astype(F32).reshape(1, -1)
    if width is not None and x.shape[1] < width:
        x = jnp.pad(x, ((0, 0), (0, width - x.shape[1])))
    return x


def _pack_kernel(w_ref, o_ref):
    tr = w_ref.shape[0]
    xbc_end = 3 * SSD_INNER
    n_rest = COL_DT - xbc_end
    o_ref[:, 0:xbc_end] = w_ref[:, 0:xbc_end].astype(BF16)
    win = w_ref[:, xbc_end:COL_DT]
    shifted = pltpu.roll(win, n_rest - SSD_HEADS, 1)
    o_ref[:, xbc_end:COL_DT - LANE] = shifted[:, 0:n_rest - LANE].astype(BF16)
    pi = lax.broadcasted_iota(jnp.int32, (SSD_HEADS, LANE), 0)
    pj = lax.broadcasted_iota(jnp.int32, (SSD_HEADS, LANE), 1)
    place = jnp.where(pj == pi + (LANE - SSD_HEADS), 1.0, 0.0).astype(BF16)
    tail = jnp.dot(w_ref[:, COL_DT:COL_DT + SSD_HEADS].astype(BF16), place, preferred_element_type=F32)
    lane = lax.broadcasted_iota(jnp.int32, (tr, LANE), 1)
    o_ref[:, COL_DT - LANE:COL_DT] = jnp.where(
        lane < LANE - SSD_HEADS, shifted[:, n_rest - LANE:n_rest], tail).astype(BF16)
    o_ref[:, COL_DT:COL_DT + LANE] = jnp.where(lane < SSD_HEADS, win[:, 0:LANE], 0.0).astype(BF16)
    o_ref[:, COL_DT + LANE:PROJ_COLS] = jnp.zeros((tr, PROJ_COLS - COL_DT - LANE), BF16)


def _pack_w_in(w_in, *, tr=256):
    depth, d, cols = w_in.shape
    assert cols == COL_DT + SSD_HEADS and d % tr == 0
    return pl.pallas_call(
        _pack_kernel,
        grid=(depth, d // tr),
        in_specs=[pl.BlockSpec((None, tr, cols), lambda l, r: (l, r, 0))],
        out_specs=pl.BlockSpec((None, tr, PROJ_COLS), lambda l, r: (l, r, 0)),
        out_shape=jax.ShapeDtypeStruct((depth, d, PROJ_COLS), BF16),
        compiler_params=_cparams(("parallel", "parallel")),
        name="pack_w_in",
    )(w_in)


def _layer_params(l, conv_w, conv_b, dt_bias, a_log, d_skip, ssd_norm, hgrn_norm, swa_qnorm,
                  swa_knorm, swa_sinks):
    eh = lax.broadcasted_iota(jnp.int32, (LANE, SSD_INNER), 0)
    ec = lax.broadcasted_iota(jnp.int32, (LANE, SSD_INNER), 1)
    return {
        'expand': (ec // SSD_HD == eh).astype(BF16),
        'cwx': conv_w[l][:, :SSD_INNER].astype(F32), 'cwb': conv_w[l][:, SSD_INNER:].astype(F32),
        'cbx': _row(conv_b[l][:SSD_INNER]), 'cbb': _row(conv_b[l][SSD_INNER:]),
        'dtb': _row(dt_bias[l], LANE), 'alog': _row(a_log[l], LANE),
        'dsk': _row(jnp.repeat(d_skip[l], SSD_HD)), 'ssd_norm': _row(ssd_norm[l]),
        'hgrn_norm': _row(jnp.tile(hgrn_norm[l], HG_HEADS)),
        'swa_qnorm': _row(jnp.tile(swa_qnorm[l], SWA_HEADS)),
        'swa_knorm': _row(jnp.tile(swa_knorm[l], SWA_KV_HEADS)),
        'swa_sinks': swa_sinks[l].astype(F32),
    }


def kernel(x_prompt, x_sample, mem_prompt, state_ssm, state_ssd_conv, state_hgrn, cache_swa_k, cache_swa_v,
           cache_mem_k, cache_mem_v, norm_mix, w_in, conv_w, conv_b, dt_bias, a_log, d_skip, ssd_norm,
           hgrn_lb, hgrn_norm, swa_qnorm, swa_knorm, swa_sinks, w_out, norm_mem, norm_memtok, w_mq, w_mk,
           w_mv, mem_qnorm, mem_knorm, w_mo, norm_ffn, w_pq, peer_keys, peer_u, peer_v):
    depth = w_in.shape[0]
    bp, tp, _ = x_prompt.shape
    bs, ts, _ = x_sample.shape
    n_p, n_s = bp * tp, bs * ts
    lbp = jax.nn.softmax(hgrn_lb.astype(F32), axis=0)
    lower = jnp.cumsum(lbp, axis=0) - lbp[0:1]
    h = jnp.concatenate([x_prompt.reshape(n_p, D_MODEL), x_sample.reshape(n_s, D_MODEL)], axis=0)
    mem2d = mem_prompt.reshape(bp * MEM_TOKENS, D_MODEL)
    outs = {k: [] for k in ('ssm_p', 'conv_p', 'hg_p', 'swk_p', 'swv_p', 'mk_p', 'mv_p',
                            'ssm_s', 'conv_s', 'hg_s', 'swk_s', 'swv_s')}
    w_in_b = _pack_w_in(w_in)
    w_out_b, w_mq_b, w_mk_b, w_mv_b, w_mo_b, w_pq_b, u_b, v_b = (
        t.astype(BF16) for t in (w_out, w_mq, w_mk, w_mv, w_mo, w_pq, peer_u, peer_v))
    for l in range(depth):
        p = _layer_params(l, conv_w, conv_b, dt_bias, a_log, d_skip, ssd_norm, hgrn_norm,
                          swa_qnorm, swa_knorm, swa_sinks)
        lb = _row(lower[l])
        proj = _mm(h, w_in_b, layer=l, gain=norm_mix[l], name="proj_in")
        ycat, conv_p, ssm_p = _ssd(proj, p, row0=0, B=bp, T=tp)
        ycat, conv_s, ssm_s = _ssd(proj, p, row0=n_p, B=bs, T=ts, conv0=state_ssd_conv[l],
                                   h0=state_ssm[l], ycat=ycat)
        ycat, hg_p = _hgrn(proj, p, lb, row0=0, B=bp, T=tp, Lc=128, ycat=ycat)
        ycat, hg_s = _hgrn(proj, p, lb, row0=n_p, B=bs, T=ts, Lc=ts, s0=state_hgrn[l], ycat=ycat)
        ycat, swk_p, swv_p = _swa(proj, p, row0=0, B=bp, T=tp, ycat=ycat)
        ycat, swk_s, swv_s = _swa(proj, p, row0=n_p, B=bs, T=ts, ck=cache_swa_k[l], cv=cache_swa_v[l],
                                  ycat=ycat)
        h = _mm(ycat, w_out_b, layer=l, res=h, name="proj_out")
        mk = _mm(mem2d, w_mk_b, layer=l, gain=norm_memtok[l], head_gain=mem_knorm[l], name="mem_k")
        mv = _mm(mem2d, w_mv_b, layer=l, gain=norm_memtok[l], name="mem_v")
        qn = _mm(h, w_mq_b, layer=l, gain=norm_mem[l], head_gain=mem_qnorm[l], name="mem_q")
        om = _mem_attend(qn, mk, mv, row0=0, B=bp, T=tp, tq=512)
        om = _mem_attend(qn, cache_mem_k[l].reshape(bs * MEM_TOKENS, MEM_WIDTH),
                         cache_mem_v[l].reshape(bs * MEM_TOKENS, MEM_WIDTH), row0=n_p, B=bs, T=ts, tq=ts,
                         out=om)
        h = _mm(om, w_mo_b, layer=l, res=h, name="mem_o")
        pq = _mm(h, w_pq_b, layer=l, gain=norm_ffn[l], name="peer_q")
        s1, e1, s2, e2, tau = _peer_route(pq, peer_keys[l].astype(F32))
        h = _peer_dense(h, norm_ffn[l], u_b, v_b, l, s1, e1, s2, e2, tau)
        for name, val in (('ssm_p', ssm_p), ('conv_p', conv_p), ('hg_p', hg_p), ('swk_p', swk_p),
                          ('swv_p', swv_p), ('ssm_s', ssm_s), ('conv_s', conv_s), ('hg_s', hg_s),
                          ('swk_s', swk_s), ('swv_s', swv_s)):
            outs[name].append(val)
        outs['mk_p'].append(mk.reshape(bp, MEM_TOKENS, MEM_HEADS, MEM_HD))
        outs['mv_p'].append(mv.reshape(bp, MEM_TOKENS, MEM_HEADS, MEM_HD))
    st = lambda k: jnp.stack(outs[k])
    return (h[:n_p].reshape(bp, tp, D_MODEL), h[n_p:].reshape(bs, ts, D_MODEL),
            st('ssm_p'), st('conv_p'), st('hg_p'), st('swk_p'), st('swv_p'), st('mk_p'), st('mv_p'),
            st('ssm_s'), st('conv_s'), st('hg_s'), st('swk_s'), st('swv_s'))
```

```python
import functools
import math

import jax
import jax.numpy as jnp
from jax import lax
from jax.experimental import pallas as pl
from jax.experimental.pallas import tpu as pltpu

F32 = jnp.float32
BF16 = jnp.bfloat16
HIGHEST = lax.Precision.HIGHEST

D_MODEL = 4096
PAST_LEN = 2048
CHUNK = 64
SSD_INNER = 2048
SSD_HD = 64
SSD_HEADS = 32
SSD_GROUPS = 8
SSD_STATE = 128
SSD_GW = SSD_INNER // SSD_GROUPS
SSD_REP = SSD_HEADS // SSD_GROUPS
HG_WIDTH = 1024
HG_HD = 128
HG_HEADS = 8
HG_BLOCK = 16
HG_LONG_BLOCK = 64
HG_SAFE_DECAY = 80.0
SWA_HD = 64
SWA_HEADS = 16
SWA_KV_HEADS = 2
SWA_REP = SWA_HEADS // SWA_KV_HEADS
WINDOW = 128
MEM_TOKENS = 256
MEM_HEADS = 4
MEM_HD = 128
MEM_WIDTH = 512
N_KEYS = 128
N_EXPERTS = N_KEYS * N_KEYS
PEER_HEADS = 8
PEER_QDIM = 256
PEER_TOPK = 16
EPS = 1e-6
NEG = -1e30
TINY = 1e-30
LANE = 128
MXU_N = 256

COL_Z, COL_XS, COL_BC = 0, 2048, 4096
COL_HQ, COL_HF, COL_HI, COL_HG = 6144, 7168, 8192, 9216
COL_SQ, COL_SK, COL_SV, COL_DT = 10240, 11264, 11392, 11520
PROJ_COLS = 11776
YCOL_SSD, YCOL_HG, YCOL_SWA = 0, 2048, 3072

VMEM_LIMIT = 60 * 1024 * 1024

NT_DIMS = (((1,), (1,)), ((), ()))
TN_DIMS = (((0,), (0,)), ((), ()))


def _cparams(sem):
    return pltpu.CompilerParams(dimension_semantics=sem, vmem_limit_bytes=VMEM_LIMIT)


def _sigmoid(x):
    return 1.0 / (1.0 + jnp.exp(-x))


def _silu(x):
    return x * _sigmoid(x)


def _split_dot(x, ones_bf16):
    hi = x.astype(BF16)
    lo = (x - hi.astype(F32)).astype(BF16)
    return (jnp.dot(hi, ones_bf16, preferred_element_type=F32)
            + jnp.dot(lo, ones_bf16, preferred_element_type=F32))


def _mm_kernel(*refs, norm, head_norm, residual, w_nk):
    it = iter(refs)
    x_ref = next(it)
    g_ref = next(it) if norm else None
    w_ref = next(it)
    hg_ref = next(it) if head_norm else None
    r_ref = next(it) if residual else None
    o_ref = next(it)
    xn_ref = next(it) if norm else None
    if norm:
        @pl.when(pl.program_id(1) == 0)
        def _():
            x = x_ref[...].astype(F32)
            ms = jnp.mean(x * x, axis=-1, keepdims=True)
            xn_ref[...] = (x * lax.rsqrt(ms + EPS) * g_ref[...]).astype(BF16)
        xb = xn_ref[...]
    else:
        xb = x_ref[...]
    if w_nk:
        acc = lax.dot_general(xb, w_ref[...], NT_DIMS, preferred_element_type=F32)
    else:
        acc = jnp.dot(xb, w_ref[...], preferred_element_type=F32)
    if head_norm:
        parts = []
        for c in range(acc.shape[1] // LANE):
            a = acc[:, c * LANE:(c + 1) * LANE]
            ms = jnp.mean(a * a, axis=-1, keepdims=True)
            parts.append(a * lax.rsqrt(ms + EPS))
        acc = jnp.concatenate(parts, axis=1) * hg_ref[...]
    if residual:
        acc = acc + r_ref[...]
    o_ref[...] = acc.astype(o_ref.dtype)


def _mm(x, w, *, layer=None, w_nk=False, gain=None, head_gain=None, res=None, out_dtype=F32, tm=512,
        tn=512, name="mm"):
    M, K = x.shape
    N = w.shape[-2] if w_nk else w.shape[-1]
    tm, tn = min(tm, M), min(tn, N)
    assert M % tm == 0 and N % tn == 0
    norm, head_norm, residual = gain is not None, head_gain is not None, res is not None
    args, specs = [x], [pl.BlockSpec((tm, K), lambda i, j: (i, 0))]
    if norm:
        args.append(gain.reshape(1, K).astype(F32))
        specs.append(pl.BlockSpec((1, K), lambda i, j: (0, 0)))
    args.append(w)
    if w_nk:
        specs.append(pl.BlockSpec((None, tn, K), lambda i, j: (layer, j, 0)))
    elif layer is None:
        specs.append(pl.BlockSpec((K, tn), lambda i, j: (0, j)))
    else:
        specs.append(pl.BlockSpec((None, K, tn), lambda i, j: (layer, 0, j)))
    if head_norm:
        args.append(jnp.tile(head_gain.astype(F32), N // head_gain.shape[0]).reshape(1, N))
        specs.append(pl.BlockSpec((1, tn), lambda i, j: (0, j)))
    if residual:
        args.append(res)
        specs.append(pl.BlockSpec((tm, tn), lambda i, j: (i, j)))
    return pl.pallas_call(
        functools.partial(_mm_kernel, norm=norm, head_norm=head_norm, residual=residual, w_nk=w_nk),
        grid=(M // tm, N // tn),
        in_specs=specs,
        out_specs=pl.BlockSpec((tm, tn), lambda i, j: (i, j)),
        out_shape=jax.ShapeDtypeStruct((M, N), out_dtype),
        scratch_shapes=[pltpu.VMEM((tm, K), BF16)] if norm else [],
        compiler_params=_cparams(("parallel", "arbitrary")),
        name=name,
    )(*args)


def _ssd_kernel(*refs, L, has_state, aliased):
    (z_ref, xs_ref, bc_ref, dt_ref, cwx_ref, cwb_ref, cbx_ref, cbb_ref, dtb_ref, alog_ref,
     dsk_ref, gn_ref, exp_ref) = refs[:13]
    rest = refs[13:]
    if has_state:
        conv0_ref, h0_ref = rest[:2]
        rest = rest[2:]
    if aliased:
        rest = rest[1:]
    y_ref, convo_ref, ho_ref, xpx_ref, xpb_ref, hT_ref = rest
    c = pl.program_id(1)
    nc = pl.num_programs(1)

    @pl.when(c == 0)
    def _init():
        if has_state:
            xpx_ref[0:8, :] = jnp.zeros((8, SSD_INNER), F32)
            xpb_ref[0:8, :] = jnp.zeros((8, SSD_INNER), F32)
            xpx_ref[5:8, :] = conv0_ref[0, :, 0:SSD_INNER]
            xpb_ref[5:8, :] = conv0_ref[0, :, SSD_INNER:2 * SSD_INNER]
            for g in range(SSD_GROUPS):
                hT_ref[g] = h0_ref[0, g * SSD_GW:(g + 1) * SSD_GW, :].T
        else:
            xpx_ref[0:8, :] = jnp.zeros((8, SSD_INNER), F32)
            xpb_ref[0:8, :] = jnp.zeros((8, SSD_INNER), F32)
            hT_ref[...] = jnp.zeros(hT_ref.shape, F32)

    xpx_ref[8:8 + L, :] = xs_ref[...]
    xpb_ref[8:8 + L, :] = bc_ref[...]

    def conv(xp_ref, w_ref, b_ref):
        acc = b_ref[...] + w_ref[3:4, :] * xp_ref[8:8 + L, :]
        for j in range(3):
            acc = acc + w_ref[j:j + 1, :] * xp_ref[5 + j:5 + j + L, :]
        return acc

    xc = _silu(conv(xpx_ref, cwx_ref, cbx_ref))
    bcc = _silu(conv(xpb_ref, cwb_ref, cbb_ref))
    tail_x = xpx_ref[5 + L:8 + L, :]
    tail_b = xpb_ref[5 + L:8 + L, :]
    xpx_ref[5:8, :] = tail_x
    xpb_ref[5:8, :] = tail_b
    convo_ref[0, :, 0:SSD_INNER] = tail_x
    convo_ref[0, :, SSD_INNER:2 * SSD_INNER] = tail_b

    dtr = dt_ref[...] + dtb_ref[...]
    dt = jnp.maximum(dtr, 0.0) + jnp.log(1.0 + jnp.exp(-jnp.abs(dtr)))
    a = -jnp.exp(alog_ref[...])
    dta = dt * a
    ri = lax.broadcasted_iota(jnp.int32, (L, L), 0)
    ci = lax.broadcasted_iota(jnp.int32, (L, L), 1)
    causal = ri >= ci
    acum = jnp.dot(causal.astype(F32), dta, precision=HIGHEST, preferred_element_type=F32)
    both = jnp.concatenate([acum, dt], axis=0)
    p0 = both.astype(BF16)
    r1 = both - p0.astype(F32)
    p1 = r1.astype(BF16)
    p2 = (r1 - p1.astype(F32)).astype(BF16)
    expand = exp_ref[...]
    bothx = ((jnp.dot(p0, expand, preferred_element_type=F32)
              + jnp.dot(p1, expand, preferred_element_type=F32))
             + jnp.dot(p2, expand, preferred_element_type=F32))
    acx = bothx[:L, :]
    dtx = bothx[L:, :]
    eax = jnp.exp(acx)
    lastx = acx[L - 1:L, :]
    tailw = jnp.exp(lastx - acx) * dtx
    cdx = jnp.exp(lastx)
    acT = acum.T
    dtT = dt.T
    lane_head = lax.broadcasted_iota(jnp.int32, (L, SSD_GW), 1) // SSD_HD
    z = z_ref[...]

    for g in range(SSD_GROUPS):
        gs = slice(g * SSD_GW, (g + 1) * SSD_GW)
        Bg = bcc[:, g * SSD_STATE:(g + 1) * SSD_STATE].astype(BF16)
        Cg = bcc[:, SSD_GROUPS * SSD_STATE + g * SSD_STATE:
                 SSD_GROUPS * SSD_STATE + (g + 1) * SSD_STATE].astype(BF16)
        cb = lax.dot_general(Cg, Bg, NT_DIMS, preferred_element_type=F32)
        ws = []
        for r in range(SSD_REP):
            hd = g * SSD_REP + r
            seg = acum[:, hd:hd + 1] - acT[hd:hd + 1, :]
            dec = jnp.where(causal, jnp.exp(jnp.where(causal, seg, 0.0)), 0.0)
            ws.append(cb * dec * dtT[hd:hd + 1, :])
        wst = jnp.concatenate(ws, axis=0).astype(BF16)
        xg = xc[:, gs]
        full = jnp.dot(wst, xg.astype(BF16), preferred_element_type=F32)
        y_intra = jnp.zeros((L, SSD_GW), F32)
        for r in range(SSD_REP):
            y_intra = y_intra + jnp.where(lane_head == r, full[r * L:(r + 1) * L, :], 0.0)
        hTg = hT_ref[g]
        y_inter = jnp.dot(Cg, hTg.astype(BF16), preferred_element_type=F32) * eax[:, gs]
        xt = (xg * tailw[:, gs]).astype(BF16)
        hT_ref[g] = hTg * cdx[:, gs] + lax.dot_general(Bg, xt, TN_DIMS, preferred_element_type=F32)
        yg = y_intra + y_inter + dsk_ref[:, gs] * xg
        u = yg * _silu(z[:, gs])
        ms = jnp.mean(u * u, axis=-1, keepdims=True)
        y_ref[:, gs] = (u * lax.rsqrt(ms + EPS) * gn_ref[:, gs]).astype(y_ref.dtype)

    @pl.when(c == nc - 1)
    def _fin():
        for g in range(SSD_GROUPS):
            ho_ref[0, g * SSD_GW:(g + 1) * SSD_GW, :] = hT_ref[g].T


def _mixed_out(ycat, n_total, args, specs):
    aliases = {}
    if ycat is not None:
        args.append(ycat)
        specs.append(pl.BlockSpec(memory_space=pl.ANY))
        aliases = {len(args) - 1: 0}
    return jax.ShapeDtypeStruct((n_total, D_MODEL), BF16), aliases


def _ssd(proj, p, *, row0, B, T, conv0=None, h0=None, ycat=None):
    L = math.gcd(T, CHUNK)
    nc = T // L
    rb0 = row0 // L
    has_state = conv0 is not None

    def rows(col):
        return lambda b, c: (rb0 + b * nc + c, col)

    const = lambda b, c: (0, 0)
    args = [proj, proj, proj, proj, p['cwx'], p['cwb'], p['cbx'], p['cbb'], p['dtb'], p['alog'],
            p['dsk'], p['ssd_norm'], p['expand']]
    specs = [pl.BlockSpec((L, SSD_INNER), rows(COL_Z // SSD_INNER)),
             pl.BlockSpec((L, SSD_INNER), rows(COL_XS // SSD_INNER)),
             pl.BlockSpec((L, SSD_INNER), rows(COL_BC // SSD_INNER)),
             pl.BlockSpec((L, LANE), rows(COL_DT // LANE)),
             pl.BlockSpec((4, SSD_INNER), const), pl.BlockSpec((4, SSD_INNER), const),
             pl.BlockSpec((1, SSD_INNER), const), pl.BlockSpec((1, SSD_INNER), const),
             pl.BlockSpec((1, LANE), const), pl.BlockSpec((1, LANE), const),
             pl.BlockSpec((1, SSD_INNER), const), pl.BlockSpec((1, SSD_INNER), const),
             pl.BlockSpec((LANE, SSD_INNER), const)]
    if has_state:
        args += [conv0, h0.reshape(B, SSD_INNER, SSD_STATE)]
        specs += [pl.BlockSpec((1, 3, 2 * SSD_INNER), lambda b, c: (b, 0, 0)),
                  pl.BlockSpec((1, SSD_INNER, SSD_STATE), lambda b, c: (b, 0, 0))]
    yshape, aliases = _mixed_out(ycat, proj.shape[0], args, specs)
    y, convo, ho = pl.pallas_call(
        functools.partial(_ssd_kernel, L=L, has_state=has_state, aliased=ycat is not None),
        grid=(B, nc),
        in_specs=specs,
        out_specs=[pl.BlockSpec((L, SSD_INNER), rows(0)),
                   pl.BlockSpec((1, 3, 2 * SSD_INNER), lambda b, c: (b, 0, 0)),
                   pl.BlockSpec((1, SSD_INNER, SSD_STATE), lambda b, c: (b, 0, 0))],
        out_shape=[yshape,
                   jax.ShapeDtypeStruct((B, 3, 2 * SSD_INNER), F32),
                   jax.ShapeDtypeStruct((B, SSD_INNER, SSD_STATE), F32)],
        scratch_shapes=[pltpu.VMEM((8 + L, SSD_INNER), F32), pltpu.VMEM((8 + L, SSD_INNER), F32),
                        pltpu.VMEM((SSD_GROUPS, SSD_STATE, SSD_GW), F32)],
        input_output_aliases=aliases,
        compiler_params=_cparams(("parallel", "arbitrary")),
        name="ssd",
    )(*args)
    return y, convo, ho.reshape(B, SSD_HEADS, SSD_HD, SSD_STATE)


def _hgrn_kernel(*refs, Lc, has_state, aliased):
    q_ref, f_ref, i_ref, g_ref, lb_ref, gn_ref = refs[:6]
    rest = refs[6:]
    if has_state:
        s0_ref = rest[0]
        rest = rest[1:]
    if aliased:
        rest = rest[1:]
    y_ref, so_ref, kp_ref, gp_ref, vp_ref, st_ref, oi_ref = rest
    c = pl.program_id(1)
    nc = pl.num_programs(1)
    nb = Lc // HG_BLOCK

    @pl.when(c == 0)
    def _init():
        for h in range(HG_HEADS):
            st_ref[h] = s0_ref[0, h].T if has_state else jnp.zeros((HG_HD, HG_HD), F32)
        kp_ref[0:HG_BLOCK, :] = jnp.zeros((HG_BLOCK, HG_WIDTH), F32)
        gp_ref[0:HG_BLOCK, :] = jnp.zeros((HG_BLOCK, HG_WIDTH), F32)
        vp_ref[0:HG_BLOCK, :] = jnp.zeros((HG_BLOCK, HG_WIDTH), F32)

    zf = f_ref[...]
    lb = lb_ref[...]
    f = lb + (1.0 - lb) * _sigmoid(zf)
    logf = jnp.log(jnp.maximum(f, TINY))
    k = (1.0 - lb) * _sigmoid(-zf)
    q = _silu(q_ref[...]) * (HG_HD ** -0.5)
    v = i_ref[...]
    vb = v.astype(BF16)
    gate = _silu(g_ref[...])
    ri = lax.broadcasted_iota(jnp.int32, (Lc, Lc), 0)
    ci = lax.broadcasted_iota(jnp.int32, (Lc, Lc), 1)

    def log_decays(blk):
        same = (ri // blk) == (ci // blk)
        ltri = jnp.where(same, jnp.where(ri >= ci, 1.0, 0.0), 0.0)
        lall = jnp.where(same, 1.0, 0.0)
        return (ltri, jnp.dot(ltri, logf, precision=HIGHEST, preferred_element_type=F32),
                jnp.dot(lall, logf, precision=HIGHEST, preferred_element_type=F32))

    def pass_state(blk, g_cum, g_tot, intra):
        qg = (q * jnp.exp(g_cum)).astype(BF16)
        kdec = (k * jnp.exp(g_tot - g_cum)).astype(BF16)
        cd = jnp.exp(g_tot)
        for h in range(HG_HEADS):
            hs = slice(h * HG_HD, (h + 1) * HG_HD)
            st = st_ref[h]
            outs = []
            for b in range(Lc // blk):
                rs = slice(b * blk, (b + 1) * blk)
                outs.append(lax.dot_general(qg[rs, hs], st.astype(BF16), NT_DIMS, preferred_element_type=F32))
                st = st * cd[b * blk:b * blk + 1, hs] + lax.dot_general(
                    vb[rs, hs], kdec[rs, hs], TN_DIMS, preferred_element_type=F32)
            st_ref[h] = st
            o = intra(hs, qg) + (jnp.concatenate(outs, axis=0) if len(outs) > 1 else outs[0])
            ms = jnp.mean(o * o, axis=-1, keepdims=True)
            y_ref[:, hs] = (o * lax.rsqrt(ms + EPS) * gn_ref[:, hs] * gate[:, hs]).astype(y_ref.dtype)

    blk_long = min(HG_LONG_BLOCK, Lc)
    ltri_l, g_l, gt_l = log_decays(blk_long)
    worst = jnp.max(-gt_l)

    @pl.when(worst < HG_SAFE_DECAY)
    def _factorised():
        kgrow = (k * jnp.exp(-g_l)).astype(BF16)
        in_block = ltri_l > 0.5

        def intra(hs, qg):
            att = lax.dot_general(qg[:, hs], kgrow[:, hs], NT_DIMS, preferred_element_type=F32)
            att = jnp.where(in_block, att, 0.0).astype(BF16)
            return jnp.dot(att, vb[:, hs], preferred_element_type=F32)

        pass_state(blk_long, g_l, gt_l, intra)

    @pl.when(worst >= HG_SAFE_DECAY)
    def _unfactorised():
        _, gb, gl = log_decays(HG_BLOCK)
        kp_ref[HG_BLOCK:HG_BLOCK + Lc, :] = k
        gp_ref[HG_BLOCK:HG_BLOCK + Lc, :] = gb
        vp_ref[HG_BLOCK:HG_BLOCK + Lc, :] = v
        tpos = lax.broadcasted_iota(jnp.int32, (Lc, HG_WIDTH), 0) % HG_BLOCK
        pw = 2 * HG_HD
        oi = lax.broadcasted_iota(jnp.int32, (pw, pw), 0) // HG_HD
        oj = lax.broadcasted_iota(jnp.int32, (pw, pw), 1) // HG_HD
        ones2 = jnp.where(oi == oj, 1.0, 0.0).astype(BF16)
        o_pair = [jnp.zeros((Lc, pw), F32) for _ in range(HG_HEADS // 2)]
        for d in range(HG_BLOCK):
            lo = HG_BLOCK - d
            kd = kp_ref[lo:lo + Lc, :]
            gd = gp_ref[lo:lo + Lc, :]
            vd = vp_ref[lo:lo + Lc, :]
            m = tpos >= d
            dec = jnp.exp(jnp.where(m, gb - gd, 0.0))
            pr = jnp.where(m, q * kd * dec, 0.0).astype(BF16)
            for hp in range(HG_HEADS // 2):
                ps = slice(hp * pw, (hp + 1) * pw)
                o_pair[hp] = o_pair[hp] + jnp.dot(pr[:, ps], ones2, preferred_element_type=F32) * vd[:, ps]
        for hp in range(HG_HEADS // 2):
            oi_ref[:, hp * pw:(hp + 1) * pw] = o_pair[hp]
        pass_state(HG_BLOCK, gb, gl, lambda hs, qg: oi_ref[:, hs])

    @pl.when(c == nc - 1)
    def _fin():
        for h in range(HG_HEADS):
            so_ref[0, h] = st_ref[h].T


def _hgrn(proj, p, lb, *, row0, B, T, Lc, s0=None, ycat=None):
    nc = T // Lc
    rb0 = row0 // Lc
    has_state = s0 is not None

    def rows(col):
        return lambda b, c: (rb0 + b * nc + c, col // HG_WIDTH)

    const = lambda b, c: (0, 0)
    args = [proj, proj, proj, proj, lb, p['hgrn_norm']]
    specs = [pl.BlockSpec((Lc, HG_WIDTH), rows(COL_HQ)), pl.BlockSpec((Lc, HG_WIDTH), rows(COL_HF)),
             pl.BlockSpec((Lc, HG_WIDTH), rows(COL_HI)), pl.BlockSpec((Lc, HG_WIDTH), rows(COL_HG)),
             pl.BlockSpec((1, HG_WIDTH), const), pl.BlockSpec((1, HG_WIDTH), const)]
    if has_state:
        args.append(s0)
        specs.append(pl.BlockSpec((1, HG_HEADS, HG_HD, HG_HD), lambda b, c: (b, 0, 0, 0)))
    yshape, aliases = _mixed_out(ycat, proj.shape[0], args, specs)
    y, so = pl.pallas_call(
        functools.partial(_hgrn_kernel, Lc=Lc, has_state=has_state, aliased=ycat is not None),
        grid=(B, nc),
        in_specs=specs,
        out_specs=[pl.BlockSpec((Lc, HG_WIDTH), rows(YCOL_HG)),
                   pl.BlockSpec((1, HG_HEADS, HG_HD, HG_HD), lambda b, c: (b, 0, 0, 0))],
        out_shape=[yshape, jax.ShapeDtypeStruct((B, HG_HEADS, HG_HD, HG_HD), F32)],
        scratch_shapes=[pltpu.VMEM((HG_BLOCK + Lc, HG_WIDTH), F32)] * 3
                       + [pltpu.VMEM((HG_HEADS, HG_HD, HG_HD), F32), pltpu.VMEM((Lc, HG_WIDTH), F32)],
        input_output_aliases=aliases,
        compiler_params=_cparams(("parallel", "arbitrary")),
        name="hgrn",
    )(*args)
    return y, so


def _swa_kernel(*refs, Tq, prompt, aliased):
    sink_ref, q_ref, k_ref, v_ref, qg_ref, kg_ref = refs[:6]
    rest = refs[6:]
    if not prompt:
        ck_ref, cv_ref = rest[:2]
        rest = rest[2:]
    if aliased:
        rest = rest[1:]
    y_ref, ko_ref, vo_ref, bias_ref = rest[:4]
    if prompt:
        kp_ref, vp_ref = rest[4:]
    c = pl.program_id(1)
    Tk = WINDOW + Tq
    bi = lax.broadcasted_iota(jnp.int32, (LANE, LANE), 0) // SWA_HD
    bj = lax.broadcasted_iota(jnp.int32, (LANE, LANE), 1) // SWA_HD
    bd = jnp.where(bi == bj, 1.0, 0.0).astype(BF16)

    def hnorm(x, gain):
        outs = []
        for j in range(x.shape[1] // LANE):
            xs = x[:, j * LANE:(j + 1) * LANE]
            ms = _split_dot(xs * xs, bd) * (1.0 / SWA_HD)
            outs.append(xs * lax.rsqrt(ms + EPS))
        return (jnp.concatenate(outs, axis=1) if len(outs) > 1 else outs[0]) * gain

    qn = hnorm(q_ref[...], qg_ref[...])
    kn = hnorm(k_ref[...], kg_ref[...])
    vn = v_ref[...]
    if prompt:
        @pl.when(c == 0)
        def _init():
            kp_ref[...] = jnp.zeros((WINDOW, LANE), F32)
            vp_ref[...] = jnp.zeros((WINDOW, LANE), F32)
        kprev = kp_ref[...]
        vprev = vp_ref[...]
    else:
        kprev = ck_ref[0]
        vprev = cv_ref[0]
    kall = jnp.concatenate([kprev, kn], axis=0)
    vall = jnp.concatenate([vprev, vn], axis=0)
    knew = kall[Tq:, :]
    vnew = vall[Tq:, :]
    ko_ref[0] = knew
    vo_ref[0] = vnew
    if prompt:
        kp_ref[...] = knew
        vp_ref[...] = vnew
    ksw = pltpu.roll(kall, SWA_HD, 1)
    vsw = pltpu.roll(vall, SWA_HD, 1)
    lowk = lax.broadcasted_iota(jnp.int32, (Tk, LANE), 1) < SWA_HD
    R = SWA_REP * Tq
    rows_i = lax.broadcasted_iota(jnp.int32, (R, Tk), 0)
    cols_i = lax.broadcasted_iota(jnp.int32, (R, Tk), 1)
    rep_i = rows_i // Tq
    rep_c = lax.broadcasted_iota(jnp.int32, (R, 1), 0) // Tq

    @pl.when(c == 0)
    def _bias():
        dist = jnp.abs(rows_i % Tq + WINDOW - cols_i).astype(F32)
        for g in range(SWA_KV_HEADS):
            slope = jnp.zeros((R, Tk), F32)
            for r in range(SWA_REP):
                slope = jnp.where(rep_i == r, 2.0 ** (-8.0 * (g * SWA_REP + r + 1) / SWA_HEADS), slope)
            bias_ref[g] = slope * dist

    valid = (cols_i + c * Tq) >= WINDOW
    lane = lax.broadcasted_iota(jnp.int32, (Tq, LANE), 1)
    low = lane < SWA_HD
    for g in range(SWA_KV_HEADS):
        parts = []
        for r in range(SWA_REP):
            hd = g * SWA_REP + r
            qp = qn[:, (hd // 2) * LANE:(hd // 2 + 1) * LANE]
            parts.append(jnp.where(low if hd % 2 == 0 else jnp.logical_not(low), qp, 0.0))
        qst = jnp.concatenate(parts, axis=0).astype(BF16)
        kboth = (jnp.where(lowk, kall, ksw) if g == 0 else jnp.where(lowk, ksw, kall)).astype(BF16)
        vboth = (jnp.where(lowk, vall, vsw) if g == 0 else jnp.where(lowk, vsw, vall)).astype(BF16)
        sink = jnp.zeros((R, 1), F32)
        for r in range(SWA_REP):
            sink = jnp.where(rep_c == r, sink_ref[g * SWA_REP + r], sink)
        s = lax.dot_general(qst, kboth, NT_DIMS, preferred_element_type=F32) * (SWA_HD ** -0.5)
        s = s - bias_ref[g]
        if prompt:
            s = jnp.where(valid, s, NEG)
        mx = jnp.maximum(jnp.max(s, axis=-1, keepdims=True), sink)
        pe = jnp.exp(s - mx)
        if prompt:
            pe = jnp.where(valid, pe, 0.0)
        inv = 1.0 / (jnp.sum(pe, axis=-1, keepdims=True) + jnp.exp(sink - mx))
        o = jnp.dot((pe * inv).astype(BF16), vboth, preferred_element_type=F32)
        for jj in range(SWA_REP // 2):
            pair = g * (SWA_REP // 2) + jj
            y_ref[:, pair * LANE:(pair + 1) * LANE] = jnp.where(
                low, o[(2 * jj) * Tq:(2 * jj + 1) * Tq, :], o[(2 * jj + 1) * Tq:(2 * jj + 2) * Tq, :]
            ).astype(y_ref.dtype)


def _swa(proj, p, *, row0, B, T, ck=None, cv=None, ycat=None):
    prompt = ck is None
    Tq = CHUNK if prompt else T
    nc = T // Tq
    rb0 = row0 // Tq

    def rows(col, width):
        return lambda b, c: (rb0 + b * nc + c, col // width)

    const = lambda b, c: (0, 0)
    args = [p['swa_sinks'], proj, proj, proj, p['swa_qnorm'], p['swa_knorm']]
    specs = [pl.BlockSpec(memory_space=pltpu.SMEM),
             pl.BlockSpec((Tq, 1024), rows(COL_SQ, 1024)),
             pl.BlockSpec((Tq, LANE), rows(COL_SK, LANE)),
             pl.BlockSpec((Tq, LANE), rows(COL_SV, LANE)),
             pl.BlockSpec((1, 1024), const), pl.BlockSpec((1, LANE), const)]
    if not prompt:
        args += [ck.reshape(B, WINDOW, LANE), cv.reshape(B, WINDOW, LANE)]
        specs += [pl.BlockSpec((1, WINDOW, LANE), lambda b, c: (b, 0, 0))] * 2
    yshape, aliases = _mixed_out(ycat, proj.shape[0], args, specs)
    y, ko, vo = pl.pallas_call(
        functools.partial(_swa_kernel, Tq=Tq, prompt=prompt, aliased=ycat is not None),
        grid=(B, nc),
        in_specs=specs,
        out_specs=[pl.BlockSpec((Tq, 1024), rows(YCOL_SWA, 1024)),
                   pl.BlockSpec((1, WINDOW, LANE), lambda b, c: (b, 0, 0)),
                   pl.BlockSpec((1, WINDOW, LANE), lambda b, c: (b, 0, 0))],
        out_shape=[yshape,
                   jax.ShapeDtypeStruct((B, WINDOW, LANE), F32),
                   jax.ShapeDtypeStruct((B, WINDOW, LANE), F32)],
        scratch_shapes=[pltpu.VMEM((SWA_KV_HEADS, SWA_REP * Tq, WINDOW + Tq), F32)]
                       + ([pltpu.VMEM((WINDOW, LANE), F32)] * 2 if prompt else []),
        input_output_aliases=aliases,
        compiler_params=_cparams(("parallel", "arbitrary")),
        name="swa",
    )(*args)
    shp = (B, WINDOW, SWA_KV_HEADS, SWA_HD)
    return y, ko.reshape(shp), vo.reshape(shp)


def _mem_kernel(q_ref, k_ref, v_ref, *rest):
    o_ref = rest[-1]
    for h in range(MEM_HEADS):
        hs = slice(h * MEM_HD, (h + 1) * MEM_HD)
        s = lax.dot_general(q_ref[:, hs].astype(BF16), k_ref[:, hs].astype(BF16), NT_DIMS,
                            preferred_element_type=F32) * (MEM_HD ** -0.5)
        mx = jnp.max(s, axis=-1, keepdims=True)
        pe = jnp.exp(s - mx)
        pe = pe / jnp.sum(pe, axis=-1, keepdims=True)
        o_ref[:, hs] = jnp.dot(pe.astype(BF16), v_ref[:, hs].astype(BF16),
                               preferred_element_type=F32).astype(o_ref.dtype)


def _mem_attend(qn, mk, mv, *, row0, B, T, tq, out=None):
    nt = T // tq
    rb0 = row0 // tq
    args = [qn, mk, mv]
    specs = [pl.BlockSpec((tq, MEM_WIDTH), lambda b, t: (rb0 + b * nt + t, 0)),
             pl.BlockSpec((MEM_TOKENS, MEM_WIDTH), lambda b, t: (b, 0)),
             pl.BlockSpec((MEM_TOKENS, MEM_WIDTH), lambda b, t: (b, 0))]
    aliases = {}
    if out is not None:
        args.append(out)
        specs.append(pl.BlockSpec(memory_space=pl.ANY))
        aliases = {3: 0}
    return pl.pallas_call(
        _mem_kernel,
        grid=(B, nt),
        in_specs=specs,
        out_specs=pl.BlockSpec((tq, MEM_WIDTH), lambda b, t: (rb0 + b * nt + t, 0)),
        out_shape=jax.ShapeDtypeStruct((qn.shape[0], MEM_WIDTH), BF16),
        input_output_aliases=aliases,
        compiler_params=_cparams(("parallel", "arbitrary")),
        name="mem_attend",
    )(*args)


def _route_kernel(pq_ref, keys_ref, s1_ref, e1_ref, s2_ref, e2_ref, tau_ref):
    half = PEER_QDIM // 2
    ninf = -jnp.inf
    K = PEER_TOPK

    def top_distinct(s):
        cur, vals = s, []
        for _ in range(K):
            mx = jnp.max(cur, axis=0, keepdims=True)
            vals.append(mx)
            cur = jnp.where(cur == mx, ninf, cur)
        return vals

    def pick_first_max(cur, rows):
        mx = jnp.max(cur, axis=0, keepdims=True)
        first = jnp.min(jnp.where(cur == mx, rows, float(cur.shape[0])), axis=0, keepdims=True)
        return mx, rows == first

    def top_ranked(s):
        rows = lax.broadcasted_iota(jnp.int32, s.shape, 0).astype(F32)
        cur, vals, rank = s, [], jnp.full(s.shape, float(K), F32)
        for it in range(K):
            mx, hit = pick_first_max(cur, rows)
            vals.append(mx)
            rank = jnp.where(hit, float(it), rank)
            cur = jnp.where(hit, ninf, cur)
        return vals, rank

    def count_ge(x, thr):
        return jnp.sum(jnp.where(x >= thr, 1.0, 0.0), axis=0, keepdims=True)

    def all_cands(v1, v2m):
        return jnp.concatenate([v1[r] + v2m for r in range(K)], axis=0)

    def finish(h, s1, s2, v1, v2m, cand, tau):
        m0 = v1[0] + v2m[0:1, :]
        zsum = jnp.sum(jnp.where(cand >= tau, jnp.exp(cand - m0), 0.0), axis=0, keepdims=True)
        s1_ref[h] = jnp.where(s1 >= v1[K - 1], s1, ninf)
        s2_ref[h] = jnp.where(s2 >= v2m[K - 1:K, :], s2, ninf)
        e1_ref[h] = jnp.exp(s1 - v1[0])
        e2_ref[h] = jnp.exp(s2 - v2m[0:1, :]) / zsum
        tau_ref[h:h + 1, :] = tau

    tm = pq_ref.shape[0]

    def scores(h):
        q1 = pq_ref[:, h * PEER_QDIM:h * PEER_QDIM + half]
        q2 = pq_ref[:, h * PEER_QDIM + half:(h + 1) * PEER_QDIM]
        return (lax.dot_general(keys_ref[h, 0], q1, NT_DIMS, precision=HIGHEST, preferred_element_type=F32),
                lax.dot_general(keys_ref[h, 1], q2, NT_DIMS, precision=HIGHEST, preferred_element_type=F32))

    any_tie = []
    for h in range(PEER_HEADS):
        s1, s2 = scores(h)
        v1 = top_distinct(s1)
        v2 = top_distinct(s2)
        v2m = jnp.concatenate(v2, axis=0)
        rowi = lax.broadcasted_iota(jnp.int32, (8, tm), 0)
        blocks = [v1[0] + v2m, v1[1] + v2m[0:8, :]]
        for r in range(2, 8):
            blocks.append(jnp.where(rowi < K // (r + 1), v1[r] + v2m[0:8, :], ninf))
        blocks.append(jnp.concatenate(v1[8:], axis=0) + v2m[0:1, :])
        cand = jnp.concatenate(blocks, axis=0)
        cur = cand
        for _ in range(K):
            tau = jnp.max(cur, axis=0, keepdims=True)
            cur = jnp.where(cur == tau, ninf, cur)
        full = all_cands(v1, v2m)
        tied = jnp.where(count_ge(full, tau) == K,
                         jnp.where(count_ge(s1, v1[K - 1]) == K,
                                   jnp.where(count_ge(s2, v2[K - 1]) == K, 0.0, 1.0), 1.0), 1.0)
        finish(h, s1, s2, v1, v2m, cand, tau)
        any_tie.append(jnp.max(tied))

    for h in range(PEER_HEADS):
        @pl.when(any_tie[h] > 0.0)
        def _with_ties():
            s1, s2 = scores(h)
            w1, rank1 = top_ranked(s1)
            w2, rank2 = top_ranked(s2)
            w2m = jnp.concatenate(w2, axis=0)
            allc = all_cands(w1, w2m)
            rows = lax.broadcasted_iota(jnp.int32, allc.shape, 0).astype(F32)
            cur, picked = allc, jnp.zeros(allc.shape, F32)
            for _ in range(K):
                _, hit = pick_first_max(cur, rows)
                picked = jnp.where(hit, 1.0, picked)
                cur = jnp.where(hit, ninf, cur)
            m0 = w1[0] + w2m[0:1, :]
            zsum = jnp.sum(picked * jnp.exp(allc - m0), axis=0, keepdims=True)
            lmap = jnp.zeros(s1.shape, F32)
            for r in range(K):
                length = jnp.sum(picked[r * K:(r + 1) * K, :], axis=0, keepdims=True)
                lmap = jnp.where(rank1 == float(r), length, lmap)
            s1_ref[h] = lmap
            s2_ref[h] = jnp.where(rank2 < float(K), -rank2, -1000.0)
            e1_ref[h] = jnp.exp(s1 - w1[0])
            e2_ref[h] = jnp.exp(s2 - w2m[0:1, :]) / zsum
            tau_ref[h:h + 1, :] = jnp.ones((1, tm), F32)


def _peer_route(pq, keys, *, tm=256):
    N = pq.shape[0]
    big = jax.ShapeDtypeStruct((PEER_HEADS, N_KEYS, N), F32)
    bspec = pl.BlockSpec((PEER_HEADS, N_KEYS, tm), lambda i: (0, 0, i))
    return pl.pallas_call(
        _route_kernel,
        grid=(N // tm,),
        in_specs=[pl.BlockSpec((tm, PEER_HEADS * PEER_QDIM), lambda i: (i, 0)),
                  pl.BlockSpec((PEER_HEADS, 2, N_KEYS, PEER_QDIM // 2), lambda i: (0, 0, 0, 0))],
        out_specs=[bspec, bspec, bspec, bspec, pl.BlockSpec((PEER_HEADS, tm), lambda i: (0, i))],
        out_shape=[big, big, big, big, jax.ShapeDtypeStruct((PEER_HEADS, N), F32)],
        compiler_params=_cparams(("parallel",)),
        name="peer_route",
    )(pq, keys)


def _dense_kernel(h_ref, g_ref, u_ref, v_ref, s1_ref, e1_ref, s2_ref, e2_ref, tau_ref, o_ref, xn_ref, c_ref,
                  *, te):
    j = pl.program_id(1)
    na = te // N_KEYS

    @pl.when(j == 0)
    def _init():
        x = h_ref[...]
        ms = jnp.mean(x * x, axis=-1, keepdims=True)
        xn_ref[...] = (x * lax.rsqrt(ms + EPS) * g_ref[...]).astype(BF16)
        o_ref[...] = x

    pre = lax.dot_general(xn_ref[...], u_ref[...], NT_DIMS, preferred_element_type=F32)
    act = 0.5 * pre * (1.0 + lax.erf(pre * (2.0 ** -0.5)))
    for al in range(na):
        a = j * na + al
        gt = jnp.zeros((N_KEYS, h_ref.shape[0]), F32)
        for h in range(PEER_HEADS):
            hit = (s1_ref[h, pl.ds(a, 1), :] + s2_ref[h]) >= tau_ref[h:h + 1, :]
            gt = gt + jnp.where(hit, e1_ref[h, pl.ds(a, 1), :] * e2_ref[h], 0.0)
        c_ref[:, al * N_KEYS:(al + 1) * N_KEYS] = (gt.T * act[:, al * N_KEYS:(al + 1) * N_KEYS]).astype(BF16)
    o_ref[...] += jnp.dot(c_ref[...], v_ref[...], preferred_element_type=F32)


def _peer_dense(h, gain, u, v, layer, s1, e1, s2, e2, tau, *, tm=512, te=512):
    N, D = h.shape
    na = te // N_KEYS
    nj = N_EXPERTS // te
    assert N % tm == 0 and N_EXPERTS % te == 0
    rows = pl.BlockSpec((PEER_HEADS, N_KEYS, tm), lambda i, j: (0, 0, i), pipeline_mode=pl.Buffered(1))
    full = pl.BlockSpec((PEER_HEADS, N_KEYS, tm), lambda i, j: (0, 0, i))
    return pl.pallas_call(
        functools.partial(_dense_kernel, te=te),
        grid=(N // tm, nj),
        in_specs=[pl.BlockSpec((tm, D), lambda i, j: (i, 0), pipeline_mode=pl.Buffered(1)),
                  pl.BlockSpec((1, D), lambda i, j: (0, 0)),
                  pl.BlockSpec((None, te, D), lambda i, j: (layer, j, 0)),
                  pl.BlockSpec((None, te, D), lambda i, j: (layer, j, 0)),
                  rows, rows, full, full,
                  pl.BlockSpec((PEER_HEADS, tm), lambda i, j: (0, i))],
        out_specs=pl.BlockSpec((tm, D), lambda i, j: (i, 0), pipeline_mode=pl.Buffered(1)),
        out_shape=jax.ShapeDtypeStruct((N, D), F32),
        scratch_shapes=[pltpu.VMEM((tm, D), BF16), pltpu.VMEM((tm, te), BF16)],
        compiler_params=_cparams(("parallel", "arbitrary")),
        name="peer_dense",
    )(h, gain.reshape(1, D).astype(F32), u, v, s1, e1, s2, e2, tau)


def _row(x, width=None):
    x = x.astype(F32).reshape(1, -1)
    if width is not None and x.shape[1] < width:
        x = jnp.pad(x, ((0, 0), (0, width - x.shape[1])))
    return x


def _cast_kernel(x_ref, o_ref):
    o_ref[...] = x_ref[...].astype(o_ref.dtype)


def _to_bf16(w, *, block_bytes=8 * 1024 * 1024):
    depth, rows, cols = w.shape
    tr = max(8, min(rows, block_bytes // (cols * 4)))
    assert rows % tr == 0
    return pl.pallas_call(
        _cast_kernel,
        grid=(depth, rows // tr),
        in_specs=[pl.BlockSpec((None, tr, cols), lambda l, r: (l, r, 0))],
        out_specs=pl.BlockSpec((None, tr, cols), lambda l, r: (l, r, 0)),
        out_shape=jax.ShapeDtypeStruct(w.shape, BF16),
        compiler_params=_cparams(("parallel", "parallel")),
        name="to_bf16",
    )(w)


PACK_ROWS = 256
PACK_GROUP = SSD_HEADS


def _pack_kernel(w_ref, dt_ref, o_ref):
    r = pl.program_id(1)
    last = pl.num_programs(1) - 1

    @pl.when(r < last)
    def _copy():
        o_ref[...] = w_ref[...].reshape(PACK_ROWS, w_ref.shape[-1]).astype(BF16)

    @pl.when(r == last)
    def _dt_and_pad():
        o_ref[0:PACK_GROUP, :] = dt_ref[...].astype(BF16)
        o_ref[PACK_GROUP:, :] = jnp.zeros((PACK_ROWS - PACK_GROUP, o_ref.shape[-1]), BF16)


def _pack_w_in(w_in):
    depth, d, cols = w_in.shape
    xbc_end = 3 * SSD_INNER
    assert cols == COL_DT + SSD_HEADS and cols % PACK_GROUP == 0 and PROJ_COLS % PACK_ROWS == 0
    gpb = PACK_ROWS // PACK_GROUP
    n_groups = cols // PACK_GROUP
    n_head = xbc_end // PACK_ROWS
    skip = (xbc_end + SSD_HEADS) // PACK_GROUP
    wt = jnp.swapaxes(w_in, 1, 2).reshape(depth, n_groups, PACK_GROUP, d)

    def src_group(l, r):
        g = jnp.where(r < n_head, r * gpb, skip + (r - n_head) * gpb)
        return (l, jnp.minimum(g, n_groups - gpb), 0, 0)

    return pl.pallas_call(
        _pack_kernel,
        grid=(depth, PROJ_COLS // PACK_ROWS),
        in_specs=[pl.BlockSpec((pl.Element(1), pl.Element(gpb), pl.Element(PACK_GROUP), pl.Element(d)),
                               src_group),
                  pl.BlockSpec((None, None, PACK_GROUP, d), lambda l, r: (l, xbc_end // PACK_GROUP, 0, 0))],
        out_specs=pl.BlockSpec((None, PACK_ROWS, d), lambda l, r: (l, r, 0)),
        out_shape=jax.ShapeDtypeStruct((depth, PROJ_COLS, d), BF16),
        compiler_params=_cparams(("parallel", "arbitrary")),
        name="pack_w_in",
    )(wt, wt)


def _layer_params(l, conv_w, conv_b, dt_bias, a_log, d_skip, ssd_norm, hgrn_norm, swa_qnorm,
                  swa_knorm, swa_sinks):
    eh = lax.broadcasted_iota(jnp.int32, (LANE, SSD_INNER), 0)
    ec = lax.broadcasted_iota(jnp.int32, (LANE, SSD_INNER), 1)
    return {
        'expand': (ec // SSD_HD == eh).astype(BF16),
        'cwx': conv_w[l][:, :SSD_INNER].astype(F32), 'cwb': conv_w[l][:, SSD_INNER:].astype(F32),
        'cbx': _row(conv_b[l][:SSD_INNER]), 'cbb': _row(conv_b[l][SSD_INNER:]),
        'dtb': _row(dt_bias[l], LANE), 'alog': _row(a_log[l], LANE),
        'dsk': _row(jnp.repeat(d_skip[l], SSD_HD)), 'ssd_norm': _row(ssd_norm[l]),
        'hgrn_norm': _row(jnp.tile(hgrn_norm[l], HG_HEADS)),
        'swa_qnorm': _row(jnp.tile(swa_qnorm[l], SWA_HEADS)),
        'swa_knorm': _row(jnp.tile(swa_knorm[l], SWA_KV_HEADS)),
        'swa_sinks': swa_sinks[l].astype(F32),
    }


def kernel(x_prompt, x_sample, mem_prompt, state_ssm, state_ssd_conv, state_hgrn, cache_swa_k, cache_swa_v,
           cache_mem_k, cache_mem_v, norm_mix, w_in, conv_w, conv_b, dt_bias, a_log, d_skip, ssd_norm,
           hgrn_lb, hgrn_norm, swa_qnorm, swa_knorm, swa_sinks, w_out, norm_mem, norm_memtok, w_mq, w_mk,
           w_mv, mem_qnorm, mem_knorm, w_mo, norm_ffn, w_pq, peer_keys, peer_u, peer_v):
    depth = w_in.shape[0]
    bp, tp, _ = x_prompt.shape
    bs, ts, _ = x_sample.shape
    n_p, n_s = bp * tp, bs * ts
    lbp = jax.nn.softmax(hgrn_lb.astype(F32), axis=0)
    lower = jnp.cumsum(lbp, axis=0) - lbp[0:1]
    h = jnp.concatenate([x_prompt.reshape(n_p, D_MODEL), x_sample.reshape(n_s, D_MODEL)], axis=0)
    mem2d = mem_prompt.reshape(bp * MEM_TOKENS, D_MODEL)
    outs = {k: [] for k in ('ssm_p', 'conv_p', 'hg_p', 'swk_p', 'swv_p', 'mk_p', 'mv_p',
                            'ssm_s', 'conv_s', 'hg_s', 'swk_s', 'swv_s')}
    w_in_b = _pack_w_in(w_in)
    w_out_b, w_mq_b, w_mk_b, w_mv_b, w_mo_b, w_pq_b, u_b, v_b = (
        _to_bf16(t) for t in (w_out, w_mq, w_mk, w_mv, w_mo, w_pq, peer_u, peer_v))
    for l in range(depth):
        p = _layer_params(l, conv_w, conv_b, dt_bias, a_log, d_skip, ssd_norm, hgrn_norm,
                          swa_qnorm, swa_knorm, swa_sinks)
        lb = _row(lower[l])
        proj = _mm(h, w_in_b, layer=l, w_nk=True, gain=norm_mix[l], name="proj_in")
        ycat, conv_p, ssm_p = _ssd(proj, p, row0=0, B=bp, T=tp)
        ycat, conv_s, ssm_s = _ssd(proj, p, row0=n_p, B=bs, T=ts, conv0=state_ssd_conv[l],
                                   h0=state_ssm[l], ycat=ycat)
        ycat, hg_p = _hgrn(proj, p, lb, row0=0, B=bp, T=tp, Lc=128, ycat=ycat)
        ycat, hg_s = _hgrn(proj, p, lb, row0=n_p, B=bs, T=ts, Lc=ts, s0=state_hgrn[l], ycat=ycat)
        ycat, swk_p, swv_p = _swa(proj, p, row0=0, B=bp, T=tp, ycat=ycat)
        ycat, swk_s, swv_s = _swa(proj, p, row0=n_p, B=bs, T=ts, ck=cache_swa_k[l], cv=cache_swa_v[l],
                                  ycat=ycat)
        h = _mm(ycat, w_out_b, layer=l, res=h, name="proj_out")
        mk = _mm(mem2d, w_mk_b, layer=l, gain=norm_memtok[l], head_gain=mem_knorm[l], name="mem_k")
        mv = _mm(mem2d, w_mv_b, layer=l, gain=norm_memtok[l], name="mem_v")
        qn = _mm(h, w_mq_b, layer=l, gain=norm_mem[l], head_gain=mem_qnorm[l], name="mem_q")
        om = _mem_attend(qn, mk, mv, row0=0, B=bp, T=tp, tq=512)
        om = _mem_attend(qn, cache_mem_k[l].reshape(bs * MEM_TOKENS, MEM_WIDTH),
                         cache_mem_v[l].reshape(bs * MEM_TOKENS, MEM_WIDTH), row0=n_p, B=bs, T=ts, tq=ts,
                         out=om)
        h = _mm(om, w_mo_b, layer=l, res=h, name="mem_o")
        pq = _mm(h, w_pq_b, layer=l, gain=norm_ffn[l], name="peer_q")
        s1, e1, s2, e2, tau = _peer_route(pq, peer_keys[l].astype(F32))
        h = _peer_dense(h, norm_ffn[l], u_b, v_b, l, s1, e1, s2, e2, tau)
        for name, val in (('ssm_p', ssm_p), ('conv_p', conv_p), ('hg_p', hg_p), ('swk_p', swk_p),
                          ('swv_p', swv_p), ('ssm_s', ssm_s), ('conv_s', conv_s), ('hg_s', hg_s),
                          ('swk_s', swk_s), ('swv_s', swv_s)):
            outs[name].append(val)
        outs['mk_p'].append(mk.reshape(bp, MEM_TOKENS, MEM_HEADS, MEM_HD))
        outs['mv_p'].append(mv.reshape(bp, MEM_TOKENS, MEM_HEADS, MEM_HD))
    st = lambda k: jnp.stack(outs[k])
    return (h[:n_p].reshape(bp, tp, D_MODEL), h[n_p:].reshape(bs, ts, D_MODEL),
            st('ssm_p'), st('conv_p'), st('hg_p'), st('swk_p'), st('swv_p'), st('mk_p'), st('mv_p'),
            st('ssm_s'), st('conv_s'), st('hg_s'), st('swk_s'), st('swv_s'))
```

```python
import functools
import math

import jax
import jax.numpy as jnp
from jax import lax
from jax.experimental import pallas as pl
from jax.experimental.pallas import tpu as pltpu

F32 = jnp.float32
BF16 = jnp.bfloat16
HIGHEST = lax.Precision.HIGHEST

D_MODEL = 4096
PAST_LEN = 2048
CHUNK = 64
SSD_INNER = 2048
SSD_HD = 64
SSD_HEADS = 32
SSD_GROUPS = 8
SSD_STATE = 128
SSD_GW = SSD_INNER // SSD_GROUPS
SSD_REP = SSD_HEADS // SSD_GROUPS
HG_WIDTH = 1024
HG_HD = 128
HG_HEADS = 8
HG_BLOCK = 16
HG_LONG_BLOCK = 64
HG_SAFE_DECAY = 80.0
SWA_HD = 64
SWA_HEADS = 16
SWA_KV_HEADS = 2
SWA_REP = SWA_HEADS // SWA_KV_HEADS
WINDOW = 128
MEM_TOKENS = 256
MEM_HEADS = 4
MEM_HD = 128
MEM_WIDTH = 512
N_KEYS = 128
N_EXPERTS = N_KEYS * N_KEYS
PEER_HEADS = 8
PEER_QDIM = 256
PEER_TOPK = 16
EPS = 1e-6
NEG = -1e30
TINY = 1e-30
LANE = 128
MXU_N = 256

COL_Z, COL_XS, COL_BC = 0, 2048, 4096
COL_HQ, COL_HF, COL_HI, COL_HG = 6144, 7168, 8192, 9216
COL_SQ, COL_SK, COL_SV, COL_DT = 10240, 11264, 11392, 11520
PROJ_COLS = 11776
YCOL_SSD, YCOL_HG, YCOL_SWA = 0, 2048, 3072

VMEM_LIMIT = 60 * 1024 * 1024

NT_DIMS = (((1,), (1,)), ((), ()))
TN_DIMS = (((0,), (0,)), ((), ()))


def _cparams(sem):
    return pltpu.CompilerParams(dimension_semantics=sem, vmem_limit_bytes=VMEM_LIMIT)


def _sigmoid(x):
    return 1.0 / (1.0 + jnp.exp(-x))


def _silu(x):
    return x * _sigmoid(x)


def _split_dot(x, ones_bf16):
    hi = x.astype(BF16)
    lo = (x - hi.astype(F32)).astype(BF16)
    return (jnp.dot(hi, ones_bf16, preferred_element_type=F32)
            + jnp.dot(lo, ones_bf16, preferred_element_type=F32))


def _mm_kernel(*refs, norm, head_norm, residual, w_nk):
    it = iter(refs)
    x_ref = next(it)
    g_ref = next(it) if norm else None
    w_ref = next(it)
    hg_ref = next(it) if head_norm else None
    r_ref = next(it) if residual else None
    o_ref = next(it)
    xn_ref = next(it) if norm else None
    if norm:
        @pl.when(pl.program_id(1) == 0)
        def _():
            x = x_ref[...].astype(F32)
            ms = jnp.mean(x * x, axis=-1, keepdims=True)
            xn_ref[...] = (x * lax.rsqrt(ms + EPS) * g_ref[...]).astype(BF16)
        xb = xn_ref[...]
    else:
        xb = x_ref[...]
    if w_nk:
        acc = lax.dot_general(xb, w_ref[...], NT_DIMS, preferred_element_type=F32)
    else:
        acc = jnp.dot(xb, w_ref[...], preferred_element_type=F32)
    if head_norm:
        parts = []
        for c in range(acc.shape[1] // LANE):
            a = acc[:, c * LANE:(c + 1) * LANE]
            ms = jnp.mean(a * a, axis=-1, keepdims=True)
            parts.append(a * lax.rsqrt(ms + EPS))
        acc = jnp.concatenate(parts, axis=1) * hg_ref[...]
    if residual:
        acc = acc + r_ref[...]
    o_ref[...] = acc.astype(o_ref.dtype)


def _mm(x, w, *, layer=None, w_nk=False, gain=None, head_gain=None, res=None, out_dtype=F32, tn=512,
        name="mm"):
    M, K = x.shape
    N = w.shape[-2] if w_nk else w.shape[-1]
    norm, head_norm, residual = gain is not None, head_gain is not None, res is not None
    tm = next(t for t in ((768, 512, 256, 128) if norm else (1024, 512, 256, 128)) if M % t == 0)
    tn = min(tn, N)
    assert N % tn == 0
    args, specs = [x], [pl.BlockSpec((tm, K), lambda i, j: (i, 0))]
    if norm:
        args.append(gain.reshape(1, K).astype(F32))
        specs.append(pl.BlockSpec((1, K), lambda i, j: (0, 0)))
    args.append(w)
    if w_nk:
        specs.append(pl.BlockSpec((None, tn, K), lambda i, j: (layer, j, 0)))
    elif layer is None:
        specs.append(pl.BlockSpec((K, tn), lambda i, j: (0, j)))
    else:
        specs.append(pl.BlockSpec((None, K, tn), lambda i, j: (layer, 0, j)))
    if head_norm:
        args.append(jnp.tile(head_gain.astype(F32), N // head_gain.shape[0]).reshape(1, N))
        specs.append(pl.BlockSpec((1, tn), lambda i, j: (0, j)))
    if residual:
        args.append(res)
        specs.append(pl.BlockSpec((tm, tn), lambda i, j: (i, j)))
    return pl.pallas_call(
        functools.partial(_mm_kernel, norm=norm, head_norm=head_norm, residual=residual, w_nk=w_nk),
        grid=(M // tm, N // tn),
        in_specs=specs,
        out_specs=pl.BlockSpec((tm, tn), lambda i, j: (i, j)),
        out_shape=jax.ShapeDtypeStruct((M, N), out_dtype),
        scratch_shapes=[pltpu.VMEM((tm, K), BF16)] if norm else [],
        compiler_params=_cparams(("parallel", "arbitrary")),
        name=name,
    )(*args)


def _ssd_kernel(*refs, L, has_state, aliased):
    (z_ref, xs_ref, bc_ref, dt_ref, cwx_ref, cwb_ref, cbx_ref, cbb_ref, dtb_ref, alog_ref,
     dsk_ref, gn_ref, exp_ref) = refs[:13]
    rest = refs[13:]
    if has_state:
        conv0_ref, h0_ref = rest[:2]
        rest = rest[2:]
    if aliased:
        rest = rest[1:]
    y_ref, convo_ref, ho_ref, xpx_ref, xpb_ref, hT_ref = rest
    c = pl.program_id(1)
    nc = pl.num_programs(1)

    @pl.when(c == 0)
    def _init():
        if has_state:
            xpx_ref[0:8, :] = jnp.zeros((8, SSD_INNER), F32)
            xpb_ref[0:8, :] = jnp.zeros((8, SSD_INNER), F32)
            xpx_ref[5:8, :] = conv0_ref[0, :, 0:SSD_INNER]
            xpb_ref[5:8, :] = conv0_ref[0, :, SSD_INNER:2 * SSD_INNER]
            for g in range(SSD_GROUPS):
                hT_ref[g] = h0_ref[0, g * SSD_GW:(g + 1) * SSD_GW, :].T
        else:
            xpx_ref[0:8, :] = jnp.zeros((8, SSD_INNER), F32)
            xpb_ref[0:8, :] = jnp.zeros((8, SSD_INNER), F32)
            hT_ref[...] = jnp.zeros(hT_ref.shape, F32)

    xpx_ref[8:8 + L, :] = xs_ref[...]
    xpb_ref[8:8 + L, :] = bc_ref[...]

    def conv(xp_ref, w_ref, b_ref):
        acc = b_ref[...] + w_ref[3:4, :] * xp_ref[8:8 + L, :]
        for j in range(3):
            acc = acc + w_ref[j:j + 1, :] * xp_ref[5 + j:5 + j + L, :]
        return acc

    xc = _silu(conv(xpx_ref, cwx_ref, cbx_ref))
    bcc = _silu(conv(xpb_ref, cwb_ref, cbb_ref))
    tail_x = xpx_ref[5 + L:8 + L, :]
    tail_b = xpb_ref[5 + L:8 + L, :]
    xpx_ref[5:8, :] = tail_x
    xpb_ref[5:8, :] = tail_b
    convo_ref[0, :, 0:SSD_INNER] = tail_x
    convo_ref[0, :, SSD_INNER:2 * SSD_INNER] = tail_b

    dtr = dt_ref[...] + dtb_ref[...]
    dt = jnp.maximum(dtr, 0.0) + jnp.log(1.0 + jnp.exp(-jnp.abs(dtr)))
    a = -jnp.exp(alog_ref[...])
    dta = dt * a
    ri = lax.broadcasted_iota(jnp.int32, (L, L), 0)
    ci = lax.broadcasted_iota(jnp.int32, (L, L), 1)
    causal = ri >= ci
    acum = jnp.dot(causal.astype(F32), dta, precision=HIGHEST, preferred_element_type=F32)
    both = jnp.concatenate([acum, dt], axis=0)
    p0 = both.astype(BF16)
    r1 = both - p0.astype(F32)
    p1 = r1.astype(BF16)
    p2 = (r1 - p1.astype(F32)).astype(BF16)
    expand = exp_ref[...]
    bothx = ((jnp.dot(p0, expand, preferred_element_type=F32)
              + jnp.dot(p1, expand, preferred_element_type=F32))
             + jnp.dot(p2, expand, preferred_element_type=F32))
    acx = bothx[:L, :]
    dtx = bothx[L:, :]
    eax = jnp.exp(acx)
    lastx = acx[L - 1:L, :]
    tailw = jnp.exp(lastx - acx) * dtx
    cdx = jnp.exp(lastx)
    acT = acum.T
    dtT = dt.T
    lane_head = lax.broadcasted_iota(jnp.int32, (L, SSD_GW), 1) // SSD_HD
    z = z_ref[...]

    for g in range(SSD_GROUPS):
        gs = slice(g * SSD_GW, (g + 1) * SSD_GW)
        Bg = bcc[:, g * SSD_STATE:(g + 1) * SSD_STATE].astype(BF16)
        Cg = bcc[:, SSD_GROUPS * SSD_STATE + g * SSD_STATE:
                 SSD_GROUPS * SSD_STATE + (g + 1) * SSD_STATE].astype(BF16)
        cb = lax.dot_general(Cg, Bg, NT_DIMS, preferred_element_type=F32)
        ws = []
        for r in range(SSD_REP):
            hd = g * SSD_REP + r
            seg = acum[:, hd:hd + 1] - acT[hd:hd + 1, :]
            dec = jnp.where(causal, jnp.exp(jnp.where(causal, seg, 0.0)), 0.0)
            ws.append(cb * dec * dtT[hd:hd + 1, :])
        wst = jnp.concatenate(ws, axis=0).astype(BF16)
        xg = xc[:, gs]
        full = jnp.dot(wst, xg.astype(BF16), preferred_element_type=F32)
        y_intra = jnp.zeros((L, SSD_GW), F32)
        for r in range(SSD_REP):
            y_intra = y_intra + jnp.where(lane_head == r, full[r * L:(r + 1) * L, :], 0.0)
        hTg = hT_ref[g]
        y_inter = jnp.dot(Cg, hTg.astype(BF16), preferred_element_type=F32) * eax[:, gs]
        xt = (xg * tailw[:, gs]).astype(BF16)
        hT_ref[g] = hTg * cdx[:, gs] + lax.dot_general(Bg, xt, TN_DIMS, preferred_element_type=F32)
        yg = y_intra + y_inter + dsk_ref[:, gs] * xg
        u = yg * _silu(z[:, gs])
        ms = jnp.mean(u * u, axis=-1, keepdims=True)
        y_ref[:, gs] = (u * lax.rsqrt(ms + EPS) * gn_ref[:, gs]).astype(y_ref.dtype)

    @pl.when(c == nc - 1)
    def _fin():
        for g in range(SSD_GROUPS):
            ho_ref[0, g * SSD_GW:(g + 1) * SSD_GW, :] = hT_ref[g].T


def _mixed_out(ycat, n_total, args, specs):
    aliases = {}
    if ycat is not None:
        args.append(ycat)
        specs.append(pl.BlockSpec(memory_space=pl.ANY))
        aliases = {len(args) - 1: 0}
    return jax.ShapeDtypeStruct((n_total, D_MODEL), BF16), aliases


def _ssd(proj, p, *, row0, B, T, conv0=None, h0=None, ycat=None):
    L = math.gcd(T, CHUNK)
    nc = T // L
    rb0 = row0 // L
    has_state = conv0 is not None

    def rows(col):
        return lambda b, c: (rb0 + b * nc + c, col)

    const = lambda b, c: (0, 0)
    args = [proj, proj, proj, proj, p['cwx'], p['cwb'], p['cbx'], p['cbb'], p['dtb'], p['alog'],
            p['dsk'], p['ssd_norm'], p['expand']]
    specs = [pl.BlockSpec((L, SSD_INNER), rows(COL_Z // SSD_INNER)),
             pl.BlockSpec((L, SSD_INNER), rows(COL_XS // SSD_INNER)),
             pl.BlockSpec((L, SSD_INNER), rows(COL_BC // SSD_INNER)),
             pl.BlockSpec((L, LANE), rows(COL_DT // LANE)),
             pl.BlockSpec((4, SSD_INNER), const), pl.BlockSpec((4, SSD_INNER), const),
             pl.BlockSpec((1, SSD_INNER), const), pl.BlockSpec((1, SSD_INNER), const),
             pl.BlockSpec((1, LANE), const), pl.BlockSpec((1, LANE), const),
             pl.BlockSpec((1, SSD_INNER), const), pl.BlockSpec((1, SSD_INNER), const),
             pl.BlockSpec((LANE, SSD_INNER), const)]
    if has_state:
        args += [conv0, h0.reshape(B, SSD_INNER, SSD_STATE)]
        specs += [pl.BlockSpec((1, 3, 2 * SSD_INNER), lambda b, c: (b, 0, 0)),
                  pl.BlockSpec((1, SSD_INNER, SSD_STATE), lambda b, c: (b, 0, 0))]
    yshape, aliases = _mixed_out(ycat, proj.shape[0], args, specs)
    y, convo, ho = pl.pallas_call(
        functools.partial(_ssd_kernel, L=L, has_state=has_state, aliased=ycat is not None),
        grid=(B, nc),
        in_specs=specs,
        out_specs=[pl.BlockSpec((L, SSD_INNER), rows(0)),
                   pl.BlockSpec((1, 3, 2 * SSD_INNER), lambda b, c: (b, 0, 0)),
                   pl.BlockSpec((1, SSD_INNER, SSD_STATE), lambda b, c: (b, 0, 0))],
        out_shape=[yshape,
                   jax.ShapeDtypeStruct((B, 3, 2 * SSD_INNER), F32),
                   jax.ShapeDtypeStruct((B, SSD_INNER, SSD_STATE), F32)],
        scratch_shapes=[pltpu.VMEM((8 + L, SSD_INNER), F32), pltpu.VMEM((8 + L, SSD_INNER), F32),
                        pltpu.VMEM((SSD_GROUPS, SSD_STATE, SSD_GW), F32)],
        input_output_aliases=aliases,
        compiler_params=_cparams(("parallel", "arbitrary")),
        name="ssd",
    )(*args)
    return y, convo, ho.reshape(B, SSD_HEADS, SSD_HD, SSD_STATE)


def _hgrn_kernel(*refs, Lc, has_state, aliased):
    q_ref, f_ref, i_ref, g_ref, lb_ref, gn_ref = refs[:6]
    rest = refs[6:]
    if has_state:
        s0_ref = rest[0]
        rest = rest[1:]
    if aliased:
        rest = rest[1:]
    y_ref, so_ref, kp_ref, gp_ref, vp_ref, st_ref, oi_ref = rest
    c = pl.program_id(1)
    nc = pl.num_programs(1)
    nb = Lc // HG_BLOCK

    @pl.when(c == 0)
    def _init():
        for h in range(HG_HEADS):
            st_ref[h] = s0_ref[0, h].T if has_state else jnp.zeros((HG_HD, HG_HD), F32)
        kp_ref[0:HG_BLOCK, :] = jnp.zeros((HG_BLOCK, HG_WIDTH), F32)
        gp_ref[0:HG_BLOCK, :] = jnp.zeros((HG_BLOCK, HG_WIDTH), F32)
        vp_ref[0:HG_BLOCK, :] = jnp.zeros((HG_BLOCK, HG_WIDTH), F32)

    zf = f_ref[...]
    lb = lb_ref[...]
    f = lb + (1.0 - lb) * _sigmoid(zf)
    logf = jnp.log(jnp.maximum(f, TINY))
    k = (1.0 - lb) * _sigmoid(-zf)
    q = _silu(q_ref[...]) * (HG_HD ** -0.5)
    v = i_ref[...]
    vb = v.astype(BF16)
    gate = _silu(g_ref[...])
    ri = lax.broadcasted_iota(jnp.int32, (Lc, Lc), 0)
    ci = lax.broadcasted_iota(jnp.int32, (Lc, Lc), 1)

    def log_decays(blk):
        same = (ri // blk) == (ci // blk)
        ltri = jnp.where(same, jnp.where(ri >= ci, 1.0, 0.0), 0.0)
        lall = jnp.where(same, 1.0, 0.0)
        return (ltri, jnp.dot(ltri, logf, precision=HIGHEST, preferred_element_type=F32),
                jnp.dot(lall, logf, precision=HIGHEST, preferred_element_type=F32))

    def pass_state(blk, g_cum, g_tot, intra):
        qg = (q * jnp.exp(g_cum)).astype(BF16)
        kdec = (k * jnp.exp(g_tot - g_cum)).astype(BF16)
        cd = jnp.exp(g_tot)
        for h in range(HG_HEADS):
            hs = slice(h * HG_HD, (h + 1) * HG_HD)
            st = st_ref[h]
            outs = []
            for b in range(Lc // blk):
                rs = slice(b * blk, (b + 1) * blk)
                outs.append(lax.dot_general(qg[rs, hs], st.astype(BF16), NT_DIMS, preferred_element_type=F32))
                st = st * cd[b * blk:b * blk + 1, hs] + lax.dot_general(
                    vb[rs, hs], kdec[rs, hs], TN_DIMS, preferred_element_type=F32)
            st_ref[h] = st
            o = intra(hs, qg) + (jnp.concatenate(outs, axis=0) if len(outs) > 1 else outs[0])
            ms = jnp.mean(o * o, axis=-1, keepdims=True)
            y_ref[:, hs] = (o * lax.rsqrt(ms + EPS) * gn_ref[:, hs] * gate[:, hs]).astype(y_ref.dtype)

    blk_long = min(HG_LONG_BLOCK, Lc)
    ltri_l, g_l, gt_l = log_decays(blk_long)
    worst = jnp.max(-gt_l)

    @pl.when(worst < HG_SAFE_DECAY)
    def _factorised():
        kgrow = (k * jnp.exp(-g_l)).astype(BF16)
        in_block = ltri_l > 0.5

        def intra(hs, qg):
            att = lax.dot_general(qg[:, hs], kgrow[:, hs], NT_DIMS, preferred_element_type=F32)
            att = jnp.where(in_block, att, 0.0).astype(BF16)
            return jnp.dot(att, vb[:, hs], preferred_element_type=F32)

        pass_state(blk_long, g_l, gt_l, intra)

    @pl.when(worst >= HG_SAFE_DECAY)
    def _unfactorised():
        _, gb, gl = log_decays(HG_BLOCK)
        kp_ref[HG_BLOCK:HG_BLOCK + Lc, :] = k
        gp_ref[HG_BLOCK:HG_BLOCK + Lc, :] = gb
        vp_ref[HG_BLOCK:HG_BLOCK + Lc, :] = v
        tpos = lax.broadcasted_iota(jnp.int32, (Lc, HG_WIDTH), 0) % HG_BLOCK
        pw = 2 * HG_HD
        oi = lax.broadcasted_iota(jnp.int32, (pw, pw), 0) // HG_HD
        oj = lax.broadcasted_iota(jnp.int32, (pw, pw), 1) // HG_HD
        ones2 = jnp.where(oi == oj, 1.0, 0.0).astype(BF16)
        o_pair = [jnp.zeros((Lc, pw), F32) for _ in range(HG_HEADS // 2)]
        for d in range(HG_BLOCK):
            lo = HG_BLOCK - d
            kd = kp_ref[lo:lo + Lc, :]
            gd = gp_ref[lo:lo + Lc, :]
            vd = vp_ref[lo:lo + Lc, :]
            m = tpos >= d
            dec = jnp.exp(jnp.where(m, gb - gd, 0.0))
            pr = jnp.where(m, q * kd * dec, 0.0).astype(BF16)
            for hp in range(HG_HEADS // 2):
                ps = slice(hp * pw, (hp + 1) * pw)
                o_pair[hp] = o_pair[hp] + jnp.dot(pr[:, ps], ones2, preferred_element_type=F32) * vd[:, ps]
        for hp in range(HG_HEADS // 2):
            oi_ref[:, hp * pw:(hp + 1) * pw] = o_pair[hp]
        pass_state(HG_BLOCK, gb, gl, lambda hs, qg: oi_ref[:, hs])

    @pl.when(c == nc - 1)
    def _fin():
        for h in range(HG_HEADS):
            so_ref[0, h] = st_ref[h].T


def _hgrn(proj, p, lb, *, row0, B, T, Lc, s0=None, ycat=None):
    nc = T // Lc
    rb0 = row0 // Lc
    has_state = s0 is not None

    def rows(col):
        return lambda b, c: (rb0 + b * nc + c, col // HG_WIDTH)

    const = lambda b, c: (0, 0)
    args = [proj, proj, proj, proj, lb, p['hgrn_norm']]
    specs = [pl.BlockSpec((Lc, HG_WIDTH), rows(COL_HQ)), pl.BlockSpec((Lc, HG_WIDTH), rows(COL_HF)),
             pl.BlockSpec((Lc, HG_WIDTH), rows(COL_HI)), pl.BlockSpec((Lc, HG_WIDTH), rows(COL_HG)),
             pl.BlockSpec((1, HG_WIDTH), const), pl.BlockSpec((1, HG_WIDTH), const)]
    if has_state:
        args.append(s0)
        specs.append(pl.BlockSpec((1, HG_HEADS, HG_HD, HG_HD), lambda b, c: (b, 0, 0, 0)))
    yshape, aliases = _mixed_out(ycat, proj.shape[0], args, specs)
    y, so = pl.pallas_call(
        functools.partial(_hgrn_kernel, Lc=Lc, has_state=has_state, aliased=ycat is not None),
        grid=(B, nc),
        in_specs=specs,
        out_specs=[pl.BlockSpec((Lc, HG_WIDTH), rows(YCOL_HG)),
                   pl.BlockSpec((1, HG_HEADS, HG_HD, HG_HD), lambda b, c: (b, 0, 0, 0))],
        out_shape=[yshape, jax.ShapeDtypeStruct((B, HG_HEADS, HG_HD, HG_HD), F32)],
        scratch_shapes=[pltpu.VMEM((HG_BLOCK + Lc, HG_WIDTH), F32)] * 3
                       + [pltpu.VMEM((HG_HEADS, HG_HD, HG_HD), F32), pltpu.VMEM((Lc, HG_WIDTH), F32)],
        input_output_aliases=aliases,
        compiler_params=_cparams(("parallel", "arbitrary")),
        name="hgrn",
    )(*args)
    return y, so


def _swa_kernel(*refs, Tq, prompt, aliased):
    sink_ref, q_ref, k_ref, v_ref, qg_ref, kg_ref = refs[:6]
    rest = refs[6:]
    if not prompt:
        ck_ref, cv_ref = rest[:2]
        rest = rest[2:]
    if aliased:
        rest = rest[1:]
    y_ref, ko_ref, vo_ref, bias_ref = rest[:4]
    if prompt:
        kp_ref, vp_ref = rest[4:]
    c = pl.program_id(1)
    Tk = WINDOW + Tq
    bi = lax.broadcasted_iota(jnp.int32, (LANE, LANE), 0) // SWA_HD
    bj = lax.broadcasted_iota(jnp.int32, (LANE, LANE), 1) // SWA_HD
    bd = jnp.where(bi == bj, 1.0, 0.0).astype(BF16)

    def hnorm(x, gain):
        outs = []
        for j in range(x.shape[1] // LANE):
            xs = x[:, j * LANE:(j + 1) * LANE]
            ms = _split_dot(xs * xs, bd) * (1.0 / SWA_HD)
            outs.append(xs * lax.rsqrt(ms + EPS))
        return (jnp.concatenate(outs, axis=1) if len(outs) > 1 else outs[0]) * gain

    qn = hnorm(q_ref[...], qg_ref[...])
    kn = hnorm(k_ref[...], kg_ref[...])
    vn = v_ref[...]
    if prompt:
        @pl.when(c == 0)
        def _init():
            kp_ref[...] = jnp.zeros((WINDOW, LANE), F32)
            vp_ref[...] = jnp.zeros((WINDOW, LANE), F32)
        kprev = kp_ref[...]
        vprev = vp_ref[...]
    else:
        kprev = ck_ref[0]
        vprev = cv_ref[0]
    kall = jnp.concatenate([kprev, kn], axis=0)
    vall = jnp.concatenate([vprev, vn], axis=0)
    knew = kall[Tq:, :]
    vnew = vall[Tq:, :]
    ko_ref[0] = knew
    vo_ref[0] = vnew
    if prompt:
        kp_ref[...] = knew
        vp_ref[...] = vnew
    ksw = pltpu.roll(kall, SWA_HD, 1)
    vsw = pltpu.roll(vall, SWA_HD, 1)
    lowk = lax.broadcasted_iota(jnp.int32, (Tk, LANE), 1) < SWA_HD
    R = SWA_REP * Tq
    rows_i = lax.broadcasted_iota(jnp.int32, (R, Tk), 0)
    cols_i = lax.broadcasted_iota(jnp.int32, (R, Tk), 1)
    rep_i = rows_i // Tq
    rep_c = lax.broadcasted_iota(jnp.int32, (R, 1), 0) // Tq

    @pl.when(c == 0)
    def _bias():
        dist = jnp.abs(rows_i % Tq + WINDOW - cols_i).astype(F32)
        for g in range(SWA_KV_HEADS):
            slope = jnp.zeros((R, Tk), F32)
            for r in range(SWA_REP):
                slope = jnp.where(rep_i == r, 2.0 ** (-8.0 * (g * SWA_REP + r + 1) / SWA_HEADS), slope)
            bias_ref[g] = slope * dist

    valid = (cols_i + c * Tq) >= WINDOW
    lane = lax.broadcasted_iota(jnp.int32, (Tq, LANE), 1)
    low = lane < SWA_HD
    for g in range(SWA_KV_HEADS):
        parts = []
        for r in range(SWA_REP):
            hd = g * SWA_REP + r
            qp = qn[:, (hd // 2) * LANE:(hd // 2 + 1) * LANE]
            parts.append(jnp.where(low if hd % 2 == 0 else jnp.logical_not(low), qp, 0.0))
        qst = jnp.concatenate(parts, axis=0).astype(BF16)
        kboth = (jnp.where(lowk, kall, ksw) if g == 0 else jnp.where(lowk, ksw, kall)).astype(BF16)
        vboth = (jnp.where(lowk, vall, vsw) if g == 0 else jnp.where(lowk, vsw, vall)).astype(BF16)
        sink = jnp.zeros((R, 1), F32)
        for r in range(SWA_REP):
            sink = jnp.where(rep_c == r, sink_ref[g * SWA_REP + r], sink)
        s = lax.dot_general(qst, kboth, NT_DIMS, preferred_element_type=F32) * (SWA_HD ** -0.5)
        s = s - bias_ref[g]
        if prompt:
            s = jnp.where(valid, s, NEG)
        mx = jnp.maximum(jnp.max(s, axis=-1, keepdims=True), sink)
        pe = jnp.exp(s - mx)
        if prompt:
            pe = jnp.where(valid, pe, 0.0)
        inv = 1.0 / (jnp.sum(pe, axis=-1, keepdims=True) + jnp.exp(sink - mx))
        o = jnp.dot((pe * inv).astype(BF16), vboth, preferred_element_type=F32)
        for jj in range(SWA_REP // 2):
            pair = g * (SWA_REP // 2) + jj
            y_ref[:, pair * LANE:(pair + 1) * LANE] = jnp.where(
                low, o[(2 * jj) * Tq:(2 * jj + 1) * Tq, :], o[(2 * jj + 1) * Tq:(2 * jj + 2) * Tq, :]
            ).astype(y_ref.dtype)


def _swa(proj, p, *, row0, B, T, ck=None, cv=None, ycat=None):
    prompt = ck is None
    Tq = CHUNK if prompt else T
    nc = T // Tq
    rb0 = row0 // Tq

    def rows(col, width):
        return lambda b, c: (rb0 + b * nc + c, col // width)

    const = lambda b, c: (0, 0)
    args = [p['swa_sinks'], proj, proj, proj, p['swa_qnorm'], p['swa_knorm']]
    specs = [pl.BlockSpec(memory_space=pltpu.SMEM),
             pl.BlockSpec((Tq, 1024), rows(COL_SQ, 1024)),
             pl.BlockSpec((Tq, LANE), rows(COL_SK, LANE)),
             pl.BlockSpec((Tq, LANE), rows(COL_SV, LANE)),
             pl.BlockSpec((1, 1024), const), pl.BlockSpec((1, LANE), const)]
    if not prompt:
        args += [ck.reshape(B, WINDOW, LANE), cv.reshape(B, WINDOW, LANE)]
        specs += [pl.BlockSpec((1, WINDOW, LANE), lambda b, c: (b, 0, 0))] * 2
    yshape, aliases = _mixed_out(ycat, proj.shape[0], args, specs)
    y, ko, vo = pl.pallas_call(
        functools.partial(_swa_kernel, Tq=Tq, prompt=prompt, aliased=ycat is not None),
        grid=(B, nc),
        in_specs=specs,
        out_specs=[pl.BlockSpec((Tq, 1024), rows(YCOL_SWA, 1024)),
                   pl.BlockSpec((1, WINDOW, LANE), lambda b, c: (b, 0, 0)),
                   pl.BlockSpec((1, WINDOW, LANE), lambda b, c: (b, 0, 0))],
        out_shape=[yshape,
                   jax.ShapeDtypeStruct((B, WINDOW, LANE), F32),
                   jax.ShapeDtypeStruct((B, WINDOW, LANE), F32)],
        scratch_shapes=[pltpu.VMEM((SWA_KV_HEADS, SWA_REP * Tq, WINDOW + Tq), F32)]
                       + ([pltpu.VMEM((WINDOW, LANE), F32)] * 2 if prompt else []),
        input_output_aliases=aliases,
        compiler_params=_cparams(("parallel", "arbitrary")),
        name="swa",
    )(*args)
    shp = (B, WINDOW, SWA_KV_HEADS, SWA_HD)
    return y, ko.reshape(shp), vo.reshape(shp)


def _mem_kernel(q_ref, k_ref, v_ref, *rest):
    o_ref = rest[-1]
    for h in range(MEM_HEADS):
        hs = slice(h * MEM_HD, (h + 1) * MEM_HD)
        s = lax.dot_general(q_ref[:, hs].astype(BF16), k_ref[:, hs].astype(BF16), NT_DIMS,
                            preferred_element_type=F32) * (MEM_HD ** -0.5)
        mx = jnp.max(s, axis=-1, keepdims=True)
        pe = jnp.exp(s - mx)
        pe = pe / jnp.sum(pe, axis=-1, keepdims=True)
        o_ref[:, hs] = jnp.dot(pe.astype(BF16), v_ref[:, hs].astype(BF16),
                               preferred_element_type=F32).astype(o_ref.dtype)


def _mem_attend(qn, mk, mv, *, row0, B, T, tq, out=None):
    nt = T // tq
    rb0 = row0 // tq
    args = [qn, mk, mv]
    specs = [pl.BlockSpec((tq, MEM_WIDTH), lambda b, t: (rb0 + b * nt + t, 0)),
             pl.BlockSpec((MEM_TOKENS, MEM_WIDTH), lambda b, t: (b, 0)),
             pl.BlockSpec((MEM_TOKENS, MEM_WIDTH), lambda b, t: (b, 0))]
    aliases = {}
    if out is not None:
        args.append(out)
        specs.append(pl.BlockSpec(memory_space=pl.ANY))
        aliases = {3: 0}
    return pl.pallas_call(
        _mem_kernel,
        grid=(B, nt),
        in_specs=specs,
        out_specs=pl.BlockSpec((tq, MEM_WIDTH), lambda b, t: (rb0 + b * nt + t, 0)),
        out_shape=jax.ShapeDtypeStruct((qn.shape[0], MEM_WIDTH), BF16),
        input_output_aliases=aliases,
        compiler_params=_cparams(("parallel", "arbitrary")),
        name="mem_attend",
    )(*args)


def _route_kernel(pq_ref, keys_ref, s1_ref, e1_ref, s2_ref, e2_ref, tau_ref):
    half = PEER_QDIM // 2
    ninf = -jnp.inf
    K = PEER_TOPK

    def top_distinct(s):
        cur, vals = s, []
        for _ in range(K):
            mx = jnp.max(cur, axis=0, keepdims=True)
            vals.append(mx)
            cur = jnp.where(cur == mx, ninf, cur)
        return vals

    def pick_first_max(cur, rows):
        mx = jnp.max(cur, axis=0, keepdims=True)
        first = jnp.min(jnp.where(cur == mx, rows, float(cur.shape[0])), axis=0, keepdims=True)
        return mx, rows == first

    def top_ranked(s):
        rows = lax.broadcasted_iota(jnp.int32, s.shape, 0).astype(F32)
        cur, vals, rank = s, [], jnp.full(s.shape, float(K), F32)
        for it in range(K):
            mx, hit = pick_first_max(cur, rows)
            vals.append(mx)
            rank = jnp.where(hit, float(it), rank)
            cur = jnp.where(hit, ninf, cur)
        return vals, rank

    def count_ge(x, thr):
        return jnp.sum(jnp.where(x >= thr, 1.0, 0.0), axis=0, keepdims=True)

    def all_cands(v1, v2m):
        return jnp.concatenate([v1[r] + v2m for r in range(K)], axis=0)

    def finish(h, s1, s2, v1, v2m, cand, tau):
        m0 = v1[0] + v2m[0:1, :]
        zsum = jnp.sum(jnp.where(cand >= tau, jnp.exp(cand - m0), 0.0), axis=0, keepdims=True)
        s1_ref[h] = jnp.where(s1 >= v1[K - 1], s1, ninf)
        s2_ref[h] = jnp.where(s2 >= v2m[K - 1:K, :], s2, ninf)
        e1_ref[h] = jnp.exp(s1 - v1[0])
        e2_ref[h] = jnp.exp(s2 - v2m[0:1, :]) / zsum
        tau_ref[h:h + 1, :] = tau

    tm = pq_ref.shape[0]

    def scores(h):
        q1 = pq_ref[:, h * PEER_QDIM:h * PEER_QDIM + half]
        q2 = pq_ref[:, h * PEER_QDIM + half:(h + 1) * PEER_QDIM]
        return (lax.dot_general(keys_ref[h, 0], q1, NT_DIMS, precision=HIGHEST, preferred_element_type=F32),
                lax.dot_general(keys_ref[h, 1], q2, NT_DIMS, precision=HIGHEST, preferred_element_type=F32))

    any_tie = []
    for h in range(PEER_HEADS):
        s1, s2 = scores(h)
        v1 = top_distinct(s1)
        v2 = top_distinct(s2)
        v2m = jnp.concatenate(v2, axis=0)
        rowi = lax.broadcasted_iota(jnp.int32, (8, tm), 0)
        blocks = [v1[0] + v2m, v1[1] + v2m[0:8, :]]
        for r in range(2, 8):
            blocks.append(jnp.where(rowi < K // (r + 1), v1[r] + v2m[0:8, :], ninf))
        blocks.append(jnp.concatenate(v1[8:], axis=0) + v2m[0:1, :])
        cand = jnp.concatenate(blocks, axis=0)
        cur = cand
        for _ in range(K):
            tau = jnp.max(cur, axis=0, keepdims=True)
            cur = jnp.where(cur == tau, ninf, cur)
        full = all_cands(v1, v2m)
        tied = jnp.where(count_ge(full, tau) == K,
                         jnp.where(count_ge(s1, v1[K - 1]) == K,
                                   jnp.where(count_ge(s2, v2[K - 1]) == K, 0.0, 1.0), 1.0), 1.0)
        finish(h, s1, s2, v1, v2m, cand, tau)
        any_tie.append(jnp.max(tied))

    for h in range(PEER_HEADS):
        @pl.when(any_tie[h] > 0.0)
        def _with_ties():
            s1, s2 = scores(h)
            w1, rank1 = top_ranked(s1)
            w2, rank2 = top_ranked(s2)
            w2m = jnp.concatenate(w2, axis=0)
            allc = all_cands(w1, w2m)
            rows = lax.broadcasted_iota(jnp.int32, allc.shape, 0).astype(F32)
            cur, picked = allc, jnp.zeros(allc.shape, F32)
            for _ in range(K):
                _, hit = pick_first_max(cur, rows)
                picked = jnp.where(hit, 1.0, picked)
                cur = jnp.where(hit, ninf, cur)
            m0 = w1[0] + w2m[0:1, :]
            zsum = jnp.sum(picked * jnp.exp(allc - m0), axis=0, keepdims=True)
            lmap = jnp.zeros(s1.shape, F32)
            for r in range(K):
                length = jnp.sum(picked[r * K:(r + 1) * K, :], axis=0, keepdims=True)
                lmap = jnp.where(rank1 == float(r), length, lmap)
            s1_ref[h] = lmap
            s2_ref[h] = jnp.where(rank2 < float(K), -rank2, -1000.0)
            e1_ref[h] = jnp.exp(s1 - w1[0])
            e2_ref[h] = jnp.exp(s2 - w2m[0:1, :]) / zsum
            tau_ref[h:h + 1, :] = jnp.ones((1, tm), F32)


def _peer_route(pq, keys, *, tm=256):
    N = pq.shape[0]
    big = jax.ShapeDtypeStruct((PEER_HEADS, N_KEYS, N), F32)
    bspec = pl.BlockSpec((PEER_HEADS, N_KEYS, tm), lambda i: (0, 0, i))
    return pl.pallas_call(
        _route_kernel,
        grid=(N // tm,),
        in_specs=[pl.BlockSpec((tm, PEER_HEADS * PEER_QDIM), lambda i: (i, 0)),
                  pl.BlockSpec((PEER_HEADS, 2, N_KEYS, PEER_QDIM // 2), lambda i: (0, 0, 0, 0))],
        out_specs=[bspec, bspec, bspec, bspec, pl.BlockSpec((PEER_HEADS, tm), lambda i: (0, i))],
        out_shape=[big, big, big, big, jax.ShapeDtypeStruct((PEER_HEADS, N), F32)],
        compiler_params=_cparams(("parallel",)),
        name="peer_route",
    )(pq, keys)


def _dense_kernel(h_ref, g_ref, u_ref, v_ref, s1_ref, e1_ref, s2_ref, e2_ref, tau_ref, o_ref, xn_ref, c_ref,
                  *, te):
    j = pl.program_id(1)
    na = te // N_KEYS

    @pl.when(j == 0)
    def _init():
        x = h_ref[...]
        ms = jnp.mean(x * x, axis=-1, keepdims=True)
        xn_ref[...] = (x * lax.rsqrt(ms + EPS) * g_ref[...]).astype(BF16)
        o_ref[...] = x

    pre = lax.dot_general(xn_ref[...], u_ref[...], NT_DIMS, preferred_element_type=F32)
    act = 0.5 * pre * (1.0 + lax.erf(pre * (2.0 ** -0.5)))
    for al in range(na):
        a = j * na + al
        gt = jnp.zeros((N_KEYS, h_ref.shape[0]), F32)
        for h in range(PEER_HEADS):
            hit = (s1_ref[h, pl.ds(a, 1), :] + s2_ref[h]) >= tau_ref[h:h + 1, :]
            gt = gt + jnp.where(hit, e1_ref[h, pl.ds(a, 1), :] * e2_ref[h], 0.0)
        c_ref[:, al * N_KEYS:(al + 1) * N_KEYS] = (gt.T * act[:, al * N_KEYS:(al + 1) * N_KEYS]).astype(BF16)
    o_ref[...] += jnp.dot(c_ref[...], v_ref[...], preferred_element_type=F32)


def _peer_dense(h, gain, u, v, layer, s1, e1, s2, e2, tau, *, tm=512, te=512):
    N, D = h.shape
    na = te // N_KEYS
    nj = N_EXPERTS // te
    assert N % tm == 0 and N_EXPERTS % te == 0
    rows = pl.BlockSpec((PEER_HEADS, N_KEYS, tm), lambda i, j: (0, 0, i), pipeline_mode=pl.Buffered(1))
    full = pl.BlockSpec((PEER_HEADS, N_KEYS, tm), lambda i, j: (0, 0, i))
    return pl.pallas_call(
        functools.partial(_dense_kernel, te=te),
        grid=(N // tm, nj),
        in_specs=[pl.BlockSpec((tm, D), lambda i, j: (i, 0), pipeline_mode=pl.Buffered(1)),
                  pl.BlockSpec((1, D), lambda i, j: (0, 0)),
                  pl.BlockSpec((None, te, D), lambda i, j: (layer, j, 0)),
                  pl.BlockSpec((None, te, D), lambda i, j: (layer, j, 0)),
                  rows, rows, full, full,
                  pl.BlockSpec((PEER_HEADS, tm), lambda i, j: (0, i))],
        out_specs=pl.BlockSpec((tm, D), lambda i, j: (i, 0), pipeline_mode=pl.Buffered(1)),
        out_shape=jax.ShapeDtypeStruct((N, D), F32),
        scratch_shapes=[pltpu.VMEM((tm, D), BF16), pltpu.VMEM((tm, te), BF16)],
        compiler_params=_cparams(("parallel", "arbitrary")),
        name="peer_dense",
    )(h, gain.reshape(1, D).astype(F32), u, v, s1, e1, s2, e2, tau)


def _row(x, width=None):
    x = x.astype(F32).reshape(1, -1)
    if width is not None and x.shape[1] < width:
        x = jnp.pad(x, ((0, 0), (0, width - x.shape[1])))
    return x


def _cast_kernel(x_ref, o_ref):
    o_ref[...] = x_ref[...].astype(o_ref.dtype)


def _to_bf16(w, *, block_bytes=8 * 1024 * 1024):
    depth, rows, cols = w.shape
    tr = max(8, min(rows, block_bytes // (cols * 4)))
    assert rows % tr == 0
    return pl.pallas_call(
        _cast_kernel,
        grid=(depth, rows // tr),
        in_specs=[pl.BlockSpec((None, tr, cols), lambda l, r: (l, r, 0))],
        out_specs=pl.BlockSpec((None, tr, cols), lambda l, r: (l, r, 0)),
        out_shape=jax.ShapeDtypeStruct(w.shape, BF16),
        compiler_params=_cparams(("parallel", "parallel")),
        name="to_bf16",
    )(w)


PACK_ROWS = 256
PACK_GROUP = SSD_HEADS


def _pack_kernel(w_ref, dt_ref, o_ref):
    r = pl.program_id(1)
    last = pl.num_programs(1) - 1

    @pl.when(r < last)
    def _copy():
        o_ref[...] = w_ref[...].reshape(PACK_ROWS, w_ref.shape[-1]).astype(BF16)

    @pl.when(r == last)
    def _dt_and_pad():
        o_ref[0:PACK_GROUP, :] = dt_ref[...].astype(BF16)
        o_ref[PACK_GROUP:, :] = jnp.zeros((PACK_ROWS - PACK_GROUP, o_ref.shape[-1]), BF16)


def _pack_w_in(w_in):
    depth, d, cols = w_in.shape
    xbc_end = 3 * SSD_INNER
    assert cols == COL_DT + SSD_HEADS and cols % PACK_GROUP == 0 and PROJ_COLS % PACK_ROWS == 0
    gpb = PACK_ROWS // PACK_GROUP
    n_groups = cols // PACK_GROUP
    n_head = xbc_end // PACK_ROWS
    skip = (xbc_end + SSD_HEADS) // PACK_GROUP
    wt = jnp.swapaxes(w_in, 1, 2).reshape(depth, n_groups, PACK_GROUP, d)

    def src_group(l, r):
        g = jnp.where(r < n_head, r * gpb, skip + (r - n_head) * gpb)
        return (l, jnp.minimum(g, n_groups - gpb), 0, 0)

    return pl.pallas_call(
        _pack_kernel,
        grid=(depth, PROJ_COLS // PACK_ROWS),
        in_specs=[pl.BlockSpec((pl.Element(1), pl.Element(gpb), pl.Element(PACK_GROUP), pl.Element(d)),
                               src_group),
                  pl.BlockSpec((None, None, PACK_GROUP, d), lambda l, r: (l, xbc_end // PACK_GROUP, 0, 0))],
        out_specs=pl.BlockSpec((None, PACK_ROWS, d), lambda l, r: (l, r, 0)),
        out_shape=jax.ShapeDtypeStruct((depth, PROJ_COLS, d), BF16),
        compiler_params=_cparams(("parallel", "arbitrary")),
        name="pack_w_in",
    )(wt, wt)


def _layer_params(l, conv_w, conv_b, dt_bias, a_log, d_skip, ssd_norm, hgrn_norm, swa_qnorm,
                  swa_knorm, swa_sinks):
    eh = lax.broadcasted_iota(jnp.int32, (LANE, SSD_INNER), 0)
    ec = lax.broadcasted_iota(jnp.int32, (LANE, SSD_INNER), 1)
    return {
        'expand': (ec // SSD_HD == eh).astype(BF16),
        'cwx': conv_w[l][:, :SSD_INNER].astype(F32), 'cwb': conv_w[l][:, SSD_INNER:].astype(F32),
        'cbx': _row(conv_b[l][:SSD_INNER]), 'cbb': _row(conv_b[l][SSD_INNER:]),
        'dtb': _row(dt_bias[l], LANE), 'alog': _row(a_log[l], LANE),
        'dsk': _row(jnp.repeat(d_skip[l], SSD_HD)), 'ssd_norm': _row(ssd_norm[l]),
        'hgrn_norm': _row(jnp.tile(hgrn_norm[l], HG_HEADS)),
        'swa_qnorm': _row(jnp.tile(swa_qnorm[l], SWA_HEADS)),
        'swa_knorm': _row(jnp.tile(swa_knorm[l], SWA_KV_HEADS)),
        'swa_sinks': swa_sinks[l].astype(F32),
    }


def kernel(x_prompt, x_sample, mem_prompt, state_ssm, state_ssd_conv, state_hgrn, cache_swa_k, cache_swa_v,
           cache_mem_k, cache_mem_v, norm_mix, w_in, conv_w, conv_b, dt_bias, a_log, d_skip, ssd_norm,
           hgrn_lb, hgrn_norm, swa_qnorm, swa_knorm, swa_sinks, w_out, norm_mem, norm_memtok, w_mq, w_mk,
           w_mv, mem_qnorm, mem_knorm, w_mo, norm_ffn, w_pq, peer_keys, peer_u, peer_v):
    depth = w_in.shape[0]
    bp, tp, _ = x_prompt.shape
    bs, ts, _ = x_sample.shape
    n_p, n_s = bp * tp, bs * ts
    lbp = jax.nn.softmax(hgrn_lb.astype(F32), axis=0)
    lower = jnp.cumsum(lbp, axis=0) - lbp[0:1]
    h = jnp.concatenate([x_prompt.reshape(n_p, D_MODEL), x_sample.reshape(n_s, D_MODEL)], axis=0)
    mem2d = mem_prompt.reshape(bp * MEM_TOKENS, D_MODEL)
    outs = {k: [] for k in ('ssm_p', 'conv_p', 'hg_p', 'swk_p', 'swv_p', 'mk_p', 'mv_p',
                            'ssm_s', 'conv_s', 'hg_s', 'swk_s', 'swv_s')}
    w_in_b = _pack_w_in(w_in)
    w_out_b, w_mq_b, w_mk_b, w_mv_b, w_mo_b, w_pq_b, u_b, v_b = (
        _to_bf16(t) for t in (w_out, w_mq, w_mk, w_mv, w_mo, w_pq, peer_u, peer_v))
    for l in range(depth):
        p = _layer_params(l, conv_w, conv_b, dt_bias, a_log, d_skip, ssd_norm, hgrn_norm,
                          swa_qnorm, swa_knorm, swa_sinks)
        lb = _row(lower[l])
        proj = _mm(h, w_in_b, layer=l, w_nk=True, gain=norm_mix[l], name="proj_in")
        ycat, conv_p, ssm_p = _ssd(proj, p, row0=0, B=bp, T=tp)
        ycat, conv_s, ssm_s = _ssd(proj, p, row0=n_p, B=bs, T=ts, conv0=state_ssd_conv[l],
                                   h0=state_ssm[l], ycat=ycat)
        ycat, hg_p = _hgrn(proj, p, lb, row0=0, B=bp, T=tp, Lc=128, ycat=ycat)
        ycat, hg_s = _hgrn(proj, p, lb, row0=n_p, B=bs, T=ts, Lc=ts, s0=state_hgrn[l], ycat=ycat)
        ycat, swk_p, swv_p = _swa(proj, p, row0=0, B=bp, T=tp, ycat=ycat)
        ycat, swk_s, swv_s = _swa(proj, p, row0=n_p, B=bs, T=ts, ck=cache_swa_k[l], cv=cache_swa_v[l],
                                  ycat=ycat)
        h = _mm(ycat, w_out_b, layer=l, res=h, name="proj_out")
        mk = _mm(mem2d, w_mk_b, layer=l, gain=norm_memtok[l], head_gain=mem_knorm[l], name="mem_k")
        mv = _mm(mem2d, w_mv_b, layer=l, gain=norm_memtok[l], name="mem_v")
        qn = _mm(h, w_mq_b, layer=l, gain=norm_mem[l], head_gain=mem_qnorm[l], name="mem_q")
        om = _mem_attend(qn, mk, mv, row0=0, B=bp, T=tp, tq=512)
        om = _mem_attend(qn, cache_mem_k[l].reshape(bs * MEM_TOKENS, MEM_WIDTH),
                         cache_mem_v[l].reshape(bs * MEM_TOKENS, MEM_WIDTH), row0=n_p, B=bs, T=ts, tq=ts,
                         out=om)
        h = _mm(om, w_mo_b, layer=l, res=h, name="mem_o")
        pq = _mm(h, w_pq_b, layer=l, gain=norm_ffn[l], name="peer_q")
        s1, e1, s2, e2, tau = _peer_route(pq, peer_keys[l].astype(F32))
        h = _peer_dense(h, norm_ffn[l], u_b, v_b, l, s1, e1, s2, e2, tau)
        for name, val in (('ssm_p', ssm_p), ('conv_p', conv_p), ('hg_p', hg_p), ('swk_p', swk_p),
                          ('swv_p', swv_p), ('ssm_s', ssm_s), ('conv_s', conv_s), ('hg_s', hg_s),
                          ('swk_s', swk_s), ('swv_s', swv_s)):
            outs[name].append(val)
        outs['mk_p'].append(mk.reshape(bp, MEM_TOKENS, MEM_HEADS, MEM_HD))
        outs['mv_p'].append(mv.reshape(bp, MEM_TOKENS, MEM_HEADS, MEM_HD))
    st = lambda k: jnp.stack(outs[k])
    return (h[:n_p].reshape(bp, tp, D_MODEL), h[n_p:].reshape(bs, ts, D_MODEL),
            st('ssm_p'), st('conv_p'), st('hg_p'), st('swk_p'), st('swv_p'), st('mk_p'), st('mv_p'),
            st('ssm_s'), st('conv_s'), st('hg_s'), st('swk_s'), st('swv_s'))
```

```python
import functools
import math

import jax
import jax.numpy as jnp
from jax import lax
from jax.experimental import pallas as pl
from jax.experimental.pallas import tpu as pltpu

F32 = jnp.float32
BF16 = jnp.bfloat16
HIGHEST = lax.Precision.HIGHEST

D_MODEL = 4096
PAST_LEN = 2048
CHUNK = 64
SSD_INNER = 2048
SSD_HD = 64
SSD_HEADS = 32
SSD_GROUPS = 8
SSD_STATE = 128
SSD_GW = SSD_INNER // SSD_GROUPS
SSD_REP = SSD_HEADS // SSD_GROUPS
HG_WIDTH = 1024
HG_HD = 128
HG_HEADS = 8
HG_BLOCK = 16
HG_LONG_BLOCK = 64
HG_SAFE_DECAY = 80.0
SWA_HD = 64
SWA_HEADS = 16
SWA_KV_HEADS = 2
SWA_REP = SWA_HEADS // SWA_KV_HEADS
WINDOW = 128
MEM_TOKENS = 256
MEM_HEADS = 4
MEM_HD = 128
MEM_WIDTH = 512
N_KEYS = 128
N_EXPERTS = N_KEYS * N_KEYS
PEER_HEADS = 8
PEER_QDIM = 256
PEER_TOPK = 16
EPS = 1e-6
NEG = -1e30
TINY = 1e-30
LANE = 128
MXU_N = 256

COL_Z, COL_XS, COL_BC = 0, 2048, 4096
COL_HQ, COL_HF, COL_HI, COL_HG = 6144, 7168, 8192, 9216
COL_SQ, COL_SK, COL_SV, COL_DT = 10240, 11264, 11392, 11520
PROJ_COLS = 11776
YCOL_SSD, YCOL_HG, YCOL_SWA = 0, 2048, 3072

VMEM_LIMIT = 60 * 1024 * 1024

NT_DIMS = (((1,), (1,)), ((), ()))
TN_DIMS = (((0,), (0,)), ((), ()))


def _cparams(sem):
    return pltpu.CompilerParams(dimension_semantics=sem, vmem_limit_bytes=VMEM_LIMIT)


def _sigmoid(x):
    return 1.0 / (1.0 + jnp.exp(-x))


def _silu(x):
    return x * _sigmoid(x)


def _split_dot(x, ones_bf16):
    hi = x.astype(BF16)
    lo = (x - hi.astype(F32)).astype(BF16)
    return (jnp.dot(hi, ones_bf16, preferred_element_type=F32)
            + jnp.dot(lo, ones_bf16, preferred_element_type=F32))


def _mm_kernel(*refs, norm, head_norm, residual, w_nk):
    it = iter(refs)
    x_ref = next(it)
    g_ref = next(it) if norm else None
    w_ref = next(it)
    hg_ref = next(it) if head_norm else None
    r_ref = next(it) if residual else None
    o_ref = next(it)
    xn_ref = next(it) if norm else None
    if norm:
        @pl.when(pl.program_id(1) == 0)
        def _():
            x = x_ref[...].astype(F32)
            ms = jnp.mean(x * x, axis=-1, keepdims=True)
            xn_ref[...] = (x * lax.rsqrt(ms + EPS) * g_ref[...]).astype(BF16)
        xb = xn_ref[...]
    else:
        xb = x_ref[...]
    if w_nk:
        acc = lax.dot_general(xb, w_ref[...], NT_DIMS, preferred_element_type=F32)
    else:
        acc = jnp.dot(xb, w_ref[...], preferred_element_type=F32)
    if head_norm:
        parts = []
        for c in range(acc.shape[1] // LANE):
            a = acc[:, c * LANE:(c + 1) * LANE]
            ms = jnp.mean(a * a, axis=-1, keepdims=True)
            parts.append(a * lax.rsqrt(ms + EPS))
        acc = jnp.concatenate(parts, axis=1) * hg_ref[...]
    if residual:
        acc = acc + r_ref[...]
    o_ref[...] = acc.astype(o_ref.dtype)


def _mm(x, w, *, layer=None, w_nk=False, gain=None, head_gain=None, res=None, out_dtype=F32, tn=512,
        name="mm"):
    M, K = x.shape
    N = w.shape[-2] if w_nk else w.shape[-1]
    norm, head_norm, residual = gain is not None, head_gain is not None, res is not None
    tm = next(t for t in ((768, 512, 256, 128) if norm else (1024, 512, 256, 128)) if M % t == 0)
    tn = min(tn, N)
    assert N % tn == 0
    args, specs = [x], [pl.BlockSpec((tm, K), lambda i, j: (i, 0))]
    if norm:
        args.append(gain.reshape(1, K).astype(F32))
        specs.append(pl.BlockSpec((1, K), lambda i, j: (0, 0)))
    args.append(w)
    if w_nk:
        specs.append(pl.BlockSpec((None, tn, K), lambda i, j: (layer, j, 0)))
    elif layer is None:
        specs.append(pl.BlockSpec((K, tn), lambda i, j: (0, j)))
    else:
        specs.append(pl.BlockSpec((None, K, tn), lambda i, j: (layer, 0, j)))
    if head_norm:
        args.append(jnp.tile(head_gain.astype(F32), N // head_gain.shape[0]).reshape(1, N))
        specs.append(pl.BlockSpec((1, tn), lambda i, j: (0, j)))
    if residual:
        args.append(res)
        specs.append(pl.BlockSpec((tm, tn), lambda i, j: (i, j)))
    return pl.pallas_call(
        functools.partial(_mm_kernel, norm=norm, head_norm=head_norm, residual=residual, w_nk=w_nk),
        grid=(M // tm, N // tn),
        in_specs=specs,
        out_specs=pl.BlockSpec((tm, tn), lambda i, j: (i, j)),
        out_shape=jax.ShapeDtypeStruct((M, N), out_dtype),
        scratch_shapes=[pltpu.VMEM((tm, K), BF16)] if norm else [],
        compiler_params=_cparams(("parallel", "arbitrary")),
        name=name,
    )(*args)


def _ssd_kernel(*refs, L, has_state, aliased):
    (z_ref, xs_ref, bc_ref, dt_ref, cwx_ref, cwb_ref, cbx_ref, cbb_ref, dtb_ref, alog_ref,
     dsk_ref, gn_ref, exp_ref) = refs[:13]
    rest = refs[13:]
    if has_state:
        conv0_ref, h0_ref = rest[:2]
        rest = rest[2:]
    if aliased:
        rest = rest[1:]
    y_ref, convo_ref, ho_ref, xpx_ref, xpb_ref, hT_ref = rest
    c = pl.program_id(1)
    nc = pl.num_programs(1)

    @pl.when(c == 0)
    def _init():
        if has_state:
            xpx_ref[0:8, :] = jnp.zeros((8, SSD_INNER), F32)
            xpb_ref[0:8, :] = jnp.zeros((8, SSD_INNER), F32)
            xpx_ref[5:8, :] = conv0_ref[0, :, 0:SSD_INNER]
            xpb_ref[5:8, :] = conv0_ref[0, :, SSD_INNER:2 * SSD_INNER]
            for g in range(SSD_GROUPS):
                hT_ref[g] = h0_ref[0, g * SSD_GW:(g + 1) * SSD_GW, :].T
        else:
            xpx_ref[0:8, :] = jnp.zeros((8, SSD_INNER), F32)
            xpb_ref[0:8, :] = jnp.zeros((8, SSD_INNER), F32)
            hT_ref[...] = jnp.zeros(hT_ref.shape, F32)

    xpx_ref[8:8 + L, :] = xs_ref[...]
    xpb_ref[8:8 + L, :] = bc_ref[...]

    def conv(xp_ref, w_ref, b_ref):
        acc = b_ref[...] + w_ref[3:4, :] * xp_ref[8:8 + L, :]
        for j in range(3):
            acc = acc + w_ref[j:j + 1, :] * xp_ref[5 + j:5 + j + L, :]
        return acc

    xc = _silu(conv(xpx_ref, cwx_ref, cbx_ref))
    bcc = _silu(conv(xpb_ref, cwb_ref, cbb_ref))
    tail_x = xpx_ref[5 + L:8 + L, :]
    tail_b = xpb_ref[5 + L:8 + L, :]
    xpx_ref[5:8, :] = tail_x
    xpb_ref[5:8, :] = tail_b
    convo_ref[0, :, 0:SSD_INNER] = tail_x
    convo_ref[0, :, SSD_INNER:2 * SSD_INNER] = tail_b

    dtr = dt_ref[...] + dtb_ref[...]
    dt = jnp.maximum(dtr, 0.0) + jnp.log(1.0 + jnp.exp(-jnp.abs(dtr)))
    a = -jnp.exp(alog_ref[...])
    dta = dt * a
    ri = lax.broadcasted_iota(jnp.int32, (L, L), 0)
    ci = lax.broadcasted_iota(jnp.int32, (L, L), 1)
    causal = ri >= ci
    acum = jnp.dot(causal.astype(F32), dta, precision=HIGHEST, preferred_element_type=F32)
    both = jnp.concatenate([acum, dt], axis=0)
    p0 = both.astype(BF16)
    r1 = both - p0.astype(F32)
    p1 = r1.astype(BF16)
    p2 = (r1 - p1.astype(F32)).astype(BF16)
    expand = exp_ref[...]
    bothx = ((jnp.dot(p0, expand, preferred_element_type=F32)
              + jnp.dot(p1, expand, preferred_element_type=F32))
             + jnp.dot(p2, expand, preferred_element_type=F32))
    acx = bothx[:L, :]
    dtx = bothx[L:, :]
    eax = jnp.exp(acx)
    lastx = acx[L - 1:L, :]
    tailw = jnp.exp(lastx - acx) * dtx
    cdx = jnp.exp(lastx)
    acT = acum.T
    dtT = dt.T
    lane_head = lax.broadcasted_iota(jnp.int32, (L, SSD_GW), 1) // SSD_HD
    z = z_ref[...]

    for g in range(SSD_GROUPS):
        gs = slice(g * SSD_GW, (g + 1) * SSD_GW)
        Bg = bcc[:, g * SSD_STATE:(g + 1) * SSD_STATE].astype(BF16)
        Cg = bcc[:, SSD_GROUPS * SSD_STATE + g * SSD_STATE:
                 SSD_GROUPS * SSD_STATE + (g + 1) * SSD_STATE].astype(BF16)
        cb = lax.dot_general(Cg, Bg, NT_DIMS, preferred_element_type=F32)
        ws = []
        for r in range(SSD_REP):
            hd = g * SSD_REP + r
            seg = acum[:, hd:hd + 1] - acT[hd:hd + 1, :]
            dec = jnp.where(causal, jnp.exp(jnp.where(causal, seg, 0.0)), 0.0)
            ws.append(cb * dec * dtT[hd:hd + 1, :])
        wst = jnp.concatenate(ws, axis=0).astype(BF16)
        xg = xc[:, gs]
        full = jnp.dot(wst, xg.astype(BF16), preferred_element_type=F32)
        y_intra = jnp.zeros((L, SSD_GW), F32)
        for r in range(SSD_REP):
            y_intra = y_intra + jnp.where(lane_head == r, full[r * L:(r + 1) * L, :], 0.0)
        hTg = hT_ref[g]
        y_inter = jnp.dot(Cg, hTg.astype(BF16), preferred_element_type=F32) * eax[:, gs]
        xt = (xg * tailw[:, gs]).astype(BF16)
        hT_ref[g] = hTg * cdx[:, gs] + lax.dot_general(Bg, xt, TN_DIMS, preferred_element_type=F32)
        yg = y_intra + y_inter + dsk_ref[:, gs] * xg
        u = yg * _silu(z[:, gs])
        ms = jnp.mean(u * u, axis=-1, keepdims=True)
        y_ref[:, gs] = (u * lax.rsqrt(ms + EPS) * gn_ref[:, gs]).astype(y_ref.dtype)

    @pl.when(c == nc - 1)
    def _fin():
        for g in range(SSD_GROUPS):
            ho_ref[0, g * SSD_GW:(g + 1) * SSD_GW, :] = hT_ref[g].T


def _mixed_out(ycat, n_total, args, specs):
    aliases = {}
    if ycat is not None:
        args.append(ycat)
        specs.append(pl.BlockSpec(memory_space=pl.ANY))
        aliases = {len(args) - 1: 0}
    return jax.ShapeDtypeStruct((n_total, D_MODEL), BF16), aliases


def _ssd(proj, p, *, row0, B, T, conv0=None, h0=None, ycat=None):
    L = math.gcd(T, CHUNK)
    nc = T // L
    rb0 = row0 // L
    has_state = conv0 is not None

    def rows(col):
        return lambda b, c: (rb0 + b * nc + c, col)

    const = lambda b, c: (0, 0)
    args = [proj, proj, proj, proj, p['cwx'], p['cwb'], p['cbx'], p['cbb'], p['dtb'], p['alog'],
            p['dsk'], p['ssd_norm'], p['expand']]
    specs = [pl.BlockSpec((L, SSD_INNER), rows(COL_Z // SSD_INNER)),
             pl.BlockSpec((L, SSD_INNER), rows(COL_XS // SSD_INNER)),
             pl.BlockSpec((L, SSD_INNER), rows(COL_BC // SSD_INNER)),
             pl.BlockSpec((L, LANE), rows(COL_DT // LANE)),
             pl.BlockSpec((4, SSD_INNER), const), pl.BlockSpec((4, SSD_INNER), const),
             pl.BlockSpec((1, SSD_INNER), const), pl.BlockSpec((1, SSD_INNER), const),
             pl.BlockSpec((1, LANE), const), pl.BlockSpec((1, LANE), const),
             pl.BlockSpec((1, SSD_INNER), const), pl.BlockSpec((1, SSD_INNER), const),
             pl.BlockSpec((LANE, SSD_INNER), const)]
    if has_state:
        args += [conv0, h0.reshape(B, SSD_INNER, SSD_STATE)]
        specs += [pl.BlockSpec((1, 3, 2 * SSD_INNER), lambda b, c: (b, 0, 0)),
                  pl.BlockSpec((1, SSD_INNER, SSD_STATE), lambda b, c: (b, 0, 0))]
    yshape, aliases = _mixed_out(ycat, proj.shape[0], args, specs)
    y, convo, ho = pl.pallas_call(
        functools.partial(_ssd_kernel, L=L, has_state=has_state, aliased=ycat is not None),
        grid=(B, nc),
        in_specs=specs,
        out_specs=[pl.BlockSpec((L, SSD_INNER), rows(0)),
                   pl.BlockSpec((1, 3, 2 * SSD_INNER), lambda b, c: (b, 0, 0)),
                   pl.BlockSpec((1, SSD_INNER, SSD_STATE), lambda b, c: (b, 0, 0))],
        out_shape=[yshape,
                   jax.ShapeDtypeStruct((B, 3, 2 * SSD_INNER), F32),
                   jax.ShapeDtypeStruct((B, SSD_INNER, SSD_STATE), F32)],
        scratch_shapes=[pltpu.VMEM((8 + L, SSD_INNER), F32), pltpu.VMEM((8 + L, SSD_INNER), F32),
                        pltpu.VMEM((SSD_GROUPS, SSD_STATE, SSD_GW), F32)],
        input_output_aliases=aliases,
        compiler_params=_cparams(("parallel", "arbitrary")),
        name="ssd",
    )(*args)
    return y, convo, ho.reshape(B, SSD_HEADS, SSD_HD, SSD_STATE)


def _hgrn_kernel(*refs, Lc, has_state, aliased):
    q_ref, f_ref, i_ref, g_ref, lb_ref, gn_ref = refs[:6]
    rest = refs[6:]
    if has_state:
        s0_ref = rest[0]
        rest = rest[1:]
    if aliased:
        rest = rest[1:]
    y_ref, so_ref, kp_ref, gp_ref, vp_ref, st_ref, oi_ref = rest
    c = pl.program_id(1)
    nc = pl.num_programs(1)
    nb = Lc // HG_BLOCK

    @pl.when(c == 0)
    def _init():
        for h in range(HG_HEADS):
            st_ref[h] = s0_ref[0, h].T if has_state else jnp.zeros((HG_HD, HG_HD), F32)
        kp_ref[0:HG_BLOCK, :] = jnp.zeros((HG_BLOCK, HG_WIDTH), F32)
        gp_ref[0:HG_BLOCK, :] = jnp.zeros((HG_BLOCK, HG_WIDTH), F32)
        vp_ref[0:HG_BLOCK, :] = jnp.zeros((HG_BLOCK, HG_WIDTH), F32)

    zf = f_ref[...]
    lb = lb_ref[...]
    f = lb + (1.0 - lb) * _sigmoid(zf)
    logf = jnp.log(jnp.maximum(f, TINY))
    k = (1.0 - lb) * _sigmoid(-zf)
    q = _silu(q_ref[...]) * (HG_HD ** -0.5)
    v = i_ref[...]
    vb = v.astype(BF16)
    gate = _silu(g_ref[...])
    ri = lax.broadcasted_iota(jnp.int32, (Lc, Lc), 0)
    ci = lax.broadcasted_iota(jnp.int32, (Lc, Lc), 1)

    def log_decays(blk):
        same = (ri // blk) == (ci // blk)
        ltri = jnp.where(same, jnp.where(ri >= ci, 1.0, 0.0), 0.0)
        lall = jnp.where(same, 1.0, 0.0)
        return (ltri, jnp.dot(ltri, logf, precision=HIGHEST, preferred_element_type=F32),
                jnp.dot(lall, logf, precision=HIGHEST, preferred_element_type=F32))

    def pass_state(blk, g_cum, g_tot, intra):
        qg = (q * jnp.exp(g_cum)).astype(BF16)
        kdec = (k * jnp.exp(g_tot - g_cum)).astype(BF16)
        cd = jnp.exp(g_tot)
        for h in range(HG_HEADS):
            hs = slice(h * HG_HD, (h + 1) * HG_HD)
            st = st_ref[h]
            outs = []
            for b in range(Lc // blk):
                rs = slice(b * blk, (b + 1) * blk)
                outs.append(lax.dot_general(qg[rs, hs], st.astype(BF16), NT_DIMS, preferred_element_type=F32))
                st = st * cd[b * blk:b * blk + 1, hs] + lax.dot_general(
                    vb[rs, hs], kdec[rs, hs], TN_DIMS, preferred_element_type=F32)
            st_ref[h] = st
            o = intra(hs, qg) + (jnp.concatenate(outs, axis=0) if len(outs) > 1 else outs[0])
            ms = jnp.mean(o * o, axis=-1, keepdims=True)
            y_ref[:, hs] = (o * lax.rsqrt(ms + EPS) * gn_ref[:, hs] * gate[:, hs]).astype(y_ref.dtype)

    blk_long = min(HG_LONG_BLOCK, Lc)
    ltri_l, g_l, gt_l = log_decays(blk_long)
    worst = jnp.max(-gt_l)

    @pl.when(worst < HG_SAFE_DECAY)
    def _factorised():
        kgrow = (k * jnp.exp(-g_l)).astype(BF16)
        in_block = ltri_l > 0.5

        def intra(hs, qg):
            att = lax.dot_general(qg[:, hs], kgrow[:, hs], NT_DIMS, preferred_element_type=F32)
            att = jnp.where(in_block, att, 0.0).astype(BF16)
            return jnp.dot(att, vb[:, hs], preferred_element_type=F32)

        pass_state(blk_long, g_l, gt_l, intra)

    @pl.when(worst >= HG_SAFE_DECAY)
    def _unfactorised():
        _, gb, gl = log_decays(HG_BLOCK)
        kp_ref[HG_BLOCK:HG_BLOCK + Lc, :] = k
        gp_ref[HG_BLOCK:HG_BLOCK + Lc, :] = gb
        vp_ref[HG_BLOCK:HG_BLOCK + Lc, :] = v
        tpos = lax.broadcasted_iota(jnp.int32, (Lc, HG_WIDTH), 0) % HG_BLOCK
        pw = 2 * HG_HD
        oi = lax.broadcasted_iota(jnp.int32, (pw, pw), 0) // HG_HD
        oj = lax.broadcasted_iota(jnp.int32, (pw, pw), 1) // HG_HD
        ones2 = jnp.where(oi == oj, 1.0, 0.0).astype(BF16)
        o_pair = [jnp.zeros((Lc, pw), F32) for _ in range(HG_HEADS // 2)]
        for d in range(HG_BLOCK):
            lo = HG_BLOCK - d
            kd = kp_ref[lo:lo + Lc, :]
            gd = gp_ref[lo:lo + Lc, :]
            vd = vp_ref[lo:lo + Lc, :]
            m = tpos >= d
            dec = jnp.exp(jnp.where(m, gb - gd, 0.0))
            pr = jnp.where(m, q * kd * dec, 0.0).astype(BF16)
            for hp in range(HG_HEADS // 2):
                ps = slice(hp * pw, (hp + 1) * pw)
                o_pair[hp] = o_pair[hp] + jnp.dot(pr[:, ps], ones2, preferred_element_type=F32) * vd[:, ps]
        for hp in range(HG_HEADS // 2):
            oi_ref[:, hp * pw:(hp + 1) * pw] = o_pair[hp]
        pass_state(HG_BLOCK, gb, gl, lambda hs, qg: oi_ref[:, hs])

    @pl.when(c == nc - 1)
    def _fin():
        for h in range(HG_HEADS):
            so_ref[0, h] = st_ref[h].T


def _hgrn(proj, p, lb, *, row0, B, T, Lc, s0=None, ycat=None):
    nc = T // Lc
    rb0 = row0 // Lc
    has_state = s0 is not None

    def rows(col):
        return lambda b, c: (rb0 + b * nc + c, col // HG_WIDTH)

    const = lambda b, c: (0, 0)
    args = [proj, proj, proj, proj, lb, p['hgrn_norm']]
    specs = [pl.BlockSpec((Lc, HG_WIDTH), rows(COL_HQ)), pl.BlockSpec((Lc, HG_WIDTH), rows(COL_HF)),
             pl.BlockSpec((Lc, HG_WIDTH), rows(COL_HI)), pl.BlockSpec((Lc, HG_WIDTH), rows(COL_HG)),
             pl.BlockSpec((1, HG_WIDTH), const), pl.BlockSpec((1, HG_WIDTH), const)]
    if has_state:
        args.append(s0)
        specs.append(pl.BlockSpec((1, HG_HEADS, HG_HD, HG_HD), lambda b, c: (b, 0, 0, 0)))
    yshape, aliases = _mixed_out(ycat, proj.shape[0], args, specs)
    y, so = pl.pallas_call(
        functools.partial(_hgrn_kernel, Lc=Lc, has_state=has_state, aliased=ycat is not None),
        grid=(B, nc),
        in_specs=specs,
        out_specs=[pl.BlockSpec((Lc, HG_WIDTH), rows(YCOL_HG)),
                   pl.BlockSpec((1, HG_HEADS, HG_HD, HG_HD), lambda b, c: (b, 0, 0, 0))],
        out_shape=[yshape, jax.ShapeDtypeStruct((B, HG_HEADS, HG_HD, HG_HD), F32)],
        scratch_shapes=[pltpu.VMEM((HG_BLOCK + Lc, HG_WIDTH), F32)] * 3
                       + [pltpu.VMEM((HG_HEADS, HG_HD, HG_HD), F32), pltpu.VMEM((Lc, HG_WIDTH), F32)],
        input_output_aliases=aliases,
        compiler_params=_cparams(("parallel", "arbitrary")),
        name="hgrn",
    )(*args)
    return y, so


def _swa_kernel(*refs, Tq, prompt, aliased):
    sink_ref, q_ref, k_ref, v_ref, qg_ref, kg_ref = refs[:6]
    rest = refs[6:]
    if not prompt:
        ck_ref, cv_ref = rest[:2]
        rest = rest[2:]
    if aliased:
        rest = rest[1:]
    y_ref, ko_ref, vo_ref, bias_ref = rest[:4]
    if prompt:
        kp_ref, vp_ref = rest[4:]
    c = pl.program_id(1)
    Tk = WINDOW + Tq
    bi = lax.broadcasted_iota(jnp.int32, (LANE, LANE), 0) // SWA_HD
    bj = lax.broadcasted_iota(jnp.int32, (LANE, LANE), 1) // SWA_HD
    bd = jnp.where(bi == bj, 1.0, 0.0).astype(BF16)

    def hnorm(x, gain):
        outs = []
        for j in range(x.shape[1] // LANE):
            xs = x[:, j * LANE:(j + 1) * LANE]
            ms = _split_dot(xs * xs, bd) * (1.0 / SWA_HD)
            outs.append(xs * lax.rsqrt(ms + EPS))
        return (jnp.concatenate(outs, axis=1) if len(outs) > 1 else outs[0]) * gain

    qn = hnorm(q_ref[...], qg_ref[...])
    kn = hnorm(k_ref[...], kg_ref[...])
    vn = v_ref[...]
    if prompt:
        @pl.when(c == 0)
        def _init():
            kp_ref[...] = jnp.zeros((WINDOW, LANE), F32)
            vp_ref[...] = jnp.zeros((WINDOW, LANE), F32)
        kprev = kp_ref[...]
        vprev = vp_ref[...]
    else:
        kprev = ck_ref[0]
        vprev = cv_ref[0]
    kall = jnp.concatenate([kprev, kn], axis=0)
    vall = jnp.concatenate([vprev, vn], axis=0)
    knew = kall[Tq:, :]
    vnew = vall[Tq:, :]
    ko_ref[0] = knew
    vo_ref[0] = vnew
    if prompt:
        kp_ref[...] = knew
        vp_ref[...] = vnew
    ksw = pltpu.roll(kall, SWA_HD, 1)
    vsw = pltpu.roll(vall, SWA_HD, 1)
    lowk = lax.broadcasted_iota(jnp.int32, (Tk, LANE), 1) < SWA_HD
    R = SWA_REP * Tq
    rows_i = lax.broadcasted_iota(jnp.int32, (R, Tk), 0)
    cols_i = lax.broadcasted_iota(jnp.int32, (R, Tk), 1)
    rep_i = rows_i // Tq
    rep_c = lax.broadcasted_iota(jnp.int32, (R, 1), 0) // Tq

    @pl.when(c == 0)
    def _bias():
        dist = jnp.abs(rows_i % Tq + WINDOW - cols_i).astype(F32)
        for g in range(SWA_KV_HEADS):
            slope = jnp.zeros((R, Tk), F32)
            for r in range(SWA_REP):
                slope = jnp.where(rep_i == r, 2.0 ** (-8.0 * (g * SWA_REP + r + 1) / SWA_HEADS), slope)
            bias_ref[g] = slope * dist

    valid = (cols_i + c * Tq) >= WINDOW
    lane = lax.broadcasted_iota(jnp.int32, (Tq, LANE), 1)
    low = lane < SWA_HD
    for g in range(SWA_KV_HEADS):
        parts = []
        for r in range(SWA_REP):
            hd = g * SWA_REP + r
            qp = qn[:, (hd // 2) * LANE:(hd // 2 + 1) * LANE]
            parts.append(jnp.where(low if hd % 2 == 0 else jnp.logical_not(low), qp, 0.0))
        qst = jnp.concatenate(parts, axis=0).astype(BF16)
        kboth = (jnp.where(lowk, kall, ksw) if g == 0 else jnp.where(lowk, ksw, kall)).astype(BF16)
        vboth = (jnp.where(lowk, vall, vsw) if g == 0 else jnp.where(lowk, vsw, vall)).astype(BF16)
        sink = jnp.zeros((R, 1), F32)
        for r in range(SWA_REP):
            sink = jnp.where(rep_c == r, sink_ref[g * SWA_REP + r], sink)
        s = lax.dot_general(qst, kboth, NT_DIMS, preferred_element_type=F32) * (SWA_HD ** -0.5)
        s = s - bias_ref[g]
        if prompt:
            s = jnp.where(valid, s, NEG)
        mx = jnp.maximum(jnp.max(s, axis=-1, keepdims=True), sink)
        pe = jnp.exp(s - mx)
        if prompt:
            pe = jnp.where(valid, pe, 0.0)
        inv = 1.0 / (jnp.sum(pe, axis=-1, keepdims=True) + jnp.exp(sink - mx))
        o = jnp.dot((pe * inv).astype(BF16), vboth, preferred_element_type=F32)
        for jj in range(SWA_REP // 2):
            pair = g * (SWA_REP // 2) + jj
            y_ref[:, pair * LANE:(pair + 1) * LANE] = jnp.where(
                low, o[(2 * jj) * Tq:(2 * jj + 1) * Tq, :], o[(2 * jj + 1) * Tq:(2 * jj + 2) * Tq, :]
            ).astype(y_ref.dtype)


def _swa(proj, p, *, row0, B, T, ck=None, cv=None, ycat=None):
    prompt = ck is None
    Tq = CHUNK if prompt else T
    nc = T // Tq
    rb0 = row0 // Tq

    def rows(col, width):
        return lambda b, c: (rb0 + b * nc + c, col // width)

    const = lambda b, c: (0, 0)
    args = [p['swa_sinks'], proj, proj, proj, p['swa_qnorm'], p['swa_knorm']]
    specs = [pl.BlockSpec(memory_space=pltpu.SMEM),
             pl.BlockSpec((Tq, 1024), rows(COL_SQ, 1024)),
             pl.BlockSpec((Tq, LANE), rows(COL_SK, LANE)),
             pl.BlockSpec((Tq, LANE), rows(COL_SV, LANE)),
             pl.BlockSpec((1, 1024), const), pl.BlockSpec((1, LANE), const)]
    if not prompt:
        args += [ck.reshape(B, WINDOW, LANE), cv.reshape(B, WINDOW, LANE)]
        specs += [pl.BlockSpec((1, WINDOW, LANE), lambda b, c: (b, 0, 0))] * 2
    yshape, aliases = _mixed_out(ycat, proj.shape[0], args, specs)
    y, ko, vo = pl.pallas_call(
        functools.partial(_swa_kernel, Tq=Tq, prompt=prompt, aliased=ycat is not None),
        grid=(B, nc),
        in_specs=specs,
        out_specs=[pl.BlockSpec((Tq, 1024), rows(YCOL_SWA, 1024)),
                   pl.BlockSpec((1, WINDOW, LANE), lambda b, c: (b, 0, 0)),
                   pl.BlockSpec((1, WINDOW, LANE), lambda b, c: (b, 0, 0))],
        out_shape=[yshape,
                   jax.ShapeDtypeStruct((B, WINDOW, LANE), F32),
                   jax.ShapeDtypeStruct((B, WINDOW, LANE), F32)],
        scratch_shapes=[pltpu.VMEM((SWA_KV_HEADS, SWA_REP * Tq, WINDOW + Tq), F32)]
                       + ([pltpu.VMEM((WINDOW, LANE), F32)] * 2 if prompt else []),
        input_output_aliases=aliases,
        compiler_params=_cparams(("parallel", "arbitrary")),
        name="swa",
    )(*args)
    shp = (B, WINDOW, SWA_KV_HEADS, SWA_HD)
    return y, ko.reshape(shp), vo.reshape(shp)


def _mem_kernel(q_ref, k_ref, v_ref, *rest):
    o_ref = rest[-1]
    for h in range(MEM_HEADS):
        hs = slice(h * MEM_HD, (h + 1) * MEM_HD)
        s = lax.dot_general(q_ref[:, hs].astype(BF16), k_ref[:, hs].astype(BF16), NT_DIMS,
                            preferred_element_type=F32) * (MEM_HD ** -0.5)
        mx = jnp.max(s, axis=-1, keepdims=True)
        pe = jnp.exp(s - mx)
        pe = pe / jnp.sum(pe, axis=-1, keepdims=True)
        o_ref[:, hs] = jnp.dot(pe.astype(BF16), v_ref[:, hs].astype(BF16),
                               preferred_element_type=F32).astype(o_ref.dtype)


def _mem_attend(qn, mk, mv, *, row0, B, T, tq, out=None):
    nt = T // tq
    rb0 = row0 // tq
    args = [qn, mk, mv]
    specs = [pl.BlockSpec((tq, MEM_WIDTH), lambda b, t: (rb0 + b * nt + t, 0)),
             pl.BlockSpec((MEM_TOKENS, MEM_WIDTH), lambda b, t: (b, 0)),
             pl.BlockSpec((MEM_TOKENS, MEM_WIDTH), lambda b, t: (b, 0))]
    aliases = {}
    if out is not None:
        args.append(out)
        specs.append(pl.BlockSpec(memory_space=pl.ANY))
        aliases = {3: 0}
    return pl.pallas_call(
        _mem_kernel,
        grid=(B, nt),
        in_specs=specs,
        out_specs=pl.BlockSpec((tq, MEM_WIDTH), lambda b, t: (rb0 + b * nt + t, 0)),
        out_shape=jax.ShapeDtypeStruct((qn.shape[0], MEM_WIDTH), BF16),
        input_output_aliases=aliases,
        compiler_params=_cparams(("parallel", "arbitrary")),
        name="mem_attend",
    )(*args)


def _route_kernel(pq_ref, keys_ref, s1_ref, e1_ref, s2_ref, e2_ref, tau_ref):
    half = PEER_QDIM // 2
    ninf = -jnp.inf
    K = PEER_TOPK

    def top_distinct(s):
        cur, vals = s, []
        for _ in range(K):
            mx = jnp.max(cur, axis=0, keepdims=True)
            vals.append(mx)
            cur = jnp.where(cur == mx, ninf, cur)
        return vals

    def pick_first_max(cur, rows):
        mx = jnp.max(cur, axis=0, keepdims=True)
        first = jnp.min(jnp.where(cur == mx, rows, float(cur.shape[0])), axis=0, keepdims=True)
        return mx, rows == first

    def top_ranked(s):
        rows = lax.broadcasted_iota(jnp.int32, s.shape, 0).astype(F32)
        cur, vals, rank = s, [], jnp.full(s.shape, float(K), F32)
        for it in range(K):
            mx, hit = pick_first_max(cur, rows)
            vals.append(mx)
            rank = jnp.where(hit, float(it), rank)
            cur = jnp.where(hit, ninf, cur)
        return vals, rank

    def count_ge(x, thr):
        return jnp.sum(jnp.where(x >= thr, 1.0, 0.0), axis=0, keepdims=True)

    def all_cands(v1, v2m):
        return jnp.concatenate([v1[r] + v2m for r in range(K)], axis=0)

    def finish(h, s1, s2, v1, v2m, cand, tau):
        m0 = v1[0] + v2m[0:1, :]
        zsum = jnp.sum(jnp.where(cand >= tau, jnp.exp(cand - m0), 0.0), axis=0, keepdims=True)
        s1_ref[h] = jnp.where(s1 >= v1[K - 1], s1, ninf)
        s2_ref[h] = jnp.where(s2 >= v2m[K - 1:K, :], s2, ninf)
        e1_ref[h] = jnp.exp(s1 - v1[0])
        e2_ref[h] = jnp.exp(s2 - v2m[0:1, :]) / zsum
        tau_ref[h:h + 1, :] = tau

    tm = pq_ref.shape[0]

    def scores(h):
        q1 = pq_ref[:, h * PEER_QDIM:h * PEER_QDIM + half]
        q2 = pq_ref[:, h * PEER_QDIM + half:(h + 1) * PEER_QDIM]
        return (lax.dot_general(keys_ref[h, 0], q1, NT_DIMS, precision=HIGHEST, preferred_element_type=F32),
                lax.dot_general(keys_ref[h, 1], q2, NT_DIMS, precision=HIGHEST, preferred_element_type=F32))

    any_tie = []
    for h in range(PEER_HEADS):
        s1, s2 = scores(h)
        v1 = top_distinct(s1)
        v2 = top_distinct(s2)
        v2m = jnp.concatenate(v2, axis=0)
        rowi = lax.broadcasted_iota(jnp.int32, (8, tm), 0)
        blocks = [v1[0] + v2m, v1[1] + v2m[0:8, :]]
        for r in range(2, 8):
            blocks.append(jnp.where(rowi < K // (r + 1), v1[r] + v2m[0:8, :], ninf))
        blocks.append(jnp.concatenate(v1[8:], axis=0) + v2m[0:1, :])
        cand = jnp.concatenate(blocks, axis=0)
        cur = cand
        for _ in range(K):
            tau = jnp.max(cur, axis=0, keepdims=True)
            cur = jnp.where(cur == tau, ninf, cur)
        full = all_cands(v1, v2m)
        tied = jnp.where(count_ge(full, tau) == K,
                         jnp.where(count_ge(s1, v1[K - 1]) == K,
                                   jnp.where(count_ge(s2, v2[K - 1]) == K, 0.0, 1.0), 1.0), 1.0)
        finish(h, s1, s2, v1, v2m, cand, tau)
        any_tie.append(jnp.max(tied))

    for h in range(PEER_HEADS):
        @pl.when(any_tie[h] > 0.0)
        def _with_ties():
            s1, s2 = scores(h)
            w1, rank1 = top_ranked(s1)
            w2, rank2 = top_ranked(s2)
            w2m = jnp.concatenate(w2, axis=0)
            allc = all_cands(w1, w2m)
            rows = lax.broadcasted_iota(jnp.int32, allc.shape, 0).astype(F32)
            cur, picked = allc, jnp.zeros(allc.shape, F32)
            for _ in range(K):
                _, hit = pick_first_max(cur, rows)
                picked = jnp.where(hit, 1.0, picked)
                cur = jnp.where(hit, ninf, cur)
            m0 = w1[0] + w2m[0:1, :]
            zsum = jnp.sum(picked * jnp.exp(allc - m0), axis=0, keepdims=True)
            lmap = jnp.zeros(s1.shape, F32)
            for r in range(K):
                length = jnp.sum(picked[r * K:(r + 1) * K, :], axis=0, keepdims=True)
                lmap = jnp.where(rank1 == float(r), length, lmap)
            s1_ref[h] = lmap
            s2_ref[h] = jnp.where(rank2 < float(K), -rank2, -1000.0)
            e1_ref[h] = jnp.exp(s1 - w1[0])
            e2_ref[h] = jnp.exp(s2 - w2m[0:1, :]) / zsum
            tau_ref[h:h + 1, :] = jnp.ones((1, tm), F32)


def _peer_route(pq, keys, *, tm=256):
    N = pq.shape[0]
    big = jax.ShapeDtypeStruct((PEER_HEADS, N_KEYS, N), F32)
    bspec = pl.BlockSpec((PEER_HEADS, N_KEYS, tm), lambda i: (0, 0, i))
    return pl.pallas_call(
        _route_kernel,
        grid=(N // tm,),
        in_specs=[pl.BlockSpec((tm, PEER_HEADS * PEER_QDIM), lambda i: (i, 0)),
                  pl.BlockSpec((PEER_HEADS, 2, N_KEYS, PEER_QDIM // 2), lambda i: (0, 0, 0, 0))],
        out_specs=[bspec, bspec, bspec, bspec, pl.BlockSpec((PEER_HEADS, tm), lambda i: (0, i))],
        out_shape=[big, big, big, big, jax.ShapeDtypeStruct((PEER_HEADS, N), F32)],
        compiler_params=_cparams(("parallel",)),
        name="peer_route",
    )(pq, keys)


def _dense_kernel(h_hbm, g_ref, u_ref, v_ref, s1_ref, e1_ref, s2_ref, e2_ref, tau_ref, o_ref, xn_ref, c_ref,
                  sem, *, te):
    i = pl.program_id(0)
    j = pl.program_id(1)
    na = te // N_KEYS
    tm = o_ref.shape[0]

    @pl.when(j == 0)
    def _init():
        fetch = pltpu.make_async_copy(h_hbm.at[pl.ds(i * tm, tm), :], o_ref, sem)
        fetch.start()
        fetch.wait()
        x = o_ref[...]
        ms = jnp.mean(x * x, axis=-1, keepdims=True)
        xn_ref[...] = (x * lax.rsqrt(ms + EPS) * g_ref[...]).astype(BF16)

    pre = lax.dot_general(xn_ref[...], u_ref[...], NT_DIMS, preferred_element_type=F32)
    act = 0.5 * pre * (1.0 + lax.erf(pre * (2.0 ** -0.5)))
    for al in range(na):
        a = j * na + al
        gt = jnp.zeros((N_KEYS, tm), F32)
        for h in range(PEER_HEADS):
            hit = (s1_ref[h, pl.ds(a, 1), :] + s2_ref[h]) >= tau_ref[h:h + 1, :]
            gt = gt + jnp.where(hit, e1_ref[h, pl.ds(a, 1), :] * e2_ref[h], 0.0)
        c_ref[:, al * N_KEYS:(al + 1) * N_KEYS] = (gt.T * act[:, al * N_KEYS:(al + 1) * N_KEYS]).astype(BF16)
    o_ref[...] += jnp.dot(c_ref[...], v_ref[...], preferred_element_type=F32)


def _peer_dense(h, gain, u, v, layer, s1, e1, s2, e2, tau, *, tm=768, te=512):
    N, D = h.shape
    nj = N_EXPERTS // te
    assert N % tm == 0 and N_EXPERTS % te == 0
    one = pl.Buffered(1)
    route = pl.BlockSpec((PEER_HEADS, N_KEYS, tm), lambda i, j: (0, 0, i), pipeline_mode=one)
    return pl.pallas_call(
        functools.partial(_dense_kernel, te=te),
        grid=(N // tm, nj),
        in_specs=[pl.BlockSpec(memory_space=pl.ANY),
                  pl.BlockSpec((1, D), lambda i, j: (0, 0)),
                  pl.BlockSpec((None, te, D), lambda i, j: (layer, j, 0)),
                  pl.BlockSpec((None, te, D), lambda i, j: (layer, j, 0)),
                  route, route, route, route,
                  pl.BlockSpec((PEER_HEADS, tm), lambda i, j: (0, i))],
        out_specs=pl.BlockSpec((tm, D), lambda i, j: (i, 0), pipeline_mode=one),
        out_shape=jax.ShapeDtypeStruct((N, D), F32),
        scratch_shapes=[pltpu.VMEM((tm, D), BF16), pltpu.VMEM((tm, te), BF16), pltpu.SemaphoreType.DMA(())],
        compiler_params=_cparams(("parallel", "arbitrary")),
        name="peer_dense",
    )(h, gain.reshape(1, D).astype(F32), u, v, s1, e1, s2, e2, tau)


def _row(x, width=None):
    x = x.astype(F32).reshape(1, -1)
    if width is not None and x.shape[1] < width:
        x = jnp.pad(x, ((0, 0), (0, width - x.shape[1])))
    return x


def _cast_kernel(x_ref, o_ref):
    o_ref[...] = x_ref[...].astype(o_ref.dtype)


def _to_bf16(w, *, block_bytes=8 * 1024 * 1024):
    depth, rows, cols = w.shape
    tr = max(8, min(rows, block_bytes // (cols * 4)))
    assert rows % tr == 0
    return pl.pallas_call(
        _cast_kernel,
        grid=(depth, rows // tr),
        in_specs=[pl.BlockSpec((None, tr, cols), lambda l, r: (l, r, 0))],
        out_specs=pl.BlockSpec((None, tr, cols), lambda l, r: (l, r, 0)),
        out_shape=jax.ShapeDtypeStruct(w.shape, BF16),
        compiler_params=_cparams(("parallel", "parallel")),
        name="to_bf16",
    )(w)


PACK_ROWS = 256
PACK_GROUP = SSD_HEADS


def _pack_kernel(w_ref, dt_ref, o_ref):
    r = pl.program_id(1)
    last = pl.num_programs(1) - 1

    @pl.when(r < last)
    def _copy():
        o_ref[...] = w_ref[...].reshape(PACK_ROWS, w_ref.shape[-1]).astype(BF16)

    @pl.when(r == last)
    def _dt_and_pad():
        o_ref[0:PACK_GROUP, :] = dt_ref[...].astype(BF16)
        o_ref[PACK_GROUP:, :] = jnp.zeros((PACK_ROWS - PACK_GROUP, o_ref.shape[-1]), BF16)


def _pack_w_in(w_in):
    depth, d, cols = w_in.shape
    xbc_end = 3 * SSD_INNER
    assert cols == COL_DT + SSD_HEADS and cols % PACK_GROUP == 0 and PROJ_COLS % PACK_ROWS == 0
    gpb = PACK_ROWS // PACK_GROUP
    n_groups = cols // PACK_GROUP
    n_head = xbc_end // PACK_ROWS
    skip = (xbc_end + SSD_HEADS) // PACK_GROUP
    wt = jnp.swapaxes(w_in, 1, 2).reshape(depth, n_groups, PACK_GROUP, d)

    def src_group(l, r):
        g = jnp.where(r < n_head, r * gpb, skip + (r - n_head) * gpb)
        return (l, jnp.minimum(g, n_groups - gpb), 0, 0)

    return pl.pallas_call(
        _pack_kernel,
        grid=(depth, PROJ_COLS // PACK_ROWS),
        in_specs=[pl.BlockSpec((pl.Element(1), pl.Element(gpb), pl.Element(PACK_GROUP), pl.Element(d)),
                               src_group),
                  pl.BlockSpec((None, None, PACK_GROUP, d), lambda l, r: (l, xbc_end // PACK_GROUP, 0, 0))],
        out_specs=pl.BlockSpec((None, PACK_ROWS, d), lambda l, r: (l, r, 0)),
        out_shape=jax.ShapeDtypeStruct((depth, PROJ_COLS, d), BF16),
        compiler_params=_cparams(("parallel", "arbitrary")),
        name="pack_w_in",
    )(wt, wt)


def _layer_params(l, conv_w, conv_b, dt_bias, a_log, d_skip, ssd_norm, hgrn_norm, swa_qnorm,
                  swa_knorm, swa_sinks):
    eh = lax.broadcasted_iota(jnp.int32, (LANE, SSD_INNER), 0)
    ec = lax.broadcasted_iota(jnp.int32, (LANE, SSD_INNER), 1)
    return {
        'expand': (ec // SSD_HD == eh).astype(BF16),
        'cwx': conv_w[l][:, :SSD_INNER].astype(F32), 'cwb': conv_w[l][:, SSD_INNER:].astype(F32),
        'cbx': _row(conv_b[l][:SSD_INNER]), 'cbb': _row(conv_b[l][SSD_INNER:]),
        'dtb': _row(dt_bias[l], LANE), 'alog': _row(a_log[l], LANE),
        'dsk': _row(jnp.repeat(d_skip[l], SSD_HD)), 'ssd_norm': _row(ssd_norm[l]),
        'hgrn_norm': _row(jnp.tile(hgrn_norm[l], HG_HEADS)),
        'swa_qnorm': _row(jnp.tile(swa_qnorm[l], SWA_HEADS)),
        'swa_knorm': _row(jnp.tile(swa_knorm[l], SWA_KV_HEADS)),
        'swa_sinks': swa_sinks[l].astype(F32),
    }


def kernel(x_prompt, x_sample, mem_prompt, state_ssm, state_ssd_conv, state_hgrn, cache_swa_k, cache_swa_v,
           cache_mem_k, cache_mem_v, norm_mix, w_in, conv_w, conv_b, dt_bias, a_log, d_skip, ssd_norm,
           hgrn_lb, hgrn_norm, swa_qnorm, swa_knorm, swa_sinks, w_out, norm_mem, norm_memtok, w_mq, w_mk,
           w_mv, mem_qnorm, mem_knorm, w_mo, norm_ffn, w_pq, peer_keys, peer_u, peer_v):
    depth = w_in.shape[0]
    bp, tp, _ = x_prompt.shape
    bs, ts, _ = x_sample.shape
    n_p, n_s = bp * tp, bs * ts
    lbp = jax.nn.softmax(hgrn_lb.astype(F32), axis=0)
    lower = jnp.cumsum(lbp, axis=0) - lbp[0:1]
    h = jnp.concatenate([x_prompt.reshape(n_p, D_MODEL), x_sample.reshape(n_s, D_MODEL)], axis=0)
    mem2d = mem_prompt.reshape(bp * MEM_TOKENS, D_MODEL)
    outs = {k: [] for k in ('ssm_p', 'conv_p', 'hg_p', 'swk_p', 'swv_p', 'mk_p', 'mv_p',
                            'ssm_s', 'conv_s', 'hg_s', 'swk_s', 'swv_s')}
    w_in_b = _pack_w_in(w_in)
    w_out_b, w_mq_b, w_mk_b, w_mv_b, w_mo_b, w_pq_b, u_b, v_b = (
        _to_bf16(t) for t in (w_out, w_mq, w_mk, w_mv, w_mo, w_pq, peer_u, peer_v))
    for l in range(depth):
        p = _layer_params(l, conv_w, conv_b, dt_bias, a_log, d_skip, ssd_norm, hgrn_norm,
                          swa_qnorm, swa_knorm, swa_sinks)
        lb = _row(lower[l])
        proj = _mm(h, w_in_b, layer=l, w_nk=True, gain=norm_mix[l], name="proj_in")
        ycat = jnp.zeros((n_p + n_s, D_MODEL), BF16)
        ycat, conv_p, ssm_p = _ssd(proj, p, row0=0, B=bp, T=tp, ycat=ycat)
        ycat, conv_s, ssm_s = _ssd(proj, p, row0=n_p, B=bs, T=ts, conv0=state_ssd_conv[l],
                                   h0=state_ssm[l], ycat=ycat)
        ycat, hg_p = _hgrn(proj, p, lb, row0=0, B=bp, T=tp, Lc=128, ycat=ycat)
        ycat, hg_s = _hgrn(proj, p, lb, row0=n_p, B=bs, T=ts, Lc=ts, s0=state_hgrn[l], ycat=ycat)
        ycat, swk_p, swv_p = _swa(proj, p, row0=0, B=bp, T=tp, ycat=ycat)
        ycat, swk_s, swv_s = _swa(proj, p, row0=n_p, B=bs, T=ts, ck=cache_swa_k[l], cv=cache_swa_v[l],
                                  ycat=ycat)
        h = _mm(ycat, w_out_b, layer=l, res=h, name="proj_out")
        mk = _mm(mem2d, w_mk_b, layer=l, gain=norm_memtok[l], head_gain=mem_knorm[l], name="mem_k")
        mv = _mm(mem2d, w_mv_b, layer=l, gain=norm_memtok[l], name="mem_v")
        qn = _mm(h, w_mq_b, layer=l, gain=norm_mem[l], head_gain=mem_qnorm[l], name="mem_q")
        om = _mem_attend(qn, mk, mv, row0=0, B=bp, T=tp, tq=512,
                         out=jnp.zeros((n_p + n_s, MEM_WIDTH), BF16))
        om = _mem_attend(qn, cache_mem_k[l].reshape(bs * MEM_TOKENS, MEM_WIDTH),
                         cache_mem_v[l].reshape(bs * MEM_TOKENS, MEM_WIDTH), row0=n_p, B=bs, T=ts, tq=ts,
                         out=om)
        h = _mm(om, w_mo_b, layer=l, res=h, name="mem_o")
        pq = _mm(h, w_pq_b, layer=l, gain=norm_ffn[l], name="peer_q")
        s1, e1, s2, e2, tau = _peer_route(pq, peer_keys[l].astype(F32))
        h = _peer_dense(h, norm_ffn[l], u_b, v_b, l, s1, e1, s2, e2, tau)
        for name, val in (('ssm_p', ssm_p), ('conv_p', conv_p), ('hg_p', hg_p), ('swk_p', swk_p),
                          ('swv_p', swv_p), ('ssm_s', ssm_s), ('conv_s', conv_s), ('hg_s', hg_s),
                          ('swk_s', swk_s), ('swv_s', swv_s)):
            outs[name].append(val)
        outs['mk_p'].append(mk.reshape(bp, MEM_TOKENS, MEM_HEADS, MEM_HD))
        outs['mv_p'].append(mv.reshape(bp, MEM_TOKENS, MEM_HEADS, MEM_HD))
    st = lambda k: jnp.stack(outs[k])
    return (h[:n_p].reshape(bp, tp, D_MODEL), h[n_p:].reshape(bs, ts, D_MODEL),
            st('ssm_p'), st('conv_p'), st('hg_p'), st('swk_p'), st('swv_p'), st('mk_p'), st('mv_p'),
            st('ssm_s'), st('conv_s'), st('hg_s'), st('swk_s'), st('swv_s'))
```

```python
import functools
import math

import jax
import jax.numpy as jnp
from jax import lax
from jax.experimental import pallas as pl
from jax.experimental.pallas import tpu as pltpu

F32 = jnp.float32
BF16 = jnp.bfloat16
HIGHEST = lax.Precision.HIGHEST

D_MODEL = 4096
PAST_LEN = 2048
CHUNK = 64
SSD_INNER = 2048
SSD_HD = 64
SSD_HEADS = 32
SSD_GROUPS = 8
SSD_STATE = 128
SSD_GW = SSD_INNER // SSD_GROUPS
SSD_REP = SSD_HEADS // SSD_GROUPS
HG_WIDTH = 1024
HG_HD = 128
HG_HEADS = 8
HG_BLOCK = 16
HG_LONG_BLOCK = 64
HG_SAFE_DECAY = 80.0
SWA_HD = 64
SWA_HEADS = 16
SWA_KV_HEADS = 2
SWA_REP = SWA_HEADS // SWA_KV_HEADS
WINDOW = 128
MEM_TOKENS = 256
MEM_HEADS = 4
MEM_HD = 128
MEM_WIDTH = 512
N_KEYS = 128
N_EXPERTS = N_KEYS * N_KEYS
PEER_HEADS = 8
PEER_QDIM = 256
PEER_TOPK = 16
EPS = 1e-6
NEG = -1e30
TINY = 1e-30
LANE = 128
MXU_N = 256

COL_Z, COL_XS, COL_BC = 0, 2048, 4096
COL_HQ, COL_HF, COL_HI, COL_HG = 6144, 7168, 8192, 9216
COL_SQ, COL_SK, COL_SV, COL_DT = 10240, 11264, 11392, 11520
PROJ_COLS = 11776
YCOL_SSD, YCOL_HG, YCOL_SWA = 0, 2048, 3072

VMEM_LIMIT = 60 * 1024 * 1024

NT_DIMS = (((1,), (1,)), ((), ()))
TN_DIMS = (((0,), (0,)), ((), ()))


def _cparams(sem):
    return pltpu.CompilerParams(dimension_semantics=sem, vmem_limit_bytes=VMEM_LIMIT)


def _sigmoid(x):
    return 1.0 / (1.0 + jnp.exp(-x))


def _silu(x):
    return x * _sigmoid(x)


def _split_dot(x, ones_bf16):
    hi = x.astype(BF16)
    lo = (x - hi.astype(F32)).astype(BF16)
    return (jnp.dot(hi, ones_bf16, preferred_element_type=F32)
            + jnp.dot(lo, ones_bf16, preferred_element_type=F32))


def _mm_kernel(*refs, norm, head_norm, residual, w_nk, n_ride):
    it = iter(refs)
    x_ref = next(it)
    g_ref = next(it) if norm else None
    w_ref = next(it)
    hg_ref = next(it) if head_norm else None
    r_ref = next(it) if residual else None
    ride_in = [next(it) for _ in range(n_ride)]
    o_ref = next(it)
    ride_out = [next(it) for _ in range(n_ride)]
    xn_ref = next(it) if norm else None
    for src, dst in zip(ride_in, ride_out):
        dst[...] = src[...].astype(dst.dtype)
    if norm:
        @pl.when(pl.program_id(1) == 0)
        def _():
            x = x_ref[...].astype(F32)
            ms = jnp.mean(x * x, axis=-1, keepdims=True)
            xn_ref[...] = (x * lax.rsqrt(ms + EPS) * g_ref[...]).astype(BF16)
        xb = xn_ref[...]
    else:
        xb = x_ref[...]
    if w_nk:
        acc = lax.dot_general(xb, w_ref[...], NT_DIMS, preferred_element_type=F32)
    else:
        acc = jnp.dot(xb, w_ref[...], preferred_element_type=F32)
    if head_norm:
        parts = []
        for c in range(acc.shape[1] // LANE):
            a = acc[:, c * LANE:(c + 1) * LANE]
            ms = jnp.mean(a * a, axis=-1, keepdims=True)
            parts.append(a * lax.rsqrt(ms + EPS))
        acc = jnp.concatenate(parts, axis=1) * hg_ref[...]
    if residual:
        acc = acc + r_ref[...]
    o_ref[...] = acc.astype(o_ref.dtype)


def _mm(x, w, *, layer=None, w_nk=False, gain=None, head_gain=None, res=None, out_dtype=F32, tn=512,
        ride=(), name="mm"):
    M, K = x.shape
    N = w.shape[-2] if w_nk else w.shape[-1]
    norm, head_norm, residual = gain is not None, head_gain is not None, res is not None
    tm = next(t for t in ((768, 512, 256, 128) if norm else (1024, 512, 256, 128)) if M % t == 0)
    tn = min(tn, N)
    assert N % tn == 0
    args, specs = [x], [pl.BlockSpec((tm, K), lambda i, j: (i, 0))]
    if norm:
        args.append(gain.reshape(1, K).astype(F32))
        specs.append(pl.BlockSpec((1, K), lambda i, j: (0, 0)))
    args.append(w)
    if w_nk:
        specs.append(pl.BlockSpec((None, tn, K), lambda i, j: (layer, j, 0)))
    elif layer is None:
        specs.append(pl.BlockSpec((K, tn), lambda i, j: (0, j)))
    else:
        specs.append(pl.BlockSpec((None, K, tn), lambda i, j: (layer, 0, j)))
    if head_norm:
        args.append(jnp.tile(head_gain.astype(F32), N // head_gain.shape[0]).reshape(1, N))
        specs.append(pl.BlockSpec((1, tn), lambda i, j: (0, j)))
    if residual:
        args.append(res)
        specs.append(pl.BlockSpec((tm, tn), lambda i, j: (i, j)))
    grid = (M // tm, N // tn)
    out_specs = [pl.BlockSpec((tm, tn), lambda i, j: (i, j))]
    out_shape = [jax.ShapeDtypeStruct((M, N), out_dtype)]
    for r in ride:
        _, rows, cols = r.shape
        rb = next(b for b in (8, 16, 32, 64, 128, 256, 512) if rows % b == 0 and rows // b <= grid[0] * grid[1])
        last = rows // rb - 1
        args.append(r)
        specs.append(pl.BlockSpec(
            (None, rb, cols), lambda i, j, last=last: (layer, jnp.minimum(i * grid[1] + j, last), 0)))
        out_specs.append(pl.BlockSpec(
            (rb, cols), lambda i, j, last=last: (jnp.minimum(i * grid[1] + j, last), 0)))
        out_shape.append(jax.ShapeDtypeStruct((rows, cols), BF16))
    outs = pl.pallas_call(
        functools.partial(_mm_kernel, norm=norm, head_norm=head_norm, residual=residual, w_nk=w_nk,
                          n_ride=len(ride)),
        grid=grid,
        in_specs=specs,
        out_specs=out_specs,
        out_shape=out_shape,
        scratch_shapes=[pltpu.VMEM((tm, K), BF16)] if norm else [],
        compiler_params=_cparams(("arbitrary", "arbitrary") if ride else ("parallel", "arbitrary")),
        name=name,
    )(*args)
    return outs if ride else outs[0]


def _ssd_kernel(*refs, L, has_state, aliased):
    (z_ref, xs_ref, bc_ref, dt_ref, cwx_ref, cwb_ref, cbx_ref, cbb_ref, dtb_ref, alog_ref,
     dsk_ref, gn_ref, exp_ref) = refs[:13]
    rest = refs[13:]
    if has_state:
        conv0_ref, h0_ref = rest[:2]
        rest = rest[2:]
    if aliased:
        rest = rest[1:]
    y_ref, convo_ref, ho_ref, xpx_ref, xpb_ref, hT_ref = rest
    c = pl.program_id(1)
    nc = pl.num_programs(1)

    @pl.when(c == 0)
    def _init():
        if has_state:
            xpx_ref[0:8, :] = jnp.zeros((8, SSD_INNER), F32)
            xpb_ref[0:8, :] = jnp.zeros((8, SSD_INNER), F32)
            xpx_ref[5:8, :] = conv0_ref[0, :, 0:SSD_INNER]
            xpb_ref[5:8, :] = conv0_ref[0, :, SSD_INNER:2 * SSD_INNER]
            for g in range(SSD_GROUPS):
                hT_ref[g] = h0_ref[0, g * SSD_GW:(g + 1) * SSD_GW, :].T
        else:
            xpx_ref[0:8, :] = jnp.zeros((8, SSD_INNER), F32)
            xpb_ref[0:8, :] = jnp.zeros((8, SSD_INNER), F32)
            hT_ref[...] = jnp.zeros(hT_ref.shape, F32)

    xpx_ref[8:8 + L, :] = xs_ref[...]
    xpb_ref[8:8 + L, :] = bc_ref[...]

    def conv(xp_ref, w_ref, b_ref):
        acc = b_ref[...] + w_ref[3:4, :] * xp_ref[8:8 + L, :]
        for j in range(3):
            acc = acc + w_ref[j:j + 1, :] * xp_ref[5 + j:5 + j + L, :]
        return acc

    xc = _silu(conv(xpx_ref, cwx_ref, cbx_ref))
    bcc = _silu(conv(xpb_ref, cwb_ref, cbb_ref))
    tail_x = xpx_ref[5 + L:8 + L, :]
    tail_b = xpb_ref[5 + L:8 + L, :]
    xpx_ref[5:8, :] = tail_x
    xpb_ref[5:8, :] = tail_b
    convo_ref[0, :, 0:SSD_INNER] = tail_x
    convo_ref[0, :, SSD_INNER:2 * SSD_INNER] = tail_b

    dtr = dt_ref[...] + dtb_ref[...]
    dt = jnp.maximum(dtr, 0.0) + jnp.log(1.0 + jnp.exp(-jnp.abs(dtr)))
    a = -jnp.exp(alog_ref[...])
    dta = dt * a
    ri = lax.broadcasted_iota(jnp.int32, (L, L), 0)
    ci = lax.broadcasted_iota(jnp.int32, (L, L), 1)
    causal = ri >= ci
    acum = jnp.dot(causal.astype(F32), dta, precision=HIGHEST, preferred_element_type=F32)
    both = jnp.concatenate([acum, dt], axis=0)
    p0 = both.astype(BF16)
    r1 = both - p0.astype(F32)
    p1 = r1.astype(BF16)
    p2 = (r1 - p1.astype(F32)).astype(BF16)
    expand = exp_ref[...]
    bothx = ((jnp.dot(p0, expand, preferred_element_type=F32)
              + jnp.dot(p1, expand, preferred_element_type=F32))
             + jnp.dot(p2, expand, preferred_element_type=F32))
    acx = bothx[:L, :]
    dtx = bothx[L:, :]
    eax = jnp.exp(acx)
    lastx = acx[L - 1:L, :]
    tailw = jnp.exp(lastx - acx) * dtx
    cdx = jnp.exp(lastx)
    acT = acum.T
    dtT = dt.T
    lane_head = lax.broadcasted_iota(jnp.int32, (L, SSD_GW), 1) // SSD_HD
    z = z_ref[...]

    for g in range(SSD_GROUPS):
        gs = slice(g * SSD_GW, (g + 1) * SSD_GW)
        Bg = bcc[:, g * SSD_STATE:(g + 1) * SSD_STATE].astype(BF16)
        Cg = bcc[:, SSD_GROUPS * SSD_STATE + g * SSD_STATE:
                 SSD_GROUPS * SSD_STATE + (g + 1) * SSD_STATE].astype(BF16)
        cb = lax.dot_general(Cg, Bg, NT_DIMS, preferred_element_type=F32)
        ws = []
        for r in range(SSD_REP):
            hd = g * SSD_REP + r
            seg = acum[:, hd:hd + 1] - acT[hd:hd + 1, :]
            dec = jnp.where(causal, jnp.exp(jnp.where(causal, seg, 0.0)), 0.0)
            ws.append(cb * dec * dtT[hd:hd + 1, :])
        wst = jnp.concatenate(ws, axis=0).astype(BF16)
        xg = xc[:, gs]
        full = jnp.dot(wst, xg.astype(BF16), preferred_element_type=F32)
        y_intra = jnp.zeros((L, SSD_GW), F32)
        for r in range(SSD_REP):
            y_intra = y_intra + jnp.where(lane_head == r, full[r * L:(r + 1) * L, :], 0.0)
        hTg = hT_ref[g]
        y_inter = jnp.dot(Cg, hTg.astype(BF16), preferred_element_type=F32) * eax[:, gs]
        xt = (xg * tailw[:, gs]).astype(BF16)
        hT_ref[g] = hTg * cdx[:, gs] + lax.dot_general(Bg, xt, TN_DIMS, preferred_element_type=F32)
        yg = y_intra + y_inter + dsk_ref[:, gs] * xg
        u = yg * _silu(z[:, gs])
        ms = jnp.mean(u * u, axis=-1, keepdims=True)
        y_ref[:, gs] = (u * lax.rsqrt(ms + EPS) * gn_ref[:, gs]).astype(y_ref.dtype)

    @pl.when(c == nc - 1)
    def _fin():
        for g in range(SSD_GROUPS):
            ho_ref[0, g * SSD_GW:(g + 1) * SSD_GW, :] = hT_ref[g].T


def _mixed_out(ycat, n_total, args, specs):
    aliases = {}
    if ycat is not None:
        args.append(ycat)
        specs.append(pl.BlockSpec(memory_space=pl.ANY))
        aliases = {len(args) - 1: 0}
    return jax.ShapeDtypeStruct((n_total, D_MODEL), BF16), aliases


def _ssd(proj, p, *, row0, B, T, conv0=None, h0=None, layer=None, ycat=None):
    L = math.gcd(T, CHUNK)
    nc = T // L
    rb0 = row0 // L
    has_state = conv0 is not None

    def rows(col):
        return lambda b, c: (rb0 + b * nc + c, col)

    const = lambda b, c: (0, 0)
    args = [proj, proj, proj, proj, p['cwx'], p['cwb'], p['cbx'], p['cbb'], p['dtb'], p['alog'],
            p['dsk'], p['ssd_norm'], p['expand']]
    specs = [pl.BlockSpec((L, SSD_INNER), rows(COL_Z // SSD_INNER)),
             pl.BlockSpec((L, SSD_INNER), rows(COL_XS // SSD_INNER)),
             pl.BlockSpec((L, SSD_INNER), rows(COL_BC // SSD_INNER)),
             pl.BlockSpec((L, LANE), rows(COL_DT // LANE)),
             pl.BlockSpec((4, SSD_INNER), const), pl.BlockSpec((4, SSD_INNER), const),
             pl.BlockSpec((1, SSD_INNER), const), pl.BlockSpec((1, SSD_INNER), const),
             pl.BlockSpec((1, LANE), const), pl.BlockSpec((1, LANE), const),
             pl.BlockSpec((1, SSD_INNER), const), pl.BlockSpec((1, SSD_INNER), const),
             pl.BlockSpec((LANE, SSD_INNER), const)]
    if has_state:
        args += [conv0, h0.reshape(h0.shape[0], B, SSD_INNER, SSD_STATE)]
        specs += [pl.BlockSpec((None, 1, 3, 2 * SSD_INNER), lambda b, c: (layer, b, 0, 0)),
                  pl.BlockSpec((None, 1, SSD_INNER, SSD_STATE), lambda b, c: (layer, b, 0, 0))]
    yshape, aliases = _mixed_out(ycat, proj.shape[0], args, specs)
    y, convo, ho = pl.pallas_call(
        functools.partial(_ssd_kernel, L=L, has_state=has_state, aliased=ycat is not None),
        grid=(B, nc),
        in_specs=specs,
        out_specs=[pl.BlockSpec((L, SSD_INNER), rows(0)),
                   pl.BlockSpec((1, 3, 2 * SSD_INNER), lambda b, c: (b, 0, 0)),
                   pl.BlockSpec((1, SSD_INNER, SSD_STATE), lambda b, c: (b, 0, 0))],
        out_shape=[yshape,
                   jax.ShapeDtypeStruct((B, 3, 2 * SSD_INNER), F32),
                   jax.ShapeDtypeStruct((B, SSD_INNER, SSD_STATE), F32)],
        scratch_shapes=[pltpu.VMEM((8 + L, SSD_INNER), F32), pltpu.VMEM((8 + L, SSD_INNER), F32),
                        pltpu.VMEM((SSD_GROUPS, SSD_STATE, SSD_GW), F32)],
        input_output_aliases=aliases,
        compiler_params=_cparams(("parallel", "arbitrary")),
        name="ssd",
    )(*args)
    return y, convo, ho.reshape(B, SSD_HEADS, SSD_HD, SSD_STATE)


def _hgrn_kernel(*refs, Lc, has_state, aliased):
    q_ref, f_ref, i_ref, g_ref, lb_ref, gn_ref = refs[:6]
    rest = refs[6:]
    if has_state:
        s0_ref = rest[0]
        rest = rest[1:]
    if aliased:
        rest = rest[1:]
    y_ref, so_ref, kp_ref, gp_ref, vp_ref, st_ref, oi_ref = rest
    c = pl.program_id(1)
    nc = pl.num_programs(1)
    nb = Lc // HG_BLOCK

    @pl.when(c == 0)
    def _init():
        for h in range(HG_HEADS):
            st_ref[h] = s0_ref[0, h].T if has_state else jnp.zeros((HG_HD, HG_HD), F32)
        kp_ref[0:HG_BLOCK, :] = jnp.zeros((HG_BLOCK, HG_WIDTH), F32)
        gp_ref[0:HG_BLOCK, :] = jnp.zeros((HG_BLOCK, HG_WIDTH), F32)
        vp_ref[0:HG_BLOCK, :] = jnp.zeros((HG_BLOCK, HG_WIDTH), F32)

    zf = f_ref[...]
    lb = lb_ref[...]
    f = lb + (1.0 - lb) * _sigmoid(zf)
    logf = jnp.log(jnp.maximum(f, TINY))
    k = (1.0 - lb) * _sigmoid(-zf)
    q = _silu(q_ref[...]) * (HG_HD ** -0.5)
    v = i_ref[...]
    vb = v.astype(BF16)
    gate = _silu(g_ref[...])
    ri = lax.broadcasted_iota(jnp.int32, (Lc, Lc), 0)
    ci = lax.broadcasted_iota(jnp.int32, (Lc, Lc), 1)

    def log_decays(blk):
        same = (ri // blk) == (ci // blk)
        ltri = jnp.where(same, jnp.where(ri >= ci, 1.0, 0.0), 0.0)
        lall = jnp.where(same, 1.0, 0.0)
        return (ltri, jnp.dot(ltri, logf, precision=HIGHEST, preferred_element_type=F32),
                jnp.dot(lall, logf, precision=HIGHEST, preferred_element_type=F32))

    def pass_state(blk, g_cum, g_tot, intra):
        qg = (q * jnp.exp(g_cum)).astype(BF16)
        kdec = (k * jnp.exp(g_tot - g_cum)).astype(BF16)
        cd = jnp.exp(g_tot)
        for h in range(HG_HEADS):
            hs = slice(h * HG_HD, (h + 1) * HG_HD)
            st = st_ref[h]
            outs = []
            for b in range(Lc // blk):
                rs = slice(b * blk, (b + 1) * blk)
                outs.append(lax.dot_general(qg[rs, hs], st.astype(BF16), NT_DIMS, preferred_element_type=F32))
                st = st * cd[b * blk:b * blk + 1, hs] + lax.dot_general(
                    vb[rs, hs], kdec[rs, hs], TN_DIMS, preferred_element_type=F32)
            st_ref[h] = st
            o = intra(hs, qg) + (jnp.concatenate(outs, axis=0) if len(outs) > 1 else outs[0])
            ms = jnp.mean(o * o, axis=-1, keepdims=True)
            y_ref[:, hs] = (o * lax.rsqrt(ms + EPS) * gn_ref[:, hs] * gate[:, hs]).astype(y_ref.dtype)

    blk_long = min(HG_LONG_BLOCK, Lc)
    ltri_l, g_l, gt_l = log_decays(blk_long)
    worst = jnp.max(-gt_l)

    @pl.when(worst < HG_SAFE_DECAY)
    def _factorised():
        kgrow = (k * jnp.exp(-g_l)).astype(BF16)
        in_block = ltri_l > 0.5

        def intra(hs, qg):
            att = lax.dot_general(qg[:, hs], kgrow[:, hs], NT_DIMS, preferred_element_type=F32)
            att = jnp.where(in_block, att, 0.0).astype(BF16)
            return jnp.dot(att, vb[:, hs], preferred_element_type=F32)

        pass_state(blk_long, g_l, gt_l, intra)

    @pl.when(worst >= HG_SAFE_DECAY)
    def _unfactorised():
        _, gb, gl = log_decays(HG_BLOCK)
        kp_ref[HG_BLOCK:HG_BLOCK + Lc, :] = k
        gp_ref[HG_BLOCK:HG_BLOCK + Lc, :] = gb
        vp_ref[HG_BLOCK:HG_BLOCK + Lc, :] = v
        tpos = lax.broadcasted_iota(jnp.int32, (Lc, HG_WIDTH), 0) % HG_BLOCK
        pw = 2 * HG_HD
        oi = lax.broadcasted_iota(jnp.int32, (pw, pw), 0) // HG_HD
        oj = lax.broadcasted_iota(jnp.int32, (pw, pw), 1) // HG_HD
        ones2 = jnp.where(oi == oj, 1.0, 0.0).astype(BF16)
        o_pair = [jnp.zeros((Lc, pw), F32) for _ in range(HG_HEADS // 2)]
        for d in range(HG_BLOCK):
            lo = HG_BLOCK - d
            kd = kp_ref[lo:lo + Lc, :]
            gd = gp_ref[lo:lo + Lc, :]
            vd = vp_ref[lo:lo + Lc, :]
            m = tpos >= d
            dec = jnp.exp(jnp.where(m, gb - gd, 0.0))
            pr = jnp.where(m, q * kd * dec, 0.0).astype(BF16)
            for hp in range(HG_HEADS // 2):
                ps = slice(hp * pw, (hp + 1) * pw)
                o_pair[hp] = o_pair[hp] + jnp.dot(pr[:, ps], ones2, preferred_element_type=F32) * vd[:, ps]
        for hp in range(HG_HEADS // 2):
            oi_ref[:, hp * pw:(hp + 1) * pw] = o_pair[hp]
        pass_state(HG_BLOCK, gb, gl, lambda hs, qg: oi_ref[:, hs])

    @pl.when(c == nc - 1)
    def _fin():
        for h in range(HG_HEADS):
            so_ref[0, h] = st_ref[h].T


def _hgrn(proj, p, lb, *, row0, B, T, Lc, s0=None, layer=None, ycat=None):
    nc = T // Lc
    rb0 = row0 // Lc
    has_state = s0 is not None

    def rows(col):
        return lambda b, c: (rb0 + b * nc + c, col // HG_WIDTH)

    const = lambda b, c: (0, 0)
    args = [proj, proj, proj, proj, lb, p['hgrn_norm']]
    specs = [pl.BlockSpec((Lc, HG_WIDTH), rows(COL_HQ)), pl.BlockSpec((Lc, HG_WIDTH), rows(COL_HF)),
             pl.BlockSpec((Lc, HG_WIDTH), rows(COL_HI)), pl.BlockSpec((Lc, HG_WIDTH), rows(COL_HG)),
             pl.BlockSpec((1, HG_WIDTH), const), pl.BlockSpec((1, HG_WIDTH), const)]
    if has_state:
        args.append(s0)
        specs.append(pl.BlockSpec((None, 1, HG_HEADS, HG_HD, HG_HD), lambda b, c: (layer, b, 0, 0, 0)))
    yshape, aliases = _mixed_out(ycat, proj.shape[0], args, specs)
    y, so = pl.pallas_call(
        functools.partial(_hgrn_kernel, Lc=Lc, has_state=has_state, aliased=ycat is not None),
        grid=(B, nc),
        in_specs=specs,
        out_specs=[pl.BlockSpec((Lc, HG_WIDTH), rows(YCOL_HG)),
                   pl.BlockSpec((1, HG_HEADS, HG_HD, HG_HD), lambda b, c: (b, 0, 0, 0))],
        out_shape=[yshape, jax.ShapeDtypeStruct((B, HG_HEADS, HG_HD, HG_HD), F32)],
        scratch_shapes=[pltpu.VMEM((HG_BLOCK + Lc, HG_WIDTH), F32)] * 3
                       + [pltpu.VMEM((HG_HEADS, HG_HD, HG_HD), F32), pltpu.VMEM((Lc, HG_WIDTH), F32)],
        input_output_aliases=aliases,
        compiler_params=_cparams(("parallel", "arbitrary")),
        name="hgrn",
    )(*args)
    return y, so


def _swa_kernel(*refs, Tq, prompt, aliased):
    sink_ref, q_ref, k_ref, v_ref, qg_ref, kg_ref = refs[:6]
    rest = refs[6:]
    if not prompt:
        ck_ref, cv_ref = rest[:2]
        rest = rest[2:]
    if aliased:
        rest = rest[1:]
    y_ref, ko_ref, vo_ref, bias_ref = rest[:4]
    if prompt:
        kp_ref, vp_ref = rest[4:]
    c = pl.program_id(1)
    Tk = WINDOW + Tq
    bi = lax.broadcasted_iota(jnp.int32, (LANE, LANE), 0) // SWA_HD
    bj = lax.broadcasted_iota(jnp.int32, (LANE, LANE), 1) // SWA_HD
    bd = jnp.where(bi == bj, 1.0, 0.0).astype(BF16)

    def hnorm(x, gain):
        outs = []
        for j in range(x.shape[1] // LANE):
            xs = x[:, j * LANE:(j + 1) * LANE]
            ms = _split_dot(xs * xs, bd) * (1.0 / SWA_HD)
            outs.append(xs * lax.rsqrt(ms + EPS))
        return (jnp.concatenate(outs, axis=1) if len(outs) > 1 else outs[0]) * gain

    qn = hnorm(q_ref[...], qg_ref[...])
    kn = hnorm(k_ref[...], kg_ref[...])
    vn = v_ref[...]
    if prompt:
        @pl.when(c == 0)
        def _init():
            kp_ref[...] = jnp.zeros((WINDOW, LANE), F32)
            vp_ref[...] = jnp.zeros((WINDOW, LANE), F32)
        kprev = kp_ref[...]
        vprev = vp_ref[...]
    else:
        kprev = ck_ref[0]
        vprev = cv_ref[0]
    kall = jnp.concatenate([kprev, kn], axis=0)
    vall = jnp.concatenate([vprev, vn], axis=0)
    knew = kall[Tq:, :]
    vnew = vall[Tq:, :]
    ko_ref[0] = knew
    vo_ref[0] = vnew
    if prompt:
        kp_ref[...] = knew
        vp_ref[...] = vnew
    ksw = pltpu.roll(kall, SWA_HD, 1)
    vsw = pltpu.roll(vall, SWA_HD, 1)
    lowk = lax.broadcasted_iota(jnp.int32, (Tk, LANE), 1) < SWA_HD
    R = SWA_REP * Tq
    rows_i = lax.broadcasted_iota(jnp.int32, (R, Tk), 0)
    cols_i = lax.broadcasted_iota(jnp.int32, (R, Tk), 1)
    rep_i = rows_i // Tq
    rep_c = lax.broadcasted_iota(jnp.int32, (R, 1), 0) // Tq

    @pl.when(c == 0)
    def _bias():
        dist = jnp.abs(rows_i % Tq + WINDOW - cols_i).astype(F32)
        for g in range(SWA_KV_HEADS):
            slope = jnp.zeros((R, Tk), F32)
            for r in range(SWA_REP):
                slope = jnp.where(rep_i == r, 2.0 ** (-8.0 * (g * SWA_REP + r + 1) / SWA_HEADS), slope)
            bias_ref[g] = slope * dist

    valid = (cols_i + c * Tq) >= WINDOW
    lane = lax.broadcasted_iota(jnp.int32, (Tq, LANE), 1)
    low = lane < SWA_HD
    for g in range(SWA_KV_HEADS):
        parts = []
        for r in range(SWA_REP):
            hd = g * SWA_REP + r
            qp = qn[:, (hd // 2) * LANE:(hd // 2 + 1) * LANE]
            parts.append(jnp.where(low if hd % 2 == 0 else jnp.logical_not(low), qp, 0.0))
        qst = jnp.concatenate(parts, axis=0).astype(BF16)
        kboth = (jnp.where(lowk, kall, ksw) if g == 0 else jnp.where(lowk, ksw, kall)).astype(BF16)
        vboth = (jnp.where(lowk, vall, vsw) if g == 0 else jnp.where(lowk, vsw, vall)).astype(BF16)
        sink = jnp.zeros((R, 1), F32)
        for r in range(SWA_REP):
            sink = jnp.where(rep_c == r, sink_ref[g * SWA_REP + r], sink)
        s = lax.dot_general(qst, kboth, NT_DIMS, preferred_element_type=F32) * (SWA_HD ** -0.5)
        s = s - bias_ref[g]
        if prompt:
            s = jnp.where(valid, s, NEG)
        mx = jnp.maximum(jnp.max(s, axis=-1, keepdims=True), sink)
        pe = jnp.exp(s - mx)
        if prompt:
            pe = jnp.where(valid, pe, 0.0)
        inv = 1.0 / (jnp.sum(pe, axis=-1, keepdims=True) + jnp.exp(sink - mx))
        o = jnp.dot((pe * inv).astype(BF16), vboth, preferred_element_type=F32)
        for jj in range(SWA_REP // 2):
            pair = g * (SWA_REP // 2) + jj
            y_ref[:, pair * LANE:(pair + 1) * LANE] = jnp.where(
                low, o[(2 * jj) * Tq:(2 * jj + 1) * Tq, :], o[(2 * jj + 1) * Tq:(2 * jj + 2) * Tq, :]
            ).astype(y_ref.dtype)


def _swa(proj, p, *, row0, B, T, ck=None, cv=None, ycat=None):
    prompt = ck is None
    Tq = CHUNK if prompt else T
    nc = T // Tq
    rb0 = row0 // Tq

    def rows(col, width):
        return lambda b, c: (rb0 + b * nc + c, col // width)

    const = lambda b, c: (0, 0)
    args = [p['swa_sinks'], proj, proj, proj, p['swa_qnorm'], p['swa_knorm']]
    specs = [pl.BlockSpec(memory_space=pltpu.SMEM),
             pl.BlockSpec((Tq, 1024), rows(COL_SQ, 1024)),
             pl.BlockSpec((Tq, LANE), rows(COL_SK, LANE)),
             pl.BlockSpec((Tq, LANE), rows(COL_SV, LANE)),
             pl.BlockSpec((1, 1024), const), pl.BlockSpec((1, LANE), const)]
    if not prompt:
        args += [ck.reshape(B, WINDOW, LANE), cv.reshape(B, WINDOW, LANE)]
        specs += [pl.BlockSpec((1, WINDOW, LANE), lambda b, c: (b, 0, 0))] * 2
    yshape, aliases = _mixed_out(ycat, proj.shape[0], args, specs)
    y, ko, vo = pl.pallas_call(
        functools.partial(_swa_kernel, Tq=Tq, prompt=prompt, aliased=ycat is not None),
        grid=(B, nc),
        in_specs=specs,
        out_specs=[pl.BlockSpec((Tq, 1024), rows(YCOL_SWA, 1024)),
                   pl.BlockSpec((1, WINDOW, LANE), lambda b, c: (b, 0, 0)),
                   pl.BlockSpec((1, WINDOW, LANE), lambda b, c: (b, 0, 0))],
        out_shape=[yshape,
                   jax.ShapeDtypeStruct((B, WINDOW, LANE), F32),
                   jax.ShapeDtypeStruct((B, WINDOW, LANE), F32)],
        scratch_shapes=[pltpu.VMEM((SWA_KV_HEADS, SWA_REP * Tq, WINDOW + Tq), F32)]
                       + ([pltpu.VMEM((WINDOW, LANE), F32)] * 2 if prompt else []),
        input_output_aliases=aliases,
        compiler_params=_cparams(("parallel", "arbitrary")),
        name="swa",
    )(*args)
    shp = (B, WINDOW, SWA_KV_HEADS, SWA_HD)
    return y, ko.reshape(shp), vo.reshape(shp)


def _mem_kernel(q_ref, k_ref, v_ref, *rest):
    o_ref = rest[-1]
    for h in range(MEM_HEADS):
        hs = slice(h * MEM_HD, (h + 1) * MEM_HD)
        s = lax.dot_general(q_ref[:, hs].astype(BF16), k_ref[:, hs].astype(BF16), NT_DIMS,
                            preferred_element_type=F32) * (MEM_HD ** -0.5)
        mx = jnp.max(s, axis=-1, keepdims=True)
        pe = jnp.exp(s - mx)
        pe = pe / jnp.sum(pe, axis=-1, keepdims=True)
        o_ref[:, hs] = jnp.dot(pe.astype(BF16), v_ref[:, hs].astype(BF16),
                               preferred_element_type=F32).astype(o_ref.dtype)


def _mem_attend(qn, mk, mv, *, row0, B, T, tq, layer=None, out=None):
    nt = T // tq
    rb0 = row0 // tq
    args = [qn, mk, mv]
    if layer is None:
        kv = pl.BlockSpec((MEM_TOKENS, MEM_WIDTH), lambda b, t: (b, 0))
    else:
        kv = pl.BlockSpec((None, MEM_TOKENS, MEM_WIDTH), lambda b, t: (layer, b, 0))
    specs = [pl.BlockSpec((tq, MEM_WIDTH), lambda b, t: (rb0 + b * nt + t, 0)), kv, kv]
    aliases = {}
    if out is not None:
        args.append(out)
        specs.append(pl.BlockSpec(memory_space=pl.ANY))
        aliases = {3: 0}
    return pl.pallas_call(
        _mem_kernel,
        grid=(B, nt),
        in_specs=specs,
        out_specs=pl.BlockSpec((tq, MEM_WIDTH), lambda b, t: (rb0 + b * nt + t, 0)),
        out_shape=jax.ShapeDtypeStruct((qn.shape[0], MEM_WIDTH), BF16),
        input_output_aliases=aliases,
        compiler_params=_cparams(("parallel", "arbitrary")),
        name="mem_attend",
    )(*args)


def _route_kernel(pq_ref, keys_ref, s1_ref, e1_ref, s2_ref, e2_ref, tau_ref):
    half = PEER_QDIM // 2
    ninf = -jnp.inf
    K = PEER_TOPK

    def top_distinct(s):
        cur, vals = s, []
        for _ in range(K):
            mx = jnp.max(cur, axis=0, keepdims=True)
            vals.append(mx)
            cur = jnp.where(cur == mx, ninf, cur)
        return vals

    def pick_first_max(cur, rows):
        mx = jnp.max(cur, axis=0, keepdims=True)
        first = jnp.min(jnp.where(cur == mx, rows, float(cur.shape[0])), axis=0, keepdims=True)
        return mx, rows == first

    def top_ranked(s):
        rows = lax.broadcasted_iota(jnp.int32, s.shape, 0).astype(F32)
        cur, vals, rank = s, [], jnp.full(s.shape, float(K), F32)
        for it in range(K):
            mx, hit = pick_first_max(cur, rows)
            vals.append(mx)
            rank = jnp.where(hit, float(it), rank)
            cur = jnp.where(hit, ninf, cur)
        return vals, rank

    def count_ge(x, thr):
        return jnp.sum(jnp.where(x >= thr, 1.0, 0.0), axis=0, keepdims=True)

    def all_cands(v1, v2m):
        return jnp.concatenate([v1[r] + v2m for r in range(K)], axis=0)

    def finish(h, s1, s2, v1, v2m, cand, tau):
        m0 = v1[0] + v2m[0:1, :]
        zsum = jnp.sum(jnp.where(cand >= tau, jnp.exp(cand - m0), 0.0), axis=0, keepdims=True)
        s1_ref[h] = jnp.where(s1 >= v1[K - 1], s1, ninf)
        s2_ref[h] = jnp.where(s2 >= v2m[K - 1:K, :], s2, ninf)
        e1_ref[h] = jnp.exp(s1 - v1[0])
        e2_ref[h] = jnp.exp(s2 - v2m[0:1, :]) / zsum
        tau_ref[h:h + 1, :] = tau

    tm = pq_ref.shape[0]

    def scores(h):
        q1 = pq_ref[:, h * PEER_QDIM:h * PEER_QDIM + half]
        q2 = pq_ref[:, h * PEER_QDIM + half:(h + 1) * PEER_QDIM]
        return (lax.dot_general(keys_ref[h, 0], q1, NT_DIMS, precision=HIGHEST, preferred_element_type=F32),
                lax.dot_general(keys_ref[h, 1], q2, NT_DIMS, precision=HIGHEST, preferred_element_type=F32))

    any_tie = []
    for h in range(PEER_HEADS):
        s1, s2 = scores(h)
        v1 = top_distinct(s1)
        v2 = top_distinct(s2)
        v2m = jnp.concatenate(v2, axis=0)
        rowi = lax.broadcasted_iota(jnp.int32, (8, tm), 0)
        blocks = [v1[0] + v2m, v1[1] + v2m[0:8, :]]
        for r in range(2, 8):
            blocks.append(jnp.where(rowi < K // (r + 1), v1[r] + v2m[0:8, :], ninf))
        blocks.append(jnp.concatenate(v1[8:], axis=0) + v2m[0:1, :])
        cand = jnp.concatenate(blocks, axis=0)
        cur = cand
        for _ in range(K):
            tau = jnp.max(cur, axis=0, keepdims=True)
            cur = jnp.where(cur == tau, ninf, cur)
        full = all_cands(v1, v2m)
        tied = jnp.where(count_ge(full, tau) == K,
                         jnp.where(count_ge(s1, v1[K - 1]) == K,
                                   jnp.where(count_ge(s2, v2[K - 1]) == K, 0.0, 1.0), 1.0), 1.0)
        finish(h, s1, s2, v1, v2m, cand, tau)
        any_tie.append(jnp.max(tied))

    for h in range(PEER_HEADS):
        @pl.when(any_tie[h] > 0.0)
        def _with_ties():
            s1, s2 = scores(h)
            w1, rank1 = top_ranked(s1)
            w2, rank2 = top_ranked(s2)
            w2m = jnp.concatenate(w2, axis=0)
            allc = all_cands(w1, w2m)
            rows = lax.broadcasted_iota(jnp.int32, allc.shape, 0).astype(F32)
            cur, picked = allc, jnp.zeros(allc.shape, F32)
            for _ in range(K):
                _, hit = pick_first_max(cur, rows)
                picked = jnp.where(hit, 1.0, picked)
                cur = jnp.where(hit, ninf, cur)
            m0 = w1[0] + w2m[0:1, :]
            zsum = jnp.sum(picked * jnp.exp(allc - m0), axis=0, keepdims=True)
            lmap = jnp.zeros(s1.shape, F32)
            for r in range(K):
                length = jnp.sum(picked[r * K:(r + 1) * K, :], axis=0, keepdims=True)
                lmap = jnp.where(rank1 == float(r), length, lmap)
            s1_ref[h] = lmap
            s2_ref[h] = jnp.where(rank2 < float(K), -rank2, -1000.0)
            e1_ref[h] = jnp.exp(s1 - w1[0])
            e2_ref[h] = jnp.exp(s2 - w2m[0:1, :]) / zsum
            tau_ref[h:h + 1, :] = jnp.ones((1, tm), F32)


def _peer_route(pq, keys, *, tm=256):
    N = pq.shape[0]
    big = jax.ShapeDtypeStruct((PEER_HEADS, N_KEYS, N), F32)
    bspec = pl.BlockSpec((PEER_HEADS, N_KEYS, tm), lambda i: (0, 0, i))
    return pl.pallas_call(
        _route_kernel,
        grid=(N // tm,),
        in_specs=[pl.BlockSpec((tm, PEER_HEADS * PEER_QDIM), lambda i: (i, 0)),
                  pl.BlockSpec((PEER_HEADS, 2, N_KEYS, PEER_QDIM // 2), lambda i: (0, 0, 0, 0))],
        out_specs=[bspec, bspec, bspec, bspec, pl.BlockSpec((PEER_HEADS, tm), lambda i: (0, i))],
        out_shape=[big, big, big, big, jax.ShapeDtypeStruct((PEER_HEADS, N), F32)],
        compiler_params=_cparams(("parallel",)),
        name="peer_route",
    )(pq, keys)


def _dense_kernel(h_hbm, g_ref, u_ref, v_ref, s1_ref, e1_ref, s2_ref, e2_ref, tau_ref, o_ref, xn_ref, c_ref,
                  sem, *, te):
    i = pl.program_id(0)
    j = pl.program_id(1)
    na = te // N_KEYS
    tm = o_ref.shape[0]

    @pl.when(j == 0)
    def _init():
        fetch = pltpu.make_async_copy(h_hbm.at[pl.ds(i * tm, tm), :], o_ref, sem)
        fetch.start()
        fetch.wait()
        x = o_ref[...]
        ms = jnp.mean(x * x, axis=-1, keepdims=True)
        xn_ref[...] = (x * lax.rsqrt(ms + EPS) * g_ref[...]).astype(BF16)

    pre = lax.dot_general(xn_ref[...], u_ref[...], NT_DIMS, preferred_element_type=F32)
    act = 0.5 * pre * (1.0 + lax.erf(pre * (2.0 ** -0.5)))
    for al in range(na):
        a = j * na + al
        gt = jnp.zeros((N_KEYS, tm), F32)
        for h in range(PEER_HEADS):
            hit = (s1_ref[h, pl.ds(a, 1), :] + s2_ref[h]) >= tau_ref[h:h + 1, :]
            gt = gt + jnp.where(hit, e1_ref[h, pl.ds(a, 1), :] * e2_ref[h], 0.0)
        c_ref[:, al * N_KEYS:(al + 1) * N_KEYS] = (gt.T * act[:, al * N_KEYS:(al + 1) * N_KEYS]).astype(BF16)
    o_ref[...] += jnp.dot(c_ref[...], v_ref[...], preferred_element_type=F32)


def _peer_dense(h, gain, u, v, s1, e1, s2, e2, tau, *, tm=768, te=512):
    N, D = h.shape
    nj = N_EXPERTS // te
    assert N % tm == 0 and N_EXPERTS % te == 0
    one = pl.Buffered(1)
    route = pl.BlockSpec((PEER_HEADS, N_KEYS, tm), lambda i, j: (0, 0, i), pipeline_mode=one)
    return pl.pallas_call(
        functools.partial(_dense_kernel, te=te),
        grid=(N // tm, nj),
        in_specs=[pl.BlockSpec(memory_space=pl.ANY),
                  pl.BlockSpec((1, D), lambda i, j: (0, 0)),
                  pl.BlockSpec((te, D), lambda i, j: (j, 0)),
                  pl.BlockSpec((te, D), lambda i, j: (j, 0)),
                  route, route, route, route,
                  pl.BlockSpec((PEER_HEADS, tm), lambda i, j: (0, i))],
        out_specs=pl.BlockSpec((tm, D), lambda i, j: (i, 0), pipeline_mode=one),
        out_shape=jax.ShapeDtypeStruct((N, D), F32),
        scratch_shapes=[pltpu.VMEM((tm, D), BF16), pltpu.VMEM((tm, te), BF16), pltpu.SemaphoreType.DMA(())],
        compiler_params=_cparams(("parallel", "arbitrary")),
        name="peer_dense",
    )(h, gain.reshape(1, D).astype(F32), u, v, s1, e1, s2, e2, tau)


def _row(x, width=None):
    x = x.astype(F32).reshape(1, -1)
    if width is not None and x.shape[1] < width:
        x = jnp.pad(x, ((0, 0), (0, width - x.shape[1])))
    return x


def _cast_kernel(x_ref, o_ref):
    o_ref[...] = x_ref[...].astype(o_ref.dtype)


def _to_bf16(w, *, block_bytes=8 * 1024 * 1024):
    depth, rows, cols = w.shape
    tr = max(8, min(rows, block_bytes // (cols * 4)))
    assert rows % tr == 0
    return pl.pallas_call(
        _cast_kernel,
        grid=(depth, rows // tr),
        in_specs=[pl.BlockSpec((None, tr, cols), lambda l, r: (l, r, 0))],
        out_specs=pl.BlockSpec((None, tr, cols), lambda l, r: (l, r, 0)),
        out_shape=jax.ShapeDtypeStruct(w.shape, BF16),
        compiler_params=_cparams(("parallel", "parallel")),
        name="to_bf16",
    )(w)


PACK_ROWS = 256
PACK_GROUP = SSD_HEADS


def _pack_kernel(w_ref, dt_ref, o_ref):
    r = pl.program_id(1)
    last = pl.num_programs(1) - 1

    @pl.when(r < last)
    def _copy():
        o_ref[...] = w_ref[...].reshape(PACK_ROWS, w_ref.shape[-1]).astype(BF16)

    @pl.when(r == last)
    def _dt_and_pad():
        o_ref[0:PACK_GROUP, :] = dt_ref[...].astype(BF16)
        o_ref[PACK_GROUP:, :] = jnp.zeros((PACK_ROWS - PACK_GROUP, o_ref.shape[-1]), BF16)


def _pack_w_in(w_in):
    depth, d, cols = w_in.shape
    xbc_end = 3 * SSD_INNER
    assert cols == COL_DT + SSD_HEADS and cols % PACK_GROUP == 0 and PROJ_COLS % PACK_ROWS == 0
    gpb = PACK_ROWS // PACK_GROUP
    n_groups = cols // PACK_GROUP
    n_head = xbc_end // PACK_ROWS
    skip = (xbc_end + SSD_HEADS) // PACK_GROUP
    wt = jnp.swapaxes(w_in, 1, 2).reshape(depth, n_groups, PACK_GROUP, d)

    def src_group(l, r):
        g = jnp.where(r < n_head, r * gpb, skip + (r - n_head) * gpb)
        return (l, jnp.minimum(g, n_groups - gpb), 0, 0)

    return pl.pallas_call(
        _pack_kernel,
        grid=(depth, PROJ_COLS // PACK_ROWS),
        in_specs=[pl.BlockSpec((pl.Element(1), pl.Element(gpb), pl.Element(PACK_GROUP), pl.Element(d)),
                               src_group),
                  pl.BlockSpec((None, None, PACK_GROUP, d), lambda l, r: (l, xbc_end // PACK_GROUP, 0, 0))],
        out_specs=pl.BlockSpec((None, PACK_ROWS, d), lambda l, r: (l, r, 0)),
        out_shape=jax.ShapeDtypeStruct((depth, PROJ_COLS, d), BF16),
        compiler_params=_cparams(("parallel", "arbitrary")),
        name="pack_w_in",
    )(wt, wt)


def _layer_params(l, conv_w, conv_b, dt_bias, a_log, d_skip, ssd_norm, hgrn_norm, swa_qnorm,
                  swa_knorm, swa_sinks):
    eh = lax.broadcasted_iota(jnp.int32, (LANE, SSD_INNER), 0)
    ec = lax.broadcasted_iota(jnp.int32, (LANE, SSD_INNER), 1)
    return {
        'expand': (ec // SSD_HD == eh).astype(BF16),
        'cwx': conv_w[l][:, :SSD_INNER].astype(F32), 'cwb': conv_w[l][:, SSD_INNER:].astype(F32),
        'cbx': _row(conv_b[l][:SSD_INNER]), 'cbb': _row(conv_b[l][SSD_INNER:]),
        'dtb': _row(dt_bias[l], LANE), 'alog': _row(a_log[l], LANE),
        'dsk': _row(jnp.repeat(d_skip[l], SSD_HD)), 'ssd_norm': _row(ssd_norm[l]),
        'hgrn_norm': _row(jnp.tile(hgrn_norm[l], HG_HEADS)),
        'swa_qnorm': _row(jnp.tile(swa_qnorm[l], SWA_HEADS)),
        'swa_knorm': _row(jnp.tile(swa_knorm[l], SWA_KV_HEADS)),
        'swa_sinks': swa_sinks[l].astype(F32),
    }


def kernel(x_prompt, x_sample, mem_prompt, state_ssm, state_ssd_conv, state_hgrn, cache_swa_k, cache_swa_v,
           cache_mem_k, cache_mem_v, norm_mix, w_in, conv_w, conv_b, dt_bias, a_log, d_skip, ssd_norm,
           hgrn_lb, hgrn_norm, swa_qnorm, swa_knorm, swa_sinks, w_out, norm_mem, norm_memtok, w_mq, w_mk,
           w_mv, mem_qnorm, mem_knorm, w_mo, norm_ffn, w_pq, peer_keys, peer_u, peer_v):
    depth = w_in.shape[0]
    bp, tp, _ = x_prompt.shape
    bs, ts, _ = x_sample.shape
    n_p, n_s = bp * tp, bs * ts
    lbp = jax.nn.softmax(hgrn_lb.astype(F32), axis=0)
    lower = jnp.cumsum(lbp, axis=0) - lbp[0:1]
    h = jnp.concatenate([x_prompt.reshape(n_p, D_MODEL), x_sample.reshape(n_s, D_MODEL)], axis=0)
    mem2d = mem_prompt.reshape(bp * MEM_TOKENS, D_MODEL)
    outs = {k: [] for k in ('ssm_p', 'conv_p', 'hg_p', 'swk_p', 'swv_p', 'mk_p', 'mv_p',
                            'ssm_s', 'conv_s', 'hg_s', 'swk_s', 'swv_s')}
    w_in_b = _pack_w_in(w_in)
    w_out_b, w_mq_b, w_mk_b, w_mv_b, w_mo_b, w_pq_b = (
        _to_bf16(t) for t in (w_out, w_mq, w_mk, w_mv, w_mo, w_pq))
    for l in range(depth):
        p = _layer_params(l, conv_w, conv_b, dt_bias, a_log, d_skip, ssd_norm, hgrn_norm,
                          swa_qnorm, swa_knorm, swa_sinks)
        lb = _row(lower[l])
        proj, u_b, v_b = _mm(h, w_in_b, layer=l, w_nk=True, gain=norm_mix[l], ride=(peer_u, peer_v),
                             name="proj_in")
        ycat = jnp.zeros((n_p + n_s, D_MODEL), BF16)
        ycat, conv_p, ssm_p = _ssd(proj, p, row0=0, B=bp, T=tp, ycat=ycat)
        ycat, conv_s, ssm_s = _ssd(proj, p, row0=n_p, B=bs, T=ts, conv0=state_ssd_conv, h0=state_ssm,
                                   layer=l, ycat=ycat)
        ycat, hg_p = _hgrn(proj, p, lb, row0=0, B=bp, T=tp, Lc=128, ycat=ycat)
        ycat, hg_s = _hgrn(proj, p, lb, row0=n_p, B=bs, T=ts, Lc=ts, s0=state_hgrn, layer=l, ycat=ycat)
        ycat, swk_p, swv_p = _swa(proj, p, row0=0, B=bp, T=tp, ycat=ycat)
        ycat, swk_s, swv_s = _swa(proj, p, row0=n_p, B=bs, T=ts, ck=cache_swa_k[l], cv=cache_swa_v[l],
                                  ycat=ycat)
        h = _mm(ycat, w_out_b, layer=l, res=h, name="proj_out")
        mk = _mm(mem2d, w_mk_b, layer=l, gain=norm_memtok[l], head_gain=mem_knorm[l], name="mem_k")
        mv = _mm(mem2d, w_mv_b, layer=l, gain=norm_memtok[l], name="mem_v")
        qn = _mm(h, w_mq_b, layer=l, gain=norm_mem[l], head_gain=mem_qnorm[l], name="mem_q")
        om = _mem_attend(qn, mk, mv, row0=0, B=bp, T=tp, tq=512,
                         out=jnp.zeros((n_p + n_s, MEM_WIDTH), BF16))
        om = _mem_attend(qn, cache_mem_k.reshape(depth, bs * MEM_TOKENS, MEM_WIDTH),
                         cache_mem_v.reshape(depth, bs * MEM_TOKENS, MEM_WIDTH), layer=l,
                         row0=n_p, B=bs, T=ts, tq=ts,
                         out=om)
        h = _mm(om, w_mo_b, layer=l, res=h, name="mem_o")
        pq = _mm(h, w_pq_b, layer=l, gain=norm_ffn[l], name="peer_q")
        s1, e1, s2, e2, tau = _peer_route(pq, peer_keys[l].astype(F32))
        h = _peer_dense(h, norm_ffn[l], u_b, v_b, s1, e1, s2, e2, tau)
        for name, val in (('ssm_p', ssm_p), ('conv_p', conv_p), ('hg_p', hg_p), ('swk_p', swk_p),
                          ('swv_p', swv_p), ('ssm_s', ssm_s), ('conv_s', conv_s), ('hg_s', hg_s),
                          ('swk_s', swk_s), ('swv_s', swv_s)):
            outs[name].append(val)
        outs['mk_p'].append(mk.reshape(bp, MEM_TOKENS, MEM_HEADS, MEM_HD))
        outs['mv_p'].append(mv.reshape(bp, MEM_TOKENS, MEM_HEADS, MEM_HD))
    st = lambda k: jnp.stack(outs[k])
    return (h[:n_p].reshape(bp, tp, D_MODEL), h[n_p:].reshape(bs, ts, D_MODEL),
            st('ssm_p'), st('conv_p'), st('hg_p'), st('swk_p'), st('swv_p'), st('mk_p'), st('mv_p'),
            st('ssm_s'), st('conv_s'), st('hg_s'), st('swk_s'), st('swv_s'))
```

```python
import functools
import math

import jax
import jax.numpy as jnp
from jax import lax
from jax.experimental import pallas as pl
from jax.experimental.pallas import tpu as pltpu

F32 = jnp.float32
BF16 = jnp.bfloat16
HIGHEST = lax.Precision.HIGHEST

D_MODEL = 4096
PAST_LEN = 2048
CHUNK = 64
SSD_INNER = 2048
SSD_HD = 64
SSD_HEADS = 32
SSD_GROUPS = 8
SSD_STATE = 128
SSD_GW = SSD_INNER // SSD_GROUPS
SSD_REP = SSD_HEADS // SSD_GROUPS
HG_WIDTH = 1024
HG_HD = 128
HG_HEADS = 8
HG_BLOCK = 16
HG_LONG_BLOCK = 64
HG_SAFE_DECAY = 80.0
SWA_HD = 64
SWA_HEADS = 16
SWA_KV_HEADS = 2
SWA_REP = SWA_HEADS // SWA_KV_HEADS
WINDOW = 128
MEM_TOKENS = 256
MEM_HEADS = 4
MEM_HD = 128
MEM_WIDTH = 512
N_KEYS = 128
N_EXPERTS = N_KEYS * N_KEYS
PEER_HEADS = 8
PEER_QDIM = 256
PEER_TOPK = 16
EPS = 1e-6
NEG = -1e30
TINY = 1e-30
LANE = 128
MXU_N = 256

COL_Z, COL_XS, COL_BC = 0, 2048, 4096
COL_HQ, COL_HF, COL_HI, COL_HG = 6144, 7168, 8192, 9216
COL_SQ, COL_SK, COL_SV, COL_DT = 10240, 11264, 11392, 11520
PROJ_COLS = 11776
YCOL_SSD, YCOL_HG, YCOL_SWA = 0, 2048, 3072

VMEM_LIMIT = 60 * 1024 * 1024

MM_RENORM_MAX_TILES = 4
NT_DIMS = (((1,), (1,)), ((), ()))
TN_DIMS = (((0,), (0,)), ((), ()))


def _cparams(sem):
    return pltpu.CompilerParams(dimension_semantics=sem, vmem_limit_bytes=VMEM_LIMIT)


def _sigmoid(x):
    return 1.0 / (1.0 + jnp.exp(-x))


def _silu(x):
    return x * _sigmoid(x)


def _split_dot(x, ones_bf16):
    hi = x.astype(BF16)
    lo = (x - hi.astype(F32)).astype(BF16)
    return (jnp.dot(hi, ones_bf16, preferred_element_type=F32)
            + jnp.dot(lo, ones_bf16, preferred_element_type=F32))


def _mm_kernel(*refs, norm, keep_norm, head_norm, residual, w_nk, n_ride):
    it = iter(refs)
    x_ref = next(it)
    g_ref = next(it) if norm else None
    w_ref = next(it)
    hg_ref = next(it) if head_norm else None
    r_ref = next(it) if residual else None
    ride_in = [next(it) for _ in range(n_ride)]
    o_ref = next(it)
    ride_out = [next(it) for _ in range(n_ride)]
    xn_ref = next(it) if keep_norm else None
    for src, dst in zip(ride_in, ride_out):
        dst[...] = src[...].astype(dst.dtype)
    def normalised():
        x = x_ref[...].astype(F32)
        ms = jnp.mean(x * x, axis=-1, keepdims=True)
        return (x * lax.rsqrt(ms + EPS) * g_ref[...]).astype(BF16)

    if norm and xn_ref is None:
        xb = normalised()
    elif norm:
        @pl.when(pl.program_id(1) == 0)
        def _():
            xn_ref[...] = normalised()
        xb = xn_ref[...]
    else:
        xb = x_ref[...]
    if w_nk:
        acc = lax.dot_general(xb, w_ref[...], NT_DIMS, preferred_element_type=F32)
    else:
        acc = jnp.dot(xb, w_ref[...], preferred_element_type=F32)
    if head_norm:
        parts = []
        for c in range(acc.shape[1] // LANE):
            a = acc[:, c * LANE:(c + 1) * LANE]
            ms = jnp.mean(a * a, axis=-1, keepdims=True)
            parts.append(a * lax.rsqrt(ms + EPS))
        acc = jnp.concatenate(parts, axis=1) * hg_ref[...]
    if residual:
        acc = acc + r_ref[...]
    o_ref[...] = acc.astype(o_ref.dtype)


def _mm(x, w, *, layer=None, w_nk=False, gain=None, head_gain=None, res=None, out_dtype=F32, tn=512,
        ride=(), name="mm"):
    M, K = x.shape
    N = w.shape[-2] if w_nk else w.shape[-1]
    norm, head_norm, residual = gain is not None, head_gain is not None, res is not None
    tn = min(tn, N)
    assert N % tn == 0
    keep_norm = norm and N // tn > MM_RENORM_MAX_TILES
    tm = next(t for t in ((768, 512, 256, 128) if keep_norm else (1024, 512, 256, 128)) if M % t == 0)
    args, specs = [x], [pl.BlockSpec((tm, K), lambda i, j: (i, 0))]
    if norm:
        args.append(gain.reshape(1, K).astype(F32))
        specs.append(pl.BlockSpec((1, K), lambda i, j: (0, 0)))
    args.append(w)
    if w_nk:
        specs.append(pl.BlockSpec((None, tn, K), lambda i, j: (layer, j, 0)))
    elif layer is None:
        specs.append(pl.BlockSpec((K, tn), lambda i, j: (0, j)))
    else:
        specs.append(pl.BlockSpec((None, K, tn), lambda i, j: (layer, 0, j)))
    if head_norm:
        args.append(jnp.tile(head_gain.astype(F32), N // head_gain.shape[0]).reshape(1, N))
        specs.append(pl.BlockSpec((1, tn), lambda i, j: (0, j)))
    if residual:
        args.append(res)
        specs.append(pl.BlockSpec((tm, tn), lambda i, j: (i, j)))
    grid = (M // tm, N // tn)
    out_specs = [pl.BlockSpec((tm, tn), lambda i, j: (i, j))]
    out_shape = [jax.ShapeDtypeStruct((M, N), out_dtype)]
    for r in ride:
        _, rows, cols = r.shape
        rb = next(b for b in (8, 16, 32, 64, 128, 256, 512) if rows % b == 0 and rows // b <= grid[0] * grid[1])
        last = rows // rb - 1
        args.append(r)
        specs.append(pl.BlockSpec(
            (None, rb, cols), lambda i, j, last=last: (layer, jnp.minimum(i * grid[1] + j, last), 0)))
        out_specs.append(pl.BlockSpec(
            (rb, cols), lambda i, j, last=last: (jnp.minimum(i * grid[1] + j, last), 0)))
        out_shape.append(jax.ShapeDtypeStruct((rows, cols), BF16))
    outs = pl.pallas_call(
        functools.partial(_mm_kernel, norm=norm, keep_norm=keep_norm, head_norm=head_norm, residual=residual,
                          w_nk=w_nk, n_ride=len(ride)),
        grid=grid,
        in_specs=specs,
        out_specs=out_specs,
        out_shape=out_shape,
        scratch_shapes=[pltpu.VMEM((tm, K), BF16)] if keep_norm else [],
        compiler_params=_cparams(("arbitrary", "arbitrary") if ride else ("parallel", "arbitrary")),
        name=name,
    )(*args)
    return outs if ride else outs[0]


def _ssd_kernel(*refs, L, has_state, aliased):
    (z_ref, xs_ref, bc_ref, dt_ref, cwx_ref, cwb_ref, cbx_ref, cbb_ref, dtb_ref, alog_ref,
     dsk_ref, gn_ref, exp_ref) = refs[:13]
    rest = refs[13:]
    if has_state:
        conv0_ref, h0_ref = rest[:2]
        rest = rest[2:]
    if aliased:
        rest = rest[1:]
    y_ref, convo_ref, ho_ref, xpx_ref, xpb_ref, hT_ref = rest
    c = pl.program_id(1)
    nc = pl.num_programs(1)

    @pl.when(c == 0)
    def _init():
        if has_state:
            xpx_ref[0:8, :] = jnp.zeros((8, SSD_INNER), F32)
            xpb_ref[0:8, :] = jnp.zeros((8, SSD_INNER), F32)
            xpx_ref[5:8, :] = conv0_ref[0, :, 0:SSD_INNER]
            xpb_ref[5:8, :] = conv0_ref[0, :, SSD_INNER:2 * SSD_INNER]
            for g in range(SSD_GROUPS):
                hT_ref[g] = h0_ref[0, g * SSD_GW:(g + 1) * SSD_GW, :].T
        else:
            xpx_ref[0:8, :] = jnp.zeros((8, SSD_INNER), F32)
            xpb_ref[0:8, :] = jnp.zeros((8, SSD_INNER), F32)
            hT_ref[...] = jnp.zeros(hT_ref.shape, F32)

    xpx_ref[8:8 + L, :] = xs_ref[...]
    xpb_ref[8:8 + L, :] = bc_ref[...]

    def conv(xp_ref, w_ref, b_ref):
        acc = b_ref[...] + w_ref[3:4, :] * xp_ref[8:8 + L, :]
        for j in range(3):
            acc = acc + w_ref[j:j + 1, :] * xp_ref[5 + j:5 + j + L, :]
        return acc

    xc = _silu(conv(xpx_ref, cwx_ref, cbx_ref))
    bcc = _silu(conv(xpb_ref, cwb_ref, cbb_ref))
    tail_x = xpx_ref[5 + L:8 + L, :]
    tail_b = xpb_ref[5 + L:8 + L, :]
    xpx_ref[5:8, :] = tail_x
    xpb_ref[5:8, :] = tail_b
    convo_ref[0, :, 0:SSD_INNER] = tail_x
    convo_ref[0, :, SSD_INNER:2 * SSD_INNER] = tail_b

    dtr = dt_ref[...] + dtb_ref[...]
    dt = jnp.maximum(dtr, 0.0) + jnp.log(1.0 + jnp.exp(-jnp.abs(dtr)))
    a = -jnp.exp(alog_ref[...])
    dta = dt * a
    ri = lax.broadcasted_iota(jnp.int32, (L, L), 0)
    ci = lax.broadcasted_iota(jnp.int32, (L, L), 1)
    causal = ri >= ci
    acum = jnp.dot(causal.astype(F32), dta, precision=HIGHEST, preferred_element_type=F32)
    both = jnp.concatenate([acum, dt], axis=0)
    p0 = both.astype(BF16)
    r1 = both - p0.astype(F32)
    p1 = r1.astype(BF16)
    p2 = (r1 - p1.astype(F32)).astype(BF16)
    expand = exp_ref[...]
    bothx = ((jnp.dot(p0, expand, preferred_element_type=F32)
              + jnp.dot(p1, expand, preferred_element_type=F32))
             + jnp.dot(p2, expand, preferred_element_type=F32))
    acx = bothx[:L, :]
    dtx = bothx[L:, :]
    eax = jnp.exp(acx)
    lastx = acx[L - 1:L, :]
    tailw = jnp.exp(lastx - acx) * dtx
    cdx = jnp.exp(lastx)
    acT = acum.T
    dtT = dt.T
    lane_head = lax.broadcasted_iota(jnp.int32, (L, SSD_GW), 1) // SSD_HD
    z = z_ref[...]

    for g in range(SSD_GROUPS):
        gs = slice(g * SSD_GW, (g + 1) * SSD_GW)
        Bg = bcc[:, g * SSD_STATE:(g + 1) * SSD_STATE].astype(BF16)
        Cg = bcc[:, SSD_GROUPS * SSD_STATE + g * SSD_STATE:
                 SSD_GROUPS * SSD_STATE + (g + 1) * SSD_STATE].astype(BF16)
        cb = lax.dot_general(Cg, Bg, NT_DIMS, preferred_element_type=F32)
        ws = []
        for r in range(SSD_REP):
            hd = g * SSD_REP + r
            seg = acum[:, hd:hd + 1] - acT[hd:hd + 1, :]
            dec = jnp.where(causal, jnp.exp(jnp.where(causal, seg, 0.0)), 0.0)
            ws.append(cb * dec * dtT[hd:hd + 1, :])
        wst = jnp.concatenate(ws, axis=0).astype(BF16)
        xg = xc[:, gs]
        full = jnp.dot(wst, xg.astype(BF16), preferred_element_type=F32)
        y_intra = jnp.zeros((L, SSD_GW), F32)
        for r in range(SSD_REP):
            y_intra = y_intra + jnp.where(lane_head == r, full[r * L:(r + 1) * L, :], 0.0)
        hTg = hT_ref[g]
        y_inter = jnp.dot(Cg, hTg.astype(BF16), preferred_element_type=F32) * eax[:, gs]
        xt = (xg * tailw[:, gs]).astype(BF16)
        hT_ref[g] = hTg * cdx[:, gs] + lax.dot_general(Bg, xt, TN_DIMS, preferred_element_type=F32)
        yg = y_intra + y_inter + dsk_ref[:, gs] * xg
        u = yg * _silu(z[:, gs])
        ms = jnp.mean(u * u, axis=-1, keepdims=True)
        y_ref[:, gs] = (u * lax.rsqrt(ms + EPS) * gn_ref[:, gs]).astype(y_ref.dtype)

    @pl.when(c == nc - 1)
    def _fin():
        for g in range(SSD_GROUPS):
            ho_ref[0, g * SSD_GW:(g + 1) * SSD_GW, :] = hT_ref[g].T


def _mixed_out(ycat, n_total, args, specs):
    aliases = {}
    if ycat is not None:
        args.append(ycat)
        specs.append(pl.BlockSpec(memory_space=pl.ANY))
        aliases = {len(args) - 1: 0}
    return jax.ShapeDtypeStruct((n_total, D_MODEL), BF16), aliases


def _ssd(proj, p, *, row0, B, T, conv0=None, h0=None, layer=None, ycat=None):
    L = math.gcd(T, CHUNK)
    nc = T // L
    rb0 = row0 // L
    has_state = conv0 is not None

    def rows(col):
        return lambda b, c: (rb0 + b * nc + c, col)

    const = lambda b, c: (0, 0)
    args = [proj, proj, proj, proj, p['cwx'], p['cwb'], p['cbx'], p['cbb'], p['dtb'], p['alog'],
            p['dsk'], p['ssd_norm'], p['expand']]
    specs = [pl.BlockSpec((L, SSD_INNER), rows(COL_Z // SSD_INNER)),
             pl.BlockSpec((L, SSD_INNER), rows(COL_XS // SSD_INNER)),
             pl.BlockSpec((L, SSD_INNER), rows(COL_BC // SSD_INNER)),
             pl.BlockSpec((L, LANE), rows(COL_DT // LANE)),
             pl.BlockSpec((4, SSD_INNER), const), pl.BlockSpec((4, SSD_INNER), const),
             pl.BlockSpec((1, SSD_INNER), const), pl.BlockSpec((1, SSD_INNER), const),
             pl.BlockSpec((1, LANE), const), pl.BlockSpec((1, LANE), const),
             pl.BlockSpec((1, SSD_INNER), const), pl.BlockSpec((1, SSD_INNER), const),
             pl.BlockSpec((LANE, SSD_INNER), const)]
    if has_state:
        args += [conv0, h0.reshape(h0.shape[0], B, SSD_INNER, SSD_STATE)]
        specs += [pl.BlockSpec((None, 1, 3, 2 * SSD_INNER), lambda b, c: (layer, b, 0, 0)),
                  pl.BlockSpec((None, 1, SSD_INNER, SSD_STATE), lambda b, c: (layer, b, 0, 0))]
    yshape, aliases = _mixed_out(ycat, proj.shape[0], args, specs)
    y, convo, ho = pl.pallas_call(
        functools.partial(_ssd_kernel, L=L, has_state=has_state, aliased=ycat is not None),
        grid=(B, nc),
        in_specs=specs,
        out_specs=[pl.BlockSpec((L, SSD_INNER), rows(0)),
                   pl.BlockSpec((1, 3, 2 * SSD_INNER), lambda b, c: (b, 0, 0)),
                   pl.BlockSpec((1, SSD_INNER, SSD_STATE), lambda b, c: (b, 0, 0))],
        out_shape=[yshape,
                   jax.ShapeDtypeStruct((B, 3, 2 * SSD_INNER), F32),
                   jax.ShapeDtypeStruct((B, SSD_INNER, SSD_STATE), F32)],
        scratch_shapes=[pltpu.VMEM((8 + L, SSD_INNER), F32), pltpu.VMEM((8 + L, SSD_INNER), F32),
                        pltpu.VMEM((SSD_GROUPS, SSD_STATE, SSD_GW), F32)],
        input_output_aliases=aliases,
        compiler_params=_cparams(("parallel", "arbitrary")),
        name="ssd",
    )(*args)
    return y, convo, ho.reshape(B, SSD_HEADS, SSD_HD, SSD_STATE)


def _hgrn_kernel(*refs, Lc, has_state, aliased):
    q_ref, f_ref, i_ref, g_ref, lb_ref, gn_ref = refs[:6]
    rest = refs[6:]
    if has_state:
        s0_ref = rest[0]
        rest = rest[1:]
    if aliased:
        rest = rest[1:]
    y_ref, so_ref, kp_ref, gp_ref, vp_ref, st_ref, oi_ref = rest
    c = pl.program_id(1)
    nc = pl.num_programs(1)
    nb = Lc // HG_BLOCK

    @pl.when(c == 0)
    def _init():
        for h in range(HG_HEADS):
            st_ref[h] = s0_ref[0, h].T if has_state else jnp.zeros((HG_HD, HG_HD), F32)
        kp_ref[0:HG_BLOCK, :] = jnp.zeros((HG_BLOCK, HG_WIDTH), F32)
        gp_ref[0:HG_BLOCK, :] = jnp.zeros((HG_BLOCK, HG_WIDTH), F32)
        vp_ref[0:HG_BLOCK, :] = jnp.zeros((HG_BLOCK, HG_WIDTH), F32)

    zf = f_ref[...]
    lb = lb_ref[...]
    f = lb + (1.0 - lb) * _sigmoid(zf)
    logf = jnp.log(jnp.maximum(f, TINY))
    k = (1.0 - lb) * _sigmoid(-zf)
    q = _silu(q_ref[...]) * (HG_HD ** -0.5)
    v = i_ref[...]
    vb = v.astype(BF16)
    gate = _silu(g_ref[...])
    ri = lax.broadcasted_iota(jnp.int32, (Lc, Lc), 0)
    ci = lax.broadcasted_iota(jnp.int32, (Lc, Lc), 1)

    def log_decays(blk):
        same = (ri // blk) == (ci // blk)
        ltri = jnp.where(same, jnp.where(ri >= ci, 1.0, 0.0), 0.0)
        lall = jnp.where(same, 1.0, 0.0)
        return (ltri, jnp.dot(ltri, logf, precision=HIGHEST, preferred_element_type=F32),
                jnp.dot(lall, logf, precision=HIGHEST, preferred_element_type=F32))

    def pass_state(blk, g_cum, g_tot, intra):
        qg = (q * jnp.exp(g_cum)).astype(BF16)
        kdec = (k * jnp.exp(g_tot - g_cum)).astype(BF16)
        cd = jnp.exp(g_tot)
        for h in range(HG_HEADS):
            hs = slice(h * HG_HD, (h + 1) * HG_HD)
            st = st_ref[h]
            outs = []
            for b in range(Lc // blk):
                rs = slice(b * blk, (b + 1) * blk)
                outs.append(lax.dot_general(qg[rs, hs], st.astype(BF16), NT_DIMS, preferred_element_type=F32))
                st = st * cd[b * blk:b * blk + 1, hs] + lax.dot_general(
                    vb[rs, hs], kdec[rs, hs], TN_DIMS, preferred_element_type=F32)
            st_ref[h] = st
            o = intra(hs, qg) + (jnp.concatenate(outs, axis=0) if len(outs) > 1 else outs[0])
            ms = jnp.mean(o * o, axis=-1, keepdims=True)
            y_ref[:, hs] = (o * lax.rsqrt(ms + EPS) * gn_ref[:, hs] * gate[:, hs]).astype(y_ref.dtype)

    blk_long = min(HG_LONG_BLOCK, Lc)
    ltri_l, g_l, gt_l = log_decays(blk_long)
    worst = jnp.max(-gt_l)

    @pl.when(worst < HG_SAFE_DECAY)
    def _factorised():
        kgrow = (k * jnp.exp(-g_l)).astype(BF16)
        in_block = ltri_l > 0.5

        def intra(hs, qg):
            att = lax.dot_general(qg[:, hs], kgrow[:, hs], NT_DIMS, preferred_element_type=F32)
            att = jnp.where(in_block, att, 0.0).astype(BF16)
            return jnp.dot(att, vb[:, hs], preferred_element_type=F32)

        pass_state(blk_long, g_l, gt_l, intra)

    @pl.when(worst >= HG_SAFE_DECAY)
    def _unfactorised():
        _, gb, gl = log_decays(HG_BLOCK)
        kp_ref[HG_BLOCK:HG_BLOCK + Lc, :] = k
        gp_ref[HG_BLOCK:HG_BLOCK + Lc, :] = gb
        vp_ref[HG_BLOCK:HG_BLOCK + Lc, :] = v
        tpos = lax.broadcasted_iota(jnp.int32, (Lc, HG_WIDTH), 0) % HG_BLOCK
        pw = 2 * HG_HD
        oi = lax.broadcasted_iota(jnp.int32, (pw, pw), 0) // HG_HD
        oj = lax.broadcasted_iota(jnp.int32, (pw, pw), 1) // HG_HD
        ones2 = jnp.where(oi == oj, 1.0, 0.0).astype(BF16)
        o_pair = [jnp.zeros((Lc, pw), F32) for _ in range(HG_HEADS // 2)]
        for d in range(HG_BLOCK):
            lo = HG_BLOCK - d
            kd = kp_ref[lo:lo + Lc, :]
            gd = gp_ref[lo:lo + Lc, :]
            vd = vp_ref[lo:lo + Lc, :]
            m = tpos >= d
            dec = jnp.exp(jnp.where(m, gb - gd, 0.0))
            pr = jnp.where(m, q * kd * dec, 0.0).astype(BF16)
            for hp in range(HG_HEADS // 2):
                ps = slice(hp * pw, (hp + 1) * pw)
                o_pair[hp] = o_pair[hp] + jnp.dot(pr[:, ps], ones2, preferred_element_type=F32) * vd[:, ps]
        for hp in range(HG_HEADS // 2):
            oi_ref[:, hp * pw:(hp + 1) * pw] = o_pair[hp]
        pass_state(HG_BLOCK, gb, gl, lambda hs, qg: oi_ref[:, hs])

    @pl.when(c == nc - 1)
    def _fin():
        for h in range(HG_HEADS):
            so_ref[0, h] = st_ref[h].T


def _hgrn(proj, p, lb, *, row0, B, T, Lc, s0=None, layer=None, ycat=None):
    nc = T // Lc
    rb0 = row0 // Lc
    has_state = s0 is not None

    def rows(col):
        return lambda b, c: (rb0 + b * nc + c, col // HG_WIDTH)

    const = lambda b, c: (0, 0)
    args = [proj, proj, proj, proj, lb, p['hgrn_norm']]
    specs = [pl.BlockSpec((Lc, HG_WIDTH), rows(COL_HQ)), pl.BlockSpec((Lc, HG_WIDTH), rows(COL_HF)),
             pl.BlockSpec((Lc, HG_WIDTH), rows(COL_HI)), pl.BlockSpec((Lc, HG_WIDTH), rows(COL_HG)),
             pl.BlockSpec((1, HG_WIDTH), const), pl.BlockSpec((1, HG_WIDTH), const)]
    if has_state:
        args.append(s0)
        specs.append(pl.BlockSpec((None, 1, HG_HEADS, HG_HD, HG_HD), lambda b, c: (layer, b, 0, 0, 0)))
    yshape, aliases = _mixed_out(ycat, proj.shape[0], args, specs)
    y, so = pl.pallas_call(
        functools.partial(_hgrn_kernel, Lc=Lc, has_state=has_state, aliased=ycat is not None),
        grid=(B, nc),
        in_specs=specs,
        out_specs=[pl.BlockSpec((Lc, HG_WIDTH), rows(YCOL_HG)),
                   pl.BlockSpec((1, HG_HEADS, HG_HD, HG_HD), lambda b, c: (b, 0, 0, 0))],
        out_shape=[yshape, jax.ShapeDtypeStruct((B, HG_HEADS, HG_HD, HG_HD), F32)],
        scratch_shapes=[pltpu.VMEM((HG_BLOCK + Lc, HG_WIDTH), F32)] * 3
                       + [pltpu.VMEM((HG_HEADS, HG_HD, HG_HD), F32), pltpu.VMEM((Lc, HG_WIDTH), F32)],
        input_output_aliases=aliases,
        compiler_params=_cparams(("parallel", "arbitrary")),
        name="hgrn",
    )(*args)
    return y, so


def _swa_kernel(*refs, Tq, prompt, aliased):
    sink_ref, q_ref, k_ref, v_ref, qg_ref, kg_ref = refs[:6]
    rest = refs[6:]
    if not prompt:
        ck_ref, cv_ref = rest[:2]
        rest = rest[2:]
    if aliased:
        rest = rest[1:]
    y_ref, ko_ref, vo_ref, bias_ref = rest[:4]
    if prompt:
        kp_ref, vp_ref = rest[4:]
    c = pl.program_id(1)
    Tk = WINDOW + Tq
    bi = lax.broadcasted_iota(jnp.int32, (LANE, LANE), 0) // SWA_HD
    bj = lax.broadcasted_iota(jnp.int32, (LANE, LANE), 1) // SWA_HD
    bd = jnp.where(bi == bj, 1.0, 0.0).astype(BF16)

    def hnorm(x, gain):
        outs = []
        for j in range(x.shape[1] // LANE):
            xs = x[:, j * LANE:(j + 1) * LANE]
            ms = _split_dot(xs * xs, bd) * (1.0 / SWA_HD)
            outs.append(xs * lax.rsqrt(ms + EPS))
        return (jnp.concatenate(outs, axis=1) if len(outs) > 1 else outs[0]) * gain

    qn = hnorm(q_ref[...], qg_ref[...])
    kn = hnorm(k_ref[...], kg_ref[...])
    vn = v_ref[...]
    if prompt:
        @pl.when(c == 0)
        def _init():
            kp_ref[...] = jnp.zeros((WINDOW, LANE), F32)
            vp_ref[...] = jnp.zeros((WINDOW, LANE), F32)
        kprev = kp_ref[...]
        vprev = vp_ref[...]
    else:
        kprev = ck_ref[0]
        vprev = cv_ref[0]
    kall = jnp.concatenate([kprev, kn], axis=0)
    vall = jnp.concatenate([vprev, vn], axis=0)
    knew = kall[Tq:, :]
    vnew = vall[Tq:, :]
    ko_ref[0] = knew
    vo_ref[0] = vnew
    if prompt:
        kp_ref[...] = knew
        vp_ref[...] = vnew
    ksw = pltpu.roll(kall, SWA_HD, 1)
    vsw = pltpu.roll(vall, SWA_HD, 1)
    lowk = lax.broadcasted_iota(jnp.int32, (Tk, LANE), 1) < SWA_HD
    R = SWA_REP * Tq
    rows_i = lax.broadcasted_iota(jnp.int32, (R, Tk), 0)
    cols_i = lax.broadcasted_iota(jnp.int32, (R, Tk), 1)
    rep_i = rows_i // Tq
    rep_c = lax.broadcasted_iota(jnp.int32, (R, 1), 0) // Tq

    @pl.when(c == 0)
    def _bias():
        dist = jnp.abs(rows_i % Tq + WINDOW - cols_i).astype(F32)
        for g in range(SWA_KV_HEADS):
            slope = jnp.zeros((R, Tk), F32)
            for r in range(SWA_REP):
                slope = jnp.where(rep_i == r, 2.0 ** (-8.0 * (g * SWA_REP + r + 1) / SWA_HEADS), slope)
            bias_ref[g] = slope * dist

    valid = (cols_i + c * Tq) >= WINDOW
    lane = lax.broadcasted_iota(jnp.int32, (Tq, LANE), 1)
    low = lane < SWA_HD
    for g in range(SWA_KV_HEADS):
        parts = []
        for r in range(SWA_REP):
            hd = g * SWA_REP + r
            qp = qn[:, (hd // 2) * LANE:(hd // 2 + 1) * LANE]
            parts.append(jnp.where(low if hd % 2 == 0 else jnp.logical_not(low), qp, 0.0))
        qst = jnp.concatenate(parts, axis=0).astype(BF16)
        kboth = (jnp.where(lowk, kall, ksw) if g == 0 else jnp.where(lowk, ksw, kall)).astype(BF16)
        vboth = (jnp.where(lowk, vall, vsw) if g == 0 else jnp.where(lowk, vsw, vall)).astype(BF16)
        sink = jnp.zeros((R, 1), F32)
        for r in range(SWA_REP):
            sink = jnp.where(rep_c == r, sink_ref[g * SWA_REP + r], sink)
        s = lax.dot_general(qst, kboth, NT_DIMS, preferred_element_type=F32) * (SWA_HD ** -0.5)
        s = s - bias_ref[g]
        if prompt:
            s = jnp.where(valid, s, NEG)
        mx = jnp.maximum(jnp.max(s, axis=-1, keepdims=True), sink)
        pe = jnp.exp(s - mx)
        if prompt:
            pe = jnp.where(valid, pe, 0.0)
        inv = 1.0 / (jnp.sum(pe, axis=-1, keepdims=True) + jnp.exp(sink - mx))
        o = jnp.dot((pe * inv).astype(BF16), vboth, preferred_element_type=F32)
        for jj in range(SWA_REP // 2):
            pair = g * (SWA_REP // 2) + jj
            y_ref[:, pair * LANE:(pair + 1) * LANE] = jnp.where(
                low, o[(2 * jj) * Tq:(2 * jj + 1) * Tq, :], o[(2 * jj + 1) * Tq:(2 * jj + 2) * Tq, :]
            ).astype(y_ref.dtype)


def _swa(proj, p, *, row0, B, T, ck=None, cv=None, ycat=None):
    prompt = ck is None
    Tq = CHUNK if prompt else T
    nc = T // Tq
    rb0 = row0 // Tq

    def rows(col, width):
        return lambda b, c: (rb0 + b * nc + c, col // width)

    const = lambda b, c: (0, 0)
    args = [p['swa_sinks'], proj, proj, proj, p['swa_qnorm'], p['swa_knorm']]
    specs = [pl.BlockSpec(memory_space=pltpu.SMEM),
             pl.BlockSpec((Tq, 1024), rows(COL_SQ, 1024)),
             pl.BlockSpec((Tq, LANE), rows(COL_SK, LANE)),
             pl.BlockSpec((Tq, LANE), rows(COL_SV, LANE)),
             pl.BlockSpec((1, 1024), const), pl.BlockSpec((1, LANE), const)]
    if not prompt:
        args += [ck.reshape(B, WINDOW, LANE), cv.reshape(B, WINDOW, LANE)]
        specs += [pl.BlockSpec((1, WINDOW, LANE), lambda b, c: (b, 0, 0))] * 2
    yshape, aliases = _mixed_out(ycat, proj.shape[0], args, specs)
    y, ko, vo = pl.pallas_call(
        functools.partial(_swa_kernel, Tq=Tq, prompt=prompt, aliased=ycat is not None),
        grid=(B, nc),
        in_specs=specs,
        out_specs=[pl.BlockSpec((Tq, 1024), rows(YCOL_SWA, 1024)),
                   pl.BlockSpec((1, WINDOW, LANE), lambda b, c: (b, 0, 0)),
                   pl.BlockSpec((1, WINDOW, LANE), lambda b, c: (b, 0, 0))],
        out_shape=[yshape,
                   jax.ShapeDtypeStruct((B, WINDOW, LANE), F32),
                   jax.ShapeDtypeStruct((B, WINDOW, LANE), F32)],
        scratch_shapes=[pltpu.VMEM((SWA_KV_HEADS, SWA_REP * Tq, WINDOW + Tq), F32)]
                       + ([pltpu.VMEM((WINDOW, LANE), F32)] * 2 if prompt else []),
        input_output_aliases=aliases,
        compiler_params=_cparams(("parallel", "arbitrary")),
        name="swa",
    )(*args)
    shp = (B, WINDOW, SWA_KV_HEADS, SWA_HD)
    return y, ko.reshape(shp), vo.reshape(shp)


def _mem_kernel(q_ref, k_ref, v_ref, *rest):
    o_ref = rest[-1]
    for h in range(MEM_HEADS):
        hs = slice(h * MEM_HD, (h + 1) * MEM_HD)
        s = lax.dot_general(q_ref[:, hs].astype(BF16), k_ref[:, hs].astype(BF16), NT_DIMS,
                            preferred_element_type=F32) * (MEM_HD ** -0.5)
        mx = jnp.max(s, axis=-1, keepdims=True)
        pe = jnp.exp(s - mx)
        pe = pe / jnp.sum(pe, axis=-1, keepdims=True)
        o_ref[:, hs] = jnp.dot(pe.astype(BF16), v_ref[:, hs].astype(BF16),
                               preferred_element_type=F32).astype(o_ref.dtype)


def _mem_attend(qn, mk, mv, *, row0, B, T, tq, layer=None, out=None):
    nt = T // tq
    rb0 = row0 // tq
    args = [qn, mk, mv]
    if layer is None:
        kv = pl.BlockSpec((MEM_TOKENS, MEM_WIDTH), lambda b, t: (b, 0))
    else:
        kv = pl.BlockSpec((None, MEM_TOKENS, MEM_WIDTH), lambda b, t: (layer, b, 0))
    specs = [pl.BlockSpec((tq, MEM_WIDTH), lambda b, t: (rb0 + b * nt + t, 0)), kv, kv]
    aliases = {}
    if out is not None:
        args.append(out)
        specs.append(pl.BlockSpec(memory_space=pl.ANY))
        aliases = {3: 0}
    return pl.pallas_call(
        _mem_kernel,
        grid=(B, nt),
        in_specs=specs,
        out_specs=pl.BlockSpec((tq, MEM_WIDTH), lambda b, t: (rb0 + b * nt + t, 0)),
        out_shape=jax.ShapeDtypeStruct((qn.shape[0], MEM_WIDTH), BF16),
        input_output_aliases=aliases,
        compiler_params=_cparams(("parallel", "arbitrary")),
        name="mem_attend",
    )(*args)


def _route_kernel(pq_ref, keys_ref, s1_ref, e1_ref, s2_ref, e2_ref, tau_ref):
    half = PEER_QDIM // 2
    ninf = -jnp.inf
    K = PEER_TOPK

    def top_distinct(s):
        cur, vals = s, []
        for _ in range(K):
            mx = jnp.max(cur, axis=0, keepdims=True)
            vals.append(mx)
            cur = jnp.where(cur == mx, ninf, cur)
        return vals

    def pick_first_max(cur, rows):
        mx = jnp.max(cur, axis=0, keepdims=True)
        first = jnp.min(jnp.where(cur == mx, rows, float(cur.shape[0])), axis=0, keepdims=True)
        return mx, rows == first

    def top_ranked(s):
        rows = lax.broadcasted_iota(jnp.int32, s.shape, 0).astype(F32)
        cur, vals, rank = s, [], jnp.full(s.shape, float(K), F32)
        for it in range(K):
            mx, hit = pick_first_max(cur, rows)
            vals.append(mx)
            rank = jnp.where(hit, float(it), rank)
            cur = jnp.where(hit, ninf, cur)
        return vals, rank

    def count_ge(x, thr):
        return jnp.sum(jnp.where(x >= thr, 1.0, 0.0), axis=0, keepdims=True)

    def all_cands(v1, v2m):
        return jnp.concatenate([v1[r] + v2m for r in range(K)], axis=0)

    def finish(h, s1, s2, v1, v2m, cand, tau):
        m0 = v1[0] + v2m[0:1, :]
        zsum = jnp.sum(jnp.where(cand >= tau, jnp.exp(cand - m0), 0.0), axis=0, keepdims=True)
        s1_ref[h] = jnp.where(s1 >= v1[K - 1], s1, ninf)
        s2_ref[h] = jnp.where(s2 >= v2m[K - 1:K, :], s2, ninf)
        e1_ref[h] = jnp.exp(s1 - v1[0])
        e2_ref[h] = jnp.exp(s2 - v2m[0:1, :]) / zsum
        tau_ref[h:h + 1, :] = tau

    tm = pq_ref.shape[0]

    def scores(h):
        q1 = pq_ref[:, h * PEER_QDIM:h * PEER_QDIM + half]
        q2 = pq_ref[:, h * PEER_QDIM + half:(h + 1) * PEER_QDIM]
        return (lax.dot_general(keys_ref[h, 0], q1, NT_DIMS, precision=HIGHEST, preferred_element_type=F32),
                lax.dot_general(keys_ref[h, 1], q2, NT_DIMS, precision=HIGHEST, preferred_element_type=F32))

    any_tie = []
    for h in range(PEER_HEADS):
        s1, s2 = scores(h)
        v1 = top_distinct(s1)
        v2 = top_distinct(s2)
        v2m = jnp.concatenate(v2, axis=0)
        rowi = lax.broadcasted_iota(jnp.int32, (8, tm), 0)
        blocks = [v1[0] + v2m, v1[1] + v2m[0:8, :]]
        for r in range(2, 8):
            blocks.append(jnp.where(rowi < K // (r + 1), v1[r] + v2m[0:8, :], ninf))
        blocks.append(jnp.concatenate(v1[8:], axis=0) + v2m[0:1, :])
        cand = jnp.concatenate(blocks, axis=0)
        cur = cand
        for _ in range(K):
            tau = jnp.max(cur, axis=0, keepdims=True)
            cur = jnp.where(cur == tau, ninf, cur)
        full = all_cands(v1, v2m)
        tied = jnp.where(count_ge(full, tau) == K,
                         jnp.where(count_ge(s1, v1[K - 1]) == K,
                                   jnp.where(count_ge(s2, v2[K - 1]) == K, 0.0, 1.0), 1.0), 1.0)
        finish(h, s1, s2, v1, v2m, cand, tau)
        any_tie.append(jnp.max(tied))

    for h in range(PEER_HEADS):
        @pl.when(any_tie[h] > 0.0)
        def _with_ties():
            s1, s2 = scores(h)
            w1, rank1 = top_ranked(s1)
            w2, rank2 = top_ranked(s2)
            w2m = jnp.concatenate(w2, axis=0)
            allc = all_cands(w1, w2m)
            rows = lax.broadcasted_iota(jnp.int32, allc.shape, 0).astype(F32)
            cur, picked = allc, jnp.zeros(allc.shape, F32)
            for _ in range(K):
                _, hit = pick_first_max(cur, rows)
                picked = jnp.where(hit, 1.0, picked)
                cur = jnp.where(hit, ninf, cur)
            m0 = w1[0] + w2m[0:1, :]
            zsum = jnp.sum(picked * jnp.exp(allc - m0), axis=0, keepdims=True)
            lmap = jnp.zeros(s1.shape, F32)
            for r in range(K):
                length = jnp.sum(picked[r * K:(r + 1) * K, :], axis=0, keepdims=True)
                lmap = jnp.where(rank1 == float(r), length, lmap)
            s1_ref[h] = lmap
            s2_ref[h] = jnp.where(rank2 < float(K), -rank2, -1000.0)
            e1_ref[h] = jnp.exp(s1 - w1[0])
            e2_ref[h] = jnp.exp(s2 - w2m[0:1, :]) / zsum
            tau_ref[h:h + 1, :] = jnp.ones((1, tm), F32)


def _peer_route(pq, keys, *, tm=256):
    N = pq.shape[0]
    big = jax.ShapeDtypeStruct((PEER_HEADS, N_KEYS, N), F32)
    bspec = pl.BlockSpec((PEER_HEADS, N_KEYS, tm), lambda i: (0, 0, i))
    return pl.pallas_call(
        _route_kernel,
        grid=(N // tm,),
        in_specs=[pl.BlockSpec((tm, PEER_HEADS * PEER_QDIM), lambda i: (i, 0)),
                  pl.BlockSpec((PEER_HEADS, 2, N_KEYS, PEER_QDIM // 2), lambda i: (0, 0, 0, 0))],
        out_specs=[bspec, bspec, bspec, bspec, pl.BlockSpec((PEER_HEADS, tm), lambda i: (0, i))],
        out_shape=[big, big, big, big, jax.ShapeDtypeStruct((PEER_HEADS, N), F32)],
        compiler_params=_cparams(("parallel",)),
        name="peer_route",
    )(pq, keys)


def _dense_kernel(h_hbm, g_ref, u_ref, v_ref, s1_ref, e1_ref, s2_ref, e2_ref, tau_ref, o_ref, xn_ref, c_ref,
                  sem, *, te):
    i = pl.program_id(0)
    j = pl.program_id(1)
    na = te // N_KEYS
    tm = o_ref.shape[0]

    @pl.when(j == 0)
    def _init():
        fetch = pltpu.make_async_copy(h_hbm.at[pl.ds(i * tm, tm), :], o_ref, sem)
        fetch.start()
        fetch.wait()
        x = o_ref[...]
        ms = jnp.mean(x * x, axis=-1, keepdims=True)
        xn_ref[...] = (x * lax.rsqrt(ms + EPS) * g_ref[...]).astype(BF16)

    pre = lax.dot_general(xn_ref[...], u_ref[...], NT_DIMS, preferred_element_type=F32)
    act = 0.5 * pre * (1.0 + lax.erf(pre * (2.0 ** -0.5)))
    for al in range(na):
        a = j * na + al
        gt = jnp.zeros((N_KEYS, tm), F32)
        for h in range(PEER_HEADS):
            hit = (s1_ref[h, pl.ds(a, 1), :] + s2_ref[h]) >= tau_ref[h:h + 1, :]
            gt = gt + jnp.where(hit, e1_ref[h, pl.ds(a, 1), :] * e2_ref[h], 0.0)
        c_ref[:, al * N_KEYS:(al + 1) * N_KEYS] = (gt.T * act[:, al * N_KEYS:(al + 1) * N_KEYS]).astype(BF16)
    o_ref[...] += jnp.dot(c_ref[...], v_ref[...], preferred_element_type=F32)


def _peer_dense(h, gain, u, v, s1, e1, s2, e2, tau, *, tm=768, te=512):
    N, D = h.shape
    nj = N_EXPERTS // te
    assert N % tm == 0 and N_EXPERTS % te == 0
    one = pl.Buffered(1)
    route = pl.BlockSpec((PEER_HEADS, N_KEYS, tm), lambda i, j: (0, 0, i), pipeline_mode=one)
    return pl.pallas_call(
        functools.partial(_dense_kernel, te=te),
        grid=(N // tm, nj),
        in_specs=[pl.BlockSpec(memory_space=pl.ANY),
                  pl.BlockSpec((1, D), lambda i, j: (0, 0)),
                  pl.BlockSpec((te, D), lambda i, j: (j, 0)),
                  pl.BlockSpec((te, D), lambda i, j: (j, 0)),
                  route, route, route, route,
                  pl.BlockSpec((PEER_HEADS, tm), lambda i, j: (0, i))],
        out_specs=pl.BlockSpec((tm, D), lambda i, j: (i, 0), pipeline_mode=one),
        out_shape=jax.ShapeDtypeStruct((N, D), F32),
        scratch_shapes=[pltpu.VMEM((tm, D), BF16), pltpu.VMEM((tm, te), BF16), pltpu.SemaphoreType.DMA(())],
        compiler_params=_cparams(("parallel", "arbitrary")),
        name="peer_dense",
    )(h, gain.reshape(1, D).astype(F32), u, v, s1, e1, s2, e2, tau)


def _row(x, width=None):
    x = x.astype(F32).reshape(1, -1)
    if width is not None and x.shape[1] < width:
        x = jnp.pad(x, ((0, 0), (0, width - x.shape[1])))
    return x


def _cast_kernel(x_ref, o_ref):
    o_ref[...] = x_ref[...].astype(o_ref.dtype)


def _to_bf16(w, *, block_bytes=8 * 1024 * 1024):
    depth, rows, cols = w.shape
    tr = max(8, min(rows, block_bytes // (cols * 4)))
    assert rows % tr == 0
    return pl.pallas_call(
        _cast_kernel,
        grid=(depth, rows // tr),
        in_specs=[pl.BlockSpec((None, tr, cols), lambda l, r: (l, r, 0))],
        out_specs=pl.BlockSpec((None, tr, cols), lambda l, r: (l, r, 0)),
        out_shape=jax.ShapeDtypeStruct(w.shape, BF16),
        compiler_params=_cparams(("parallel", "parallel")),
        name="to_bf16",
    )(w)


PACK_ROWS = 256
PACK_GROUP = SSD_HEADS


def _pack_kernel(w_ref, dt_ref, o_ref):
    r = pl.program_id(1)
    last = pl.num_programs(1) - 1

    @pl.when(r < last)
    def _copy():
        o_ref[...] = w_ref[...].reshape(PACK_ROWS, w_ref.shape[-1]).astype(BF16)

    @pl.when(r == last)
    def _dt_and_pad():
        o_ref[0:PACK_GROUP, :] = dt_ref[...].astype(BF16)
        o_ref[PACK_GROUP:, :] = jnp.zeros((PACK_ROWS - PACK_GROUP, o_ref.shape[-1]), BF16)


def _pack_w_in(w_in):
    depth, d, cols = w_in.shape
    xbc_end = 3 * SSD_INNER
    assert cols == COL_DT + SSD_HEADS and cols % PACK_GROUP == 0 and PROJ_COLS % PACK_ROWS == 0
    gpb = PACK_ROWS // PACK_GROUP
    n_groups = cols // PACK_GROUP
    n_head = xbc_end // PACK_ROWS
    skip = (xbc_end + SSD_HEADS) // PACK_GROUP
    wt = jnp.swapaxes(w_in, 1, 2).reshape(depth, n_groups, PACK_GROUP, d)

    def src_group(l, r):
        g = jnp.where(r < n_head, r * gpb, skip + (r - n_head) * gpb)
        return (l, jnp.minimum(g, n_groups - gpb), 0, 0)

    return pl.pallas_call(
        _pack_kernel,
        grid=(depth, PROJ_COLS // PACK_ROWS),
        in_specs=[pl.BlockSpec((pl.Element(1), pl.Element(gpb), pl.Element(PACK_GROUP), pl.Element(d)),
                               src_group),
                  pl.BlockSpec((None, None, PACK_GROUP, d), lambda l, r: (l, xbc_end // PACK_GROUP, 0, 0))],
        out_specs=pl.BlockSpec((None, PACK_ROWS, d), lambda l, r: (l, r, 0)),
        out_shape=jax.ShapeDtypeStruct((depth, PROJ_COLS, d), BF16),
        compiler_params=_cparams(("parallel", "arbitrary")),
        name="pack_w_in",
    )(wt, wt)


def _layer_params(l, conv_w, conv_b, dt_bias, a_log, d_skip, ssd_norm, hgrn_norm, swa_qnorm,
                  swa_knorm, swa_sinks):
    eh = lax.broadcasted_iota(jnp.int32, (LANE, SSD_INNER), 0)
    ec = lax.broadcasted_iota(jnp.int32, (LANE, SSD_INNER), 1)
    return {
        'expand': (ec // SSD_HD == eh).astype(BF16),
        'cwx': conv_w[l][:, :SSD_INNER].astype(F32), 'cwb': conv_w[l][:, SSD_INNER:].astype(F32),
        'cbx': _row(conv_b[l][:SSD_INNER]), 'cbb': _row(conv_b[l][SSD_INNER:]),
        'dtb': _row(dt_bias[l], LANE), 'alog': _row(a_log[l], LANE),
        'dsk': _row(jnp.repeat(d_skip[l], SSD_HD)), 'ssd_norm': _row(ssd_norm[l]),
        'hgrn_norm': _row(jnp.tile(hgrn_norm[l], HG_HEADS)),
        'swa_qnorm': _row(jnp.tile(swa_qnorm[l], SWA_HEADS)),
        'swa_knorm': _row(jnp.tile(swa_knorm[l], SWA_KV_HEADS)),
        'swa_sinks': swa_sinks[l].astype(F32),
    }


def kernel(x_prompt, x_sample, mem_prompt, state_ssm, state_ssd_conv, state_hgrn, cache_swa_k, cache_swa_v,
           cache_mem_k, cache_mem_v, norm_mix, w_in, conv_w, conv_b, dt_bias, a_log, d_skip, ssd_norm,
           hgrn_lb, hgrn_norm, swa_qnorm, swa_knorm, swa_sinks, w_out, norm_mem, norm_memtok, w_mq, w_mk,
           w_mv, mem_qnorm, mem_knorm, w_mo, norm_ffn, w_pq, peer_keys, peer_u, peer_v):
    depth = w_in.shape[0]
    bp, tp, _ = x_prompt.shape
    bs, ts, _ = x_sample.shape
    n_p, n_s = bp * tp, bs * ts
    lbp = jax.nn.softmax(hgrn_lb.astype(F32), axis=0)
    lower = jnp.cumsum(lbp, axis=0) - lbp[0:1]
    h = jnp.concatenate([x_prompt.reshape(n_p, D_MODEL), x_sample.reshape(n_s, D_MODEL)], axis=0)
    mem2d = mem_prompt.reshape(bp * MEM_TOKENS, D_MODEL)
    outs = {k: [] for k in ('ssm_p', 'conv_p', 'hg_p', 'swk_p', 'swv_p', 'mk_p', 'mv_p',
                            'ssm_s', 'conv_s', 'hg_s', 'swk_s', 'swv_s')}
    w_in_b = _pack_w_in(w_in)
    w_out_b, w_mq_b, w_mk_b, w_mv_b, w_mo_b, w_pq_b = (
        _to_bf16(t) for t in (w_out, w_mq, w_mk, w_mv, w_mo, w_pq))
    for l in range(depth):
        p = _layer_params(l, conv_w, conv_b, dt_bias, a_log, d_skip, ssd_norm, hgrn_norm,
                          swa_qnorm, swa_knorm, swa_sinks)
        lb = _row(lower[l])
        proj, u_b, v_b = _mm(h, w_in_b, layer=l, w_nk=True, gain=norm_mix[l], ride=(peer_u, peer_v),
                             name="proj_in")
        ycat = jnp.zeros((n_p + n_s, D_MODEL), BF16)
        ycat, conv_p, ssm_p = _ssd(proj, p, row0=0, B=bp, T=tp, ycat=ycat)
        ycat, conv_s, ssm_s = _ssd(proj, p, row0=n_p, B=bs, T=ts, conv0=state_ssd_conv, h0=state_ssm,
                                   layer=l, ycat=ycat)
        ycat, hg_p = _hgrn(proj, p, lb, row0=0, B=bp, T=tp, Lc=128, ycat=ycat)
        ycat, hg_s = _hgrn(proj, p, lb, row0=n_p, B=bs, T=ts, Lc=ts, s0=state_hgrn, layer=l, ycat=ycat)
        ycat, swk_p, swv_p = _swa(proj, p, row0=0, B=bp, T=tp, ycat=ycat)
        ycat, swk_s, swv_s = _swa(proj, p, row0=n_p, B=bs, T=ts, ck=cache_swa_k[l], cv=cache_swa_v[l],
                                  ycat=ycat)
        h = _mm(ycat, w_out_b, layer=l, res=h, name="proj_out")
        mk = _mm(mem2d, w_mk_b, layer=l, gain=norm_memtok[l], head_gain=mem_knorm[l], name="mem_k")
        mv = _mm(mem2d, w_mv_b, layer=l, gain=norm_memtok[l], name="mem_v")
        qn = _mm(h, w_mq_b, layer=l, gain=norm_mem[l], head_gain=mem_qnorm[l], name="mem_q")
        om = _mem_attend(qn, mk, mv, row0=0, B=bp, T=tp, tq=512,
                         out=jnp.zeros((n_p + n_s, MEM_WIDTH), BF16))
        om = _mem_attend(qn, cache_mem_k.reshape(depth, bs * MEM_TOKENS, MEM_WIDTH),
                         cache_mem_v.reshape(depth, bs * MEM_TOKENS, MEM_WIDTH), layer=l,
                         row0=n_p, B=bs, T=ts, tq=ts,
                         out=om)
        h = _mm(om, w_mo_b, layer=l, res=h, name="mem_o")
        pq = _mm(h, w_pq_b, layer=l, gain=norm_ffn[l], name="peer_q")
        s1, e1, s2, e2, tau = _peer_route(pq, peer_keys[l].astype(F32))
        h = _peer_dense(h, norm_ffn[l], u_b, v_b, s1, e1, s2, e2, tau)
        for name, val in (('ssm_p', ssm_p), ('conv_p', conv_p), ('hg_p', hg_p), ('swk_p', swk_p),
                          ('swv_p', swv_p), ('ssm_s', ssm_s), ('conv_s', conv_s), ('hg_s', hg_s),
                          ('swk_s', swk_s), ('swv_s', swv_s)):
            outs[name].append(val)
        outs['mk_p'].append(mk.reshape(bp, MEM_TOKENS, MEM_HEADS, MEM_HD))
        outs['mv_p'].append(mv.reshape(bp, MEM_TOKENS, MEM_HEADS, MEM_HD))
    st = lambda k: jnp.stack(outs[k])
    return (h[:n_p].reshape(bp, tp, D_MODEL), h[n_p:].reshape(bs, ts, D_MODEL),
            st('ssm_p'), st('conv_p'), st('hg_p'), st('swk_p'), st('swv_p'), st('mk_p'), st('mv_p'),
            st('ssm_s'), st('conv_s'), st('hg_s'), st('swk_s'), st('swv_s'))
```
